```python
import math
import jax, jax.numpy as jnp
from jax import lax
import numpy as np

D_MODEL = 1024
BATCH = 1
SEQ = 16384
DEPTH = 2
DEC_BATCH = 32
DEC_SEQ = 1
PAST_LEN = 16384
PAGE_SIZE = 128

D_CONV = D_MODEL
CONV_K = 31
D_INNER = 2 * D_MODEL
SSM_HEADDIM = 64
SSM_HEADS = D_INNER // SSM_HEADDIM
SSM_GROUPS = 4
D_STATE = 128
SSM_CONV_K = 4
SSM_CONV_DIM = D_INNER + 2 * SSM_GROUPS * D_STATE
SSM_CHUNK = 128
N_HEADS = 16
HEAD_DIM = 64
KV_GROUPS = 2
BLOCK = 64
TOP_N = 16
WINDOW = 512
CMP_HIDDEN = 256
Q_BLOCK = 128
ROPE_THETA = 500000.0
ROT_DIM = HEAD_DIM // 4
BIG = 1e9
D_FF = ((8 * D_MODEL + 3 * 256 - 1) // (3 * 256)) * 256
EPS = 1e-6
SPLITS = [2 * D_CONV, D_INNER, SSM_CONV_DIM, SSM_HEADS, N_HEADS * HEAD_DIM] + [KV_GROUPS * HEAD_DIM] * 6 + [3 * N_HEADS, 3 * D_MODEL]
N_IN = sum(SPLITS)

kernel_name = 'hybrid_conformer_ssd_nsa_adaln_step'


def rmsnorm(x, g):
    xf = x.astype(jnp.float32)
    y = xf * lax.rsqrt(jnp.mean(xf * xf, -1, keepdims=True) + EPS)
    return (y * g.astype(jnp.float32)).astype(x.dtype)


def layernorm(x, g, b):
    xf = x.astype(jnp.float32)
    xc = xf - jnp.mean(xf, -1, keepdims=True)
    y = xc * lax.rsqrt(jnp.mean(xc * xc, -1, keepdims=True) + EPS)
    return (y * g.astype(jnp.float32) + b.astype(jnp.float32)).astype(x.dtype)


def causal_dwconv(x_ext, w, b):
    ch = x_ext.shape[-1]
    out = lax.conv_general_dilated(x_ext, w.astype(x_ext.dtype)[:, None, :], window_strides=(1,), padding='VALID',
                                   dimension_numbers=('NWC', 'WIO', 'NWC'), feature_group_count=ch)
    return out + b.astype(x_ext.dtype)


def apply_rope(x, pos):
    half = ROT_DIM // 2
    inv = ROPE_THETA ** (-(jnp.arange(half, dtype=jnp.float32) * 2.0 / ROT_DIM))
    ang = pos.astype(jnp.float32)[:, None] * inv[None, :]
    cos = jnp.cos(ang)[None, :, None, :]
    sin = jnp.sin(ang)[None, :, None, :]
    xf = x.astype(jnp.float32)
    x1 = xf[..., :half]
    x2 = xf[..., half:ROT_DIM]
    out = jnp.concatenate([x1 * cos - x2 * sin, x2 * cos + x1 * sin, xf[..., ROT_DIM:]], axis=-1)
    return out.astype(x.dtype)


def masked_softmax(s, mask):
    s = jnp.where(mask, s, -1e30)
    m = jnp.max(s, -1, keepdims=True)
    e = jnp.where(mask, jnp.exp(s - m), 0.0)
    return e / jnp.maximum(jnp.sum(e, -1, keepdims=True), 1e-30)


def compress(k_raw, pe, w1, w2):
    bn, t, g, hd = k_raw.shape
    nb = t // BLOCK
    blk = k_raw[:, :nb * BLOCK].reshape(bn, nb, BLOCK, g, hd) + pe.astype(k_raw.dtype)[None, None, :, None, :]
    blk = blk.transpose(0, 1, 3, 2, 4).reshape(bn, nb, g, BLOCK * hd)
    return jax.nn.silu(blk @ w1) @ w2


def ssd(x, dt, A, bm, cm, h0):
    b, l, nh, p = x.shape
    g, n = bm.shape[2], bm.shape[3]
    r = nh // g
    qc = min(SSM_CHUNK, l)
    lp = -(-l // qc) * qc
    nc = lp // qc
    x = jnp.pad(x, ((0, 0), (0, lp - l), (0, 0), (0, 0)))
    dt = jnp.pad(dt, ((0, 0), (0, lp - l), (0, 0)))
    bm = jnp.pad(bm, ((0, 0), (0, lp - l), (0, 0), (0, 0)))
    cm = jnp.pad(cm, ((0, 0), (0, lp - l), (0, 0), (0, 0)))
    X = (x * dt[..., None]).reshape(b, nc, qc, g, r, p)
    a = (dt * A).reshape(b, nc, qc, g, r)
    Bc = bm.reshape(b, nc, qc, g, n)
    Cc = cm.reshape(b, nc, qc, g, n)
    a_cum = jnp.cumsum(a, axis=2)
    tri = jnp.tril(jnp.ones((qc, qc), dtype=bool))
    seg = a_cum[:, :, :, None] - a_cum[:, :, None, :]
    Lmat = jnp.exp(jnp.where(tri[:, :, None, None], seg, -jnp.inf))
    cb = jnp.einsum('bcqgn,bcsgn->bcgqs', Cc, Bc)
    y_diag = jnp.einsum('bcgqs,bcqsgr,bcsgrp->bcqgrp', cb, Lmat, X)
    decay_s = jnp.exp(a_cum[:, :, -1:] - a_cum)
    states = jnp.einsum('bcsgn,bcsgr,bcsgrp->bcgrpn', Bc, decay_s, X)
    chunk_decay = jnp.exp(a_cum[:, :, -1])

    def step(hc, inp):
        st, dc = inp
        return hc * dc[..., None, None] + st, hc

    h_last, h_prev = lax.scan(step, h0.reshape(b, g, r, p, n),
                              (states.transpose(1, 0, 2, 3, 4, 5), chunk_decay.transpose(1, 0, 2, 3)))
    h_prev = h_prev.transpose(1, 0, 2, 3, 4, 5)
    y_off = jnp.einsum('bcqgn,bcgrpn,bcqgr->bcqgrp', Cc, h_prev, jnp.exp(a_cum))
    y = (y_diag + y_off).reshape(b, lp, nh, p)[:, :l]
    return y, h_last.reshape(b, nh, p, n)


def nsa_attention(q_plain, q_rot, kc, vc, ks, vs, kw, vw, gates, q_pos0, kw_pos0):
    f32 = jnp.float32
    bn, L = q_plain.shape[0], q_plain.shape[1]
    R = N_HEADS // KV_GROUPS
    T = ks.shape[1]
    qb = min(Q_BLOCK, L)
    nq = -(-L // qb)
    lp = nq * qb
    padq = ((0, 0), (0, lp - L), (0, 0), (0, 0))
    qp_all = jnp.pad(q_plain.astype(f32), padq).reshape(bn, lp, KV_GROUPS, R, HEAD_DIM)
    qr_all = jnp.pad(q_rot.astype(f32), padq).reshape(bn, lp, KV_GROUPS, R, HEAD_DIM)
    g_all = jnp.pad(gates, padq).reshape(bn, lp, 3, KV_GROUPS, R)
    kc = kc.astype(f32)
    vc = vc.astype(f32)
    nb_c = kc.shape[1]
    nb_s = -(-T // BLOCK)
    pads = ((0, 0), (0, nb_s * BLOCK - T), (0, 0), (0, 0))
    ks_b = jnp.pad(ks.astype(f32), pads).reshape(bn, nb_s, BLOCK, KV_GROUPS, HEAD_DIM).transpose(0, 3, 1, 2, 4)
    vs_b = jnp.pad(vs.astype(f32), pads).reshape(bn, nb_s, BLOCK, KV_GROUPS, HEAD_DIM).transpose(0, 3, 1, 2, 4)
    padw = ((0, 0), (WINDOW, lp - L), (0, 0), (0, 0))
    kw_p = jnp.pad(kw.astype(f32), padw)
    vw_p = jnp.pad(vw.astype(f32), padw)
    k_top = min(TOP_N, nb_s)
    scale = HEAD_DIM ** -0.5
    take = jax.vmap(jax.vmap(lambda arr, ix: arr[ix]))
    jc = jnp.arange(nb_c)
    js = jnp.arange(nb_s)
    offs = jnp.arange(BLOCK)
    wi = jnp.arange(WINDOW + qb)
    msel = k_top * BLOCK

    def one_block(i):
        q0 = i * qb
        t = q_pos0 + q0 + jnp.arange(qb)
        qp = lax.dynamic_slice_in_dim(qp_all, q0, qb, axis=1)
        qr = lax.dynamic_slice_in_dim(qr_all, q0, qb, axis=1)
        g = lax.dynamic_slice_in_dim(g_all, q0, qb, axis=1)[..., None]
        mask_c = (jc[None, :] + 1) * BLOCK - 1 <= t[:, None]
        p_c = masked_softmax(jnp.einsum('bqgrd,bjgd->bgrqj', qp, kc) * scale, mask_c)
        o_c = jnp.einsum('bgrqj,bjgd->bqgrd', p_c, vc)
        imp = jnp.pad(p_c.sum(axis=2), ((0, 0), (0, 0), (0, 0), (0, nb_s - nb_c)))
        cur = (t // BLOCK)[:, None]
        valid = js[None, :] <= cur
        forced = valid & ((js[None, :] == 0) | (js[None, :] == cur) | (js[None, :] == cur - 1))
        score = jnp.where(forced, BIG, jnp.where(valid, imp, -BIG))
        top_s, idx = lax.top_k(score, k_top)
        k_sel = take(ks_b, idx)
        v_sel = take(vs_b, idx)
        kpos = idx[..., None] * BLOCK + offs
        mask_s = (top_s > -0.5 * BIG)[..., None] & (kpos <= t[None, None, :, None, None])
        s_s = jnp.einsum('bqgrd,bgqkld->bgrqkl', qr, k_sel) * scale
        p_s = masked_softmax(s_s.reshape(bn, KV_GROUPS, R, qb, msel), mask_s.reshape(bn, KV_GROUPS, 1, qb, msel))
        o_s = jnp.einsum('bgrqm,bgqmd->bqgrd', p_s, v_sel.reshape(bn, KV_GROUPS, qb, msel, HEAD_DIM))
        start = q_pos0 + q0 - kw_pos0
        k_w = lax.dynamic_slice_in_dim(kw_p, start, WINDOW + qb, axis=1)
        v_w = lax.dynamic_slice_in_dim(vw_p, start, WINDOW + qb, axis=1)
        pidx = start + wi
        wpos = kw_pos0 - WINDOW + pidx
        mask_w = (pidx[None, :] >= WINDOW) & (wpos[None, :] <= t[:, None]) & (t[:, None] - wpos[None, :] < WINDOW)
        p_w = masked_softmax(jnp.einsum('bqgrd,bmgd->bgrqm', qr, k_w) * scale, mask_w)
        o_w = jnp.einsum('bgrqm,bmgd->bqgrd', p_w, v_w)
        return g[:, :, 0] * o_c + g[:, :, 1] * o_s + g[:, :, 2] * o_w

    out = lax.map(one_block, jnp.arange(nq))
    return out.transpose(1, 0, 2, 3, 4, 5).reshape(bn, lp, N_HEADS * HEAD_DIM)[:, :L]


def layer(x, c, past, w, q_pos0):
    kc_p, vc_p, ks_p, vs_p, kw_p, vw_p, conv_p, sconv_p, h0 = past
    f32 = jnp.float32
    dtype = x.dtype
    bn, L = x.shape[0], x.shape[1]
    mod = (jax.nn.silu(c) @ w['w_ada'] + w['b_ada'])[:, None, :]
    sh1, sc1, g1, sh2, sc2, g2 = jnp.split(mod, 6, axis=-1)
    h = rmsnorm(x, w['norm1_g']) * (1 + sc1) + sh1
    u = h @ w['w_in']
    split_idx = [int(v) for v in np.cumsum(SPLITS)[:-1]]
    u_cf, z, xbc, dt_raw, q, kc, vc, ks, vs, kw, vw, g_nsa, g_mix = jnp.split(u, split_idx, axis=-1)
    a_lin, b_lin = jnp.split(u_cf, 2, axis=-1)
    glu = a_lin * jax.nn.sigmoid(b_lin)
    glu_ext = jnp.concatenate([conv_p.astype(dtype), glu], axis=1)
    new_conv = glu_ext[:, -(CONV_K - 1):]
    cv = causal_dwconv(glu_ext, w['cf_dw_w'], w['cf_dw_b'])
    cv = jax.nn.silu(layernorm(cv, w['cf_ln_g'], w['cf_ln_b']))
    br_cf = cv @ w['w_cf_out']
    xbc_ext = jnp.concatenate([sconv_p.astype(dtype), xbc], axis=1)
    new_sconv = xbc_ext[:, -(SSM_CONV_K - 1):]
    xbc = jax.nn.silu(causal_dwconv(xbc_ext, w['ssm_conv_w'], w['ssm_conv_b']))
    xs, bm, cm = jnp.split(xbc, [D_INNER, D_INNER + SSM_GROUPS * D_STATE], axis=-1)
    dt = jax.nn.softplus(dt_raw.astype(f32) + w['dt_bias'].astype(f32))
    A = -jnp.exp(w['a_log'].astype(f32))
    xh = xs.astype(f32).reshape(bn, L, SSM_HEADS, SSM_HEADDIM)
    y, h_new = ssd(xh, dt, A, bm.astype(f32).reshape(bn, L, SSM_GROUPS, D_STATE),
                   cm.astype(f32).reshape(bn, L, SSM_GROUPS, D_STATE), h0.astype(f32))
    y = y + w['d_skip'].astype(f32)[:, None] * xh
    y = y.reshape(bn, L, D_INNER) * jax.nn.silu(z.astype(f32))
    yg = y.reshape(bn, L, SSM_GROUPS, D_INNER // SSM_GROUPS)
    yg = yg * lax.rsqrt(jnp.mean(yg * yg, -1, keepdims=True) + EPS)
    y = (yg.reshape(bn, L, D_INNER) * w['ssm_norm_g'].astype(f32)).astype(dtype)
    br_ssm = y @ w['w_ssm_out']
    pos = q_pos0 + jnp.arange(L, dtype=jnp.int32)
    kvs = (bn, L, KV_GROUPS, HEAD_DIM)
    qh = q.reshape(bn, L, N_HEADS, HEAD_DIM)
    q_rot = apply_rope(qh, pos)
    kc = kc.reshape(kvs)
    vc = vc.reshape(kvs)
    ks = apply_rope(ks.reshape(kvs), pos)
    vs = vs.reshape(kvs)
    kw = apply_rope(kw.reshape(kvs), pos)
    vw = vw.reshape(kvs)
    kc_all = jnp.concatenate([kc_p.astype(dtype), kc], axis=1)
    vc_all = jnp.concatenate([vc_p.astype(dtype), vc], axis=1)
    ks_all = jnp.concatenate([ks_p.astype(dtype), ks], axis=1)
    vs_all = jnp.concatenate([vs_p.astype(dtype), vs], axis=1)
    kcmp = compress(kc_all, w['cmp_pe_k'], w['cmp_w1_k'], w['cmp_w2_k'])
    vcmp = compress(vc_all, w['cmp_pe_v'], w['cmp_w1_v'], w['cmp_w2_v'])
    kw_ext = jnp.concatenate([kw_p.astype(dtype), kw], axis=1)
    vw_ext = jnp.concatenate([vw_p.astype(dtype), vw], axis=1)
    kw_pos0 = q_pos0 - kw_p.shape[1]
    keep = min(WINDOW, q_pos0 + L)
    new_kw = kw_ext[:, kw_ext.shape[1] - keep:]
    new_vw = vw_ext[:, vw_ext.shape[1] - keep:]
    gates = jax.nn.sigmoid(g_nsa.astype(f32)).reshape(bn, L, 3, N_HEADS)
    att = nsa_attention(qh, q_rot, kcmp, vcmp, ks_all, vs_all, kw_ext, vw_ext, gates, q_pos0, kw_pos0).astype(dtype)
    br_att = att @ w['w_attn_out']
    gm = jax.nn.sigmoid(g_mix.astype(f32)).reshape(bn, L, 3, D_MODEL).astype(dtype)
    mixed = gm[:, :, 0] * br_cf + gm[:, :, 1] * br_ssm + gm[:, :, 2] * br_att
    x = x + g1 * (mixed @ w['w_o'])
    h2 = rmsnorm(x, w['norm2_g']) * (1 + sc2) + sh2
    gf, uf = jnp.split(h2 @ w['w_ffn_in'], 2, axis=-1)
    x = x + g2 * ((jax.nn.silu(gf) * uf) @ w['w_ffn_out'])
    return x, (kc, vc, ks, vs, new_kw, new_vw, new_conv, new_sconv, h_new)


def setup_inputs(seed: int = 0) -> dict:
    key = jax.random.key(seed)
    keys = iter(jax.random.split(key, 64))

    def nrm(shape, scale):
        return jax.random.normal(next(keys), shape, jnp.float32) * scale

    n_pages = PAST_LEN // PAGE_SIZE
    n_used = DEC_BATCH * n_pages
    n_pool = n_used + (n_used + 3) // 4
    win = min(WINDOW, PAST_LEN)
    pool_shape = (DEPTH, n_pool, PAGE_SIZE, KV_GROUPS, HEAD_DIM)
    d = {}
    d['x_prompt'] = nrm((BATCH, SEQ, D_MODEL), 1.0)
    d['x_sample'] = nrm((DEC_BATCH, DEC_SEQ, D_MODEL), 1.0)
    d['c_prompt'] = nrm((BATCH, D_MODEL), 1.0)
    d['c_sample'] = nrm((DEC_BATCH, D_MODEL), 1.0)
    d['cache_cmp_k'] = nrm(pool_shape, 1.0)
    d['cache_cmp_v'] = nrm(pool_shape, 1.0)
    d['cache_slc_k'] = nrm(pool_shape, 1.0)
    d['cache_slc_v'] = nrm(pool_shape, 1.0)
    d['state_win_k'] = nrm((DEPTH, DEC_BATCH, win, KV_GROUPS, HEAD_DIM), 1.0)
    d['state_win_v'] = nrm((DEPTH, DEC_BATCH, win, KV_GROUPS, HEAD_DIM), 1.0)
    d['state_conv'] = nrm((DEPTH, DEC_BATCH, CONV_K - 1, D_CONV), 0.5)
    d['state_ssm_conv'] = nrm((DEPTH, DEC_BATCH, SSM_CONV_K - 1, SSM_CONV_DIM), 1.0)
    d['state_ssm'] = nrm((DEPTH, DEC_BATCH, SSM_HEADS, SSM_HEADDIM, D_STATE), 0.5)
    perm = jax.random.permutation(next(keys), n_pool)
    d['page_table'] = perm[:n_used].reshape(DEC_BATCH, n_pages).astype(jnp.int32)
    d['norm1_g'] = 1.0 + nrm((DEPTH, D_MODEL), 0.05)
    d['w_ada'] = nrm((DEPTH, D_MODEL, 6 * D_MODEL), 0.5 * D_MODEL ** -0.5)
    d['b_ada'] = nrm((DEPTH, 6 * D_MODEL), 0.02)
    d['w_in'] = nrm((DEPTH, D_MODEL, N_IN), D_MODEL ** -0.5)
    d['cf_dw_w'] = nrm((DEPTH, CONV_K, D_CONV), CONV_K ** -0.5)
    d['cf_dw_b'] = nrm((DEPTH, D_CONV), 0.02)
    d['cf_ln_g'] = 1.0 + nrm((DEPTH, D_CONV), 0.05)
    d['cf_ln_b'] = nrm((DEPTH, D_CONV), 0.02)
    d['w_cf_out'] = nrm((DEPTH, D_CONV, D_MODEL), D_CONV ** -0.5)
    d['ssm_conv_w'] = nrm((DEPTH, SSM_CONV_K, SSM_CONV_DIM), SSM_CONV_K ** -0.5)
    d['ssm_conv_b'] = nrm((DEPTH, SSM_CONV_DIM), 0.02)
    dt0 = jnp.exp(jax.random.uniform(next(keys), (DEPTH, SSM_HEADS), jnp.float32, math.log(1e-3), math.log(1e-1)))
    d['dt_bias'] = dt0 + jnp.log(-jnp.expm1(-dt0))
    d['a_log'] = jnp.log(jax.random.uniform(next(keys), (DEPTH, SSM_HEADS), jnp.float32, 1.0, 16.0))
    d['d_skip'] = 1.0 + nrm((DEPTH, SSM_HEADS), 0.1)
    d['ssm_norm_g'] = 1.0 + nrm((DEPTH, D_INNER), 0.05)
    d['w_ssm_out'] = nrm((DEPTH, D_INNER, D_MODEL), D_INNER ** -0.5)
    d['cmp_pe_k'] = nrm((DEPTH, BLOCK, HEAD_DIM), 0.1)
    d['cmp_pe_v'] = nrm((DEPTH, BLOCK, HEAD_DIM), 0.1)
    d['cmp_w1_k'] = nrm((DEPTH, BLOCK * HEAD_DIM, CMP_HIDDEN), (BLOCK * HEAD_DIM) ** -0.5)
    d['cmp_w2_k'] = nrm((DEPTH, CMP_HIDDEN, HEAD_DIM), CMP_HIDDEN ** -0.5)
    d['cmp_w1_v'] = nrm((DEPTH, BLOCK * HEAD_DIM, CMP_HIDDEN), (BLOCK * HEAD_DIM) ** -0.5)
    d['cmp_w2_v'] = nrm((DEPTH, CMP_HIDDEN, HEAD_DIM), CMP_HIDDEN ** -0.5)
    d['w_attn_out'] = nrm((DEPTH, N_HEADS * HEAD_DIM, D_MODEL), (N_HEADS * HEAD_DIM) ** -0.5)
    d['w_o'] = nrm((DEPTH, D_MODEL, D_MODEL), D_MODEL ** -0.5)
    d['norm2_g'] = 1.0 + nrm((DEPTH, D_MODEL), 0.05)
    d['w_ffn_in'] = nrm((DEPTH, D_MODEL, 2 * D_FF), D_MODEL ** -0.5)
    d['w_ffn_out'] = nrm((DEPTH, D_FF, D_MODEL), D_FF ** -0.5)
    d['final_g'] = 1.0 + nrm((D_MODEL,), 0.05)
    return d


def reference(x_prompt, x_sample, c_prompt, c_sample,
              cache_cmp_k, cache_cmp_v, cache_slc_k, cache_slc_v,
              state_win_k, state_win_v, state_conv, state_ssm_conv, state_ssm,
              page_table,
              norm1_g, w_ada, b_ada, w_in,
              cf_dw_w, cf_dw_b, cf_ln_g, cf_ln_b, w_cf_out,
              ssm_conv_w, ssm_conv_b, dt_bias, a_log, d_skip, ssm_norm_g, w_ssm_out,
              cmp_pe_k, cmp_pe_v, cmp_w1_k, cmp_w2_k, cmp_w1_v, cmp_w2_v, w_attn_out,
              w_o, norm2_g, w_ffn_in, w_ffn_out, final_g):
    bp = x_prompt.shape[0]
    bs = x_sample.shape[0]
    pdt = x_prompt.dtype
    sample_pos0 = page_table.shape[1] * PAGE_SIZE
    empty = jnp.zeros((bp, 0, KV_GROUPS, HEAD_DIM), pdt)
    past_p = (empty, empty, empty, empty, empty, empty,
              jnp.zeros((bp, CONV_K - 1, D_CONV), pdt),
              jnp.zeros((bp, SSM_CONV_K - 1, SSM_CONV_DIM), pdt),
              jnp.zeros((bp, SSM_HEADS, SSM_HEADDIM, D_STATE), jnp.float32))

    def gather_pages(pool):
        return pool[page_table].reshape(bs, -1, KV_GROUPS, HEAD_DIM)

    xp = x_prompt
    xs = x_sample
    outs_p = [[] for _ in range(9)]
    outs_s = [[] for _ in range(9)]
    for l in range(DEPTH):
        w = dict(norm1_g=norm1_g[l], w_ada=w_ada[l], b_ada=b_ada[l], w_in=w_in[l],
                 cf_dw_w=cf_dw_w[l], cf_dw_b=cf_dw_b[l], cf_ln_g=cf_ln_g[l], cf_ln_b=cf_ln_b[l], w_cf_out=w_cf_out[l],
                 ssm_conv_w=ssm_conv_w[l], ssm_conv_b=ssm_conv_b[l], dt_bias=dt_bias[l], a_log=a_log[l],
                 d_skip=d_skip[l], ssm_norm_g=ssm_norm_g[l], w_ssm_out=w_ssm_out[l],
                 cmp_pe_k=cmp_pe_k[l], cmp_pe_v=cmp_pe_v[l], cmp_w1_k=cmp_w1_k[l], cmp_w2_k=cmp_w2_k[l],
                 cmp_w1_v=cmp_w1_v[l], cmp_w2_v=cmp_w2_v[l], w_attn_out=w_attn_out[l],
                 w_o=w_o[l], norm2_g=norm2_g[l], w_ffn_in=w_ffn_in[l], w_ffn_out=w_ffn_out[l])
        xp, st_p = layer(xp, c_prompt, past_p, w, 0)
        past_s = (gather_pages(cache_cmp_k[l]), gather_pages(cache_cmp_v[l]),
                  gather_pages(cache_slc_k[l]), gather_pages(cache_slc_v[l]),
                  state_win_k[l], state_win_v[l], state_conv[l], state_ssm_conv[l], state_ssm[l])
        xs, st_s = layer(xs, c_sample, past_s, w, sample_pos0)
        for i in range(9):
            outs_p[i].append(st_p[i])
            outs_s[i].append(st_s[i])
    y_prompt = rmsnorm(xp, final_g)
    y_sample = rmsnorm(xs, final_g)
    np_ = [jnp.stack(o) for o in outs_p]
    ns_ = [jnp.stack(o) for o in outs_s]
    return (y_prompt, y_sample,
            np_[0], np_[1], np_[2], np_[3], np_[4], np_[5], np_[6], np_[7], np_[8],
            ns_[0], ns_[1], ns_[2], ns_[3], ns_[4], ns_[5], ns_[6], ns_[7], ns_[8])
```

```python
import functools

import numpy as np
import jax
import jax.numpy as jnp
from jax import lax
from jax.experimental import pallas as pl
from jax.experimental.pallas import tpu as pltpu

F32 = jnp.float32
BF16 = jnp.bfloat16

D_MODEL = 1024
PAGE_SIZE = 128
CONV_K = 31
D_INNER = 2 * D_MODEL
SSM_HEADDIM = 64
SSM_HEADS = D_INNER // SSM_HEADDIM
SSM_GROUPS = 4
D_STATE = 128
SSM_CONV_K = 4
SSM_CONV_DIM = D_INNER + 2 * SSM_GROUPS * D_STATE
SSM_CHUNK = 128
N_HEADS = 16
HEAD_DIM = 64
KV_GROUPS = 2
HEADS_PER_GROUP = N_HEADS // KV_GROUPS
BLOCK = 64
TOP_N = 16
WINDOW = 512
CMP_HIDDEN = 256
ROPE_THETA = 500000.0
ROT_DIM = HEAD_DIM // 4
BIG = 1e9
NEG = -1e30
D_FF = ((8 * D_MODEL + 3 * 256 - 1) // (3 * 256)) * 256
EPS = 1e-6
KV_W = KV_GROUPS * HEAD_DIM

XBC_OFF = 0
GMIX_OFF = 3072
UCF_OFF = 6144
Z_OFF = 8192
Q_OFF = 10240
KV_OFF = 11264
SMALL_OFF = KV_OFF + 6 * KV_W
N_PACK = 12288

TQ = 256
TK = 512

VMEM_LIMIT = 56 * 1024 * 1024


def _cparams(*sem):
    return pltpu.CompilerParams(dimension_semantics=sem, vmem_limit_bytes=VMEM_LIMIT)


def _dot(a, b):
    return jnp.dot(a, b, preferred_element_type=F32)


def _dot_nt(a, b):
    return lax.dot_general(a, b, (((1,), (1,)), ((), ())), preferred_element_type=F32)


def _dot_tn(a, b):
    return lax.dot_general(a, b, (((0,), (0,)), ((), ())), preferred_element_type=F32)


def _sigmoid(x):
    return jax.nn.sigmoid(x)


def _silu(x):
    return x * jax.nn.sigmoid(x)


def _softplus(x):
    return jnp.maximum(x, 0.0) + jnp.log(1.0 + jnp.exp(-jnp.abs(x)))


def _split3(x):
    hi = x.astype(BF16)
    r1 = x - hi.astype(F32)
    mid = r1.astype(BF16)
    lo = (r1 - mid.astype(F32)).astype(BF16)
    return hi, mid, lo


def _dot_exact_rhs01(x, m01):
    hi, mid, lo = _split3(x)
    return _dot(hi, m01) + _dot(mid, m01) + _dot(lo, m01)


def _dot_exact_lhs01(m01, x):
    hi, mid, lo = _split3(x)
    return _dot(m01, hi) + _dot(m01, mid) + _dot(m01, lo)


def _mod_kernel(c_ref, w_ref, b_ref, o_ref):
    c = c_ref[...]
    o_ref[0] = _dot(_silu(c).astype(BF16), w_ref[0].astype(BF16)) + b_ref[0]


def _modulation(c_all, w_ada, b_ada):
    depth = w_ada.shape[0]
    bc = c_all.shape[0]
    n = w_ada.shape[2]
    tn = 1536
    return pl.pallas_call(
        _mod_kernel,
        grid=(depth, n // tn),
        in_specs=[pl.BlockSpec((bc, D_MODEL), lambda l, j: (0, 0)),
                  pl.BlockSpec((1, D_MODEL, tn), lambda l, j: (l, 0, j)),
                  pl.BlockSpec((1, 1, tn), lambda l, j: (l, 0, j))],
        out_specs=pl.BlockSpec((1, bc, tn), lambda l, j: (l, 0, j)),
        out_shape=jax.ShapeDtypeStruct((depth, bc, n), F32),
        compiler_params=_cparams("arbitrary", "arbitrary"),
        name="modulation",
    )(c_all, w_ada, b_ada.reshape(depth, 1, n))


def _in_proj_kernel(x_ref, g_ref, sc_ref, sh_ref, w_ref, o_ref, h_ref):
    @pl.when(pl.program_id(1) == 0)
    def _():
        x = x_ref[...]
        y = x * lax.rsqrt(jnp.mean(x * x, axis=-1, keepdims=True) + EPS) * g_ref[...]
        h_ref[...] = (y * (1.0 + sc_ref[...]) + sh_ref[...]).astype(BF16)

    o_ref[...] = _dot(h_ref[...], w_ref[...])


def _in_proj(x, g, sc, sh, w, tm):
    r = x.shape[0]
    n = w.shape[1]
    tn = 1024
    mrows = tm if sc.shape[0] == r else 1
    mmap = (lambda i, j: (i, 0)) if sc.shape[0] == r else (lambda i, j: (0, 0))
    return pl.pallas_call(
        _in_proj_kernel,
        grid=(r // tm, n // tn),
        in_specs=[pl.BlockSpec((tm, D_MODEL), lambda i, j: (i, 0)),
                  pl.BlockSpec((1, D_MODEL), lambda i, j: (0, 0)),
                  pl.BlockSpec((mrows, D_MODEL), mmap),
                  pl.BlockSpec((mrows, D_MODEL), mmap),
                  pl.BlockSpec((D_MODEL, tn), lambda i, j: (0, j))],
        out_specs=pl.BlockSpec((tm, tn), lambda i, j: (i, j)),
        out_shape=jax.ShapeDtypeStruct((r, n), F32),
        scratch_shapes=[pltpu.VMEM((tm, D_MODEL), BF16)],
        compiler_params=_cparams("arbitrary", "arbitrary"),
        name="in_proj",
    )(x, g, sc, sh, w)


def _matmul_kernel(x_ref, w_ref, o_ref):
    o_ref[...] = _dot(x_ref[...].astype(BF16), w_ref[...])


def _matmul(x, w):
    r, k = x.shape
    n = w.shape[1]
    return pl.pallas_call(
        _matmul_kernel,
        grid=(1,),
        in_specs=[pl.BlockSpec((r, k), lambda i: (0, 0)), pl.BlockSpec((k, n), lambda i: (0, 0))],
        out_specs=pl.BlockSpec((r, n), lambda i: (0, 0)),
        out_shape=jax.ShapeDtypeStruct((r, n), F32),
        compiler_params=_cparams("arbitrary"),
        name="row_matmul",
    )(x, w)


CF_TL = 256
CF_RC = 32
CF_HALO = 32


def _layernorm_silu(x, g, b):
    xc = x - jnp.mean(x, axis=-1, keepdims=True)
    y = xc * lax.rsqrt(jnp.mean(xc * xc, axis=-1, keepdims=True) + EPS)
    return _silu(y * g + b)


def _conformer_kernel(u_ref, w_ref, b_ref, lg_ref, lb_ref, wo_ref, o_ref, nc_ref, bufs, cvb):
    i = pl.program_id(0)
    tl = CF_TL

    @pl.when(i == 0)
    def _():
        bufs[0, 0:CF_HALO, :] = jnp.zeros((CF_HALO, D_MODEL), F32)

    u = u_ref[...]
    bufs[0, CF_HALO:CF_HALO + tl, :] = u[:, :D_MODEL] * _sigmoid(u[:, D_MODEL:])
    for s in range(1, 8):
        bufs[s, 0:tl + 24, :] = bufs[0, s:s + tl + 24, :]

    def chunk(c, carry):
        off = pl.multiple_of(c * CF_RC, CF_RC)
        acc = jnp.broadcast_to(b_ref[...], (CF_RC, D_MODEL))
        for k in range(CONV_K):
            a, s = divmod(CF_HALO - (CONV_K - 1) + k, 8)
            acc = acc + w_ref[k:k + 1, :] * bufs[s, pl.ds(off + 8 * a, CF_RC), :]
        cvb[pl.ds(off, CF_RC), :] = _layernorm_silu(acc, lg_ref[...], lb_ref[...]).astype(BF16)
        return carry

    lax.fori_loop(0, tl // CF_RC, chunk, 0)
    tail = bufs[0, tl:tl + CF_HALO, :]
    nc_ref[...] = tail
    bufs[0, 0:CF_HALO, :] = tail
    o_ref[...] = _dot(cvb[...], wo_ref[...])


def _conformer_prompt(u, dw_w, dw_b, ln_g, ln_b, w_out):
    l = u.shape[0]
    tl = CF_TL
    wpad = jnp.concatenate([dw_w, jnp.zeros((1, D_MODEL), F32)], axis=0)
    return pl.pallas_call(
        _conformer_kernel,
        grid=(l // tl,),
        in_specs=[pl.BlockSpec((tl, 2 * D_MODEL), lambda i: (i, UCF_OFF // (2 * D_MODEL))),
                  pl.BlockSpec((CONV_K + 1, D_MODEL), lambda i: (0, 0)),
                  pl.BlockSpec((1, D_MODEL), lambda i: (0, 0)),
                  pl.BlockSpec((1, D_MODEL), lambda i: (0, 0)),
                  pl.BlockSpec((1, D_MODEL), lambda i: (0, 0)),
                  pl.BlockSpec((D_MODEL, D_MODEL), lambda i: (0, 0))],
        out_specs=[pl.BlockSpec((tl, D_MODEL), lambda i: (i, 0)),
                   pl.BlockSpec((CF_HALO, D_MODEL), lambda i: (0, 0))],
        out_shape=[jax.ShapeDtypeStruct((l, D_MODEL), F32),
                   jax.ShapeDtypeStruct((CF_HALO, D_MODEL), F32)],
        scratch_shapes=[pltpu.VMEM((8, tl + CF_HALO, D_MODEL), F32),
                        pltpu.VMEM((tl, D_MODEL), BF16)],
        compiler_params=_cparams("arbitrary"),
        name="conformer_prompt",
    )(u, wpad, dw_b, ln_g, ln_b, w_out)


def _gated_group_norm(y, z, g):
    y = y * _silu(z)
    gw = D_INNER // SSM_GROUPS
    parts = []
    for k in range(SSM_GROUPS):
        yg = y[:, k * gw:(k + 1) * gw]
        parts.append(yg * lax.rsqrt(jnp.mean(yg * yg, axis=-1, keepdims=True) + EPS))
    return jnp.concatenate(parts, axis=1) * g


def _ssd_kernel(xbc_ref, z_ref, sm_ref, cw_ref, cb_ref, dtb_ref, alog_ref, dsk_ref, ng_ref, rexp_ref,
                wo_ref, o_ref, hout_ref, sc_ref, cbuf, hst):
    i = pl.program_id(0)
    q = SSM_CHUNK

    @pl.when(i == 0)
    def _():
        cbuf[0:8, :] = jnp.zeros((8, SSM_CONV_DIM), F32)
        hst[...] = jnp.zeros(hst.shape, F32)

    x = xbc_ref[...]
    cbuf[8:8 + q, :] = x
    conv = cb_ref[...] + cw_ref[3:4, :] * x
    for k in range(SSM_CONV_K - 1):
        conv = conv + cw_ref[k:k + 1, :] * cbuf[5 + k:5 + k + q, :]
    tail = cbuf[q:q + 8, :]
    sc_ref[...] = tail
    cbuf[0:8, :] = tail
    xa = _silu(conv)
    xs = xa[:, :D_INNER]
    bm = xa[:, D_INNER:D_INNER + SSM_GROUPS * D_STATE]
    cm = xa[:, D_INNER + SSM_GROUPS * D_STATE:]

    dt = _softplus(sm_ref[...] + dtb_ref[...])
    a = dt * (-jnp.exp(alog_ref[...]))
    row = lax.broadcasted_iota(jnp.int32, (q, q), 0)
    col = lax.broadcasted_iota(jnp.int32, (q, q), 1)
    tri = row >= col
    a_cum = _dot_exact_lhs01(tri.astype(BF16), a)
    a_cum_t = a_cum.T
    a_last = a_cum[q - 1:q, :]
    rexp = rexp_ref[...]
    dtx = _dot_exact_rhs01(dt, rexp)
    eax = _dot_exact_rhs01(jnp.exp(a_cum), rexp)
    decx = _dot_exact_rhs01(jnp.exp(a_last - a_cum), rexp)
    xin = xs * dtx
    xdec = (xin * decx).astype(BF16)
    xin_b = xin.astype(BF16)
    cd_col = jnp.broadcast_to(jnp.exp(a_last), (q, q)).T
    lane_lo = lax.broadcasted_iota(jnp.int32, (q, 2 * SSM_HEADDIM), 1) < SSM_HEADDIM

    hpg = SSM_HEADS // SSM_GROUPS
    gw = hpg * SSM_HEADDIM
    y_groups = []
    for g in range(SSM_GROUPS):
        cg = cm[:, g * D_STATE:(g + 1) * D_STATE].astype(BF16)
        bg = bm[:, g * D_STATE:(g + 1) * D_STATE].astype(BF16)
        cb = _dot_nt(cg, bg)
        hprev = hst[g * gw:(g + 1) * gw, :]
        y_off = _dot_nt(cg, hprev.astype(BF16)) * eax[:, g * gw:(g + 1) * gw]
        st = _dot_tn(xdec[:, g * gw:(g + 1) * gw], bg)
        pair_parts = []
        for pr in range(hpg // 2):
            xp = xin_b[:, g * gw + pr * 128:g * gw + (pr + 1) * 128]
            yp = None
            for sub in range(2):
                h = g * hpg + pr * 2 + sub
                seg = a_cum[:, h:h + 1] - a_cum_t[h:h + 1, :]
                lm = jnp.exp(jnp.where(tri, seg, -jnp.inf))
                mh = (cb * lm).astype(BF16)
                xm = jnp.where(lane_lo if sub == 0 else jnp.logical_not(lane_lo), xp, jnp.zeros_like(xp))
                d = _dot(mh, xm)
                yp = d if yp is None else yp + d
                r0 = h * SSM_HEADDIM
                lo = (pr * 2 + sub) * SSM_HEADDIM
                hst[r0:r0 + SSM_HEADDIM, :] = (hprev[lo:lo + SSM_HEADDIM, :] * cd_col[h:h + 1, :]
                                              + st[lo:lo + SSM_HEADDIM, :])
            pair_parts.append(yp)
        y_groups.append(jnp.concatenate(pair_parts, axis=1) + y_off)
    y = jnp.concatenate(y_groups, axis=1) + dsk_ref[...] * xs
    yn = _gated_group_norm(y, z_ref[...], ng_ref[...])
    o_ref[...] = _dot(yn.astype(BF16), wo_ref[...])

    @pl.when(i == pl.num_programs(0) - 1)
    def _():
        hout_ref[...] = hst[...]


def _pad_lanes(v, n=128):
    return jnp.concatenate([v, jnp.zeros((v.shape[0], n - v.shape[1]), v.dtype)], axis=1)


def _ssd_prompt(u, conv_w, conv_b, dt_bias, a_log, dskip_x, norm_g, rexp, w_out):
    l = u.shape[0]
    q = SSM_CHUNK
    cw = jnp.concatenate([conv_w, jnp.zeros((4, SSM_CONV_DIM), F32)], axis=0)
    const = lambda i: (0, 0)
    return pl.pallas_call(
        _ssd_kernel,
        grid=(l // q,),
        in_specs=[pl.BlockSpec((q, SSM_CONV_DIM), lambda i: (i, XBC_OFF // SSM_CONV_DIM)),
                  pl.BlockSpec((q, D_INNER), lambda i: (i, Z_OFF // D_INNER)),
                  pl.BlockSpec((q, 128), lambda i: (i, SMALL_OFF // 128)),
                  pl.BlockSpec((8, SSM_CONV_DIM), const),
                  pl.BlockSpec((1, SSM_CONV_DIM), const),
                  pl.BlockSpec((1, 128), const),
                  pl.BlockSpec((1, 128), const),
                  pl.BlockSpec((1, D_INNER), const),
                  pl.BlockSpec((1, D_INNER), const),
                  pl.BlockSpec((128, D_INNER), const),
                  pl.BlockSpec((D_INNER, D_MODEL), const)],
        out_specs=[pl.BlockSpec((q, D_MODEL), lambda i: (i, 0)),
                   pl.BlockSpec((D_INNER, D_STATE), const),
                   pl.BlockSpec((8, SSM_CONV_DIM), const)],
        out_shape=[jax.ShapeDtypeStruct((l, D_MODEL), F32),
                   jax.ShapeDtypeStruct((D_INNER, D_STATE), F32),
                   jax.ShapeDtypeStruct((8, SSM_CONV_DIM), F32)],
        scratch_shapes=[pltpu.VMEM((q + 8, SSM_CONV_DIM), F32),
                        pltpu.VMEM((D_INNER, D_STATE), F32)],
        compiler_params=_cparams("arbitrary"),
        name="ssd_prompt",
    )(u, u, u, cw, conv_b, _pad_lanes(dt_bias), _pad_lanes(a_log), dskip_x, norm_g, rexp, w_out)


def _rope(x, c, s1, s2):
    n = x.shape[1]
    return x * c + pltpu.roll(x, ROT_DIM // 2, 1) * s1 + pltpu.roll(x, n - ROT_DIM // 2, 1) * s2


def _prep_kernel(q_ref, kv_ref, tab_ref, kc_ref, vc_ref, ks_ref, vs_ref, kw_ref, vw_ref,
                 qpt_ref, qrt_ref, ksb_ref, vst_ref, kwb_ref, vwt_ref, gt_ref):
    c = tab_ref[0]
    s1 = tab_ref[1]
    s2 = tab_ref[2]
    reps = D_MODEL // 128
    q = q_ref[...] * (HEAD_DIM ** -0.5)
    qr = _rope(q, jnp.concatenate([c] * reps, axis=1), jnp.concatenate([s1] * reps, axis=1),
               jnp.concatenate([s2] * reps, axis=1))
    qpt_ref[...] = q.T.astype(BF16)
    qrt_ref[...] = qr.T.astype(BF16)
    kv = kv_ref[...]
    kc_ref[...] = kv[:, 0:KV_W]
    vc_ref[...] = kv[:, KV_W:2 * KV_W]
    ks = _rope(kv[:, 2 * KV_W:3 * KV_W], c, s1, s2)
    ks_ref[...] = ks
    ksb_ref[...] = ks.astype(BF16)
    vs = kv[:, 3 * KV_W:4 * KV_W]
    vs_ref[...] = vs
    vst_ref[0] = vs.T.astype(BF16)
    kw = _rope(kv[:, 4 * KV_W:5 * KV_W], c, s1, s2)
    kw_ref[...] = kw
    kwb_ref[...] = kw.astype(BF16)
    vw = kv[:, 5 * KV_W:6 * KV_W]
    vw_ref[...] = vw
    vwt_ref[0] = vw.T.astype(BF16)
    gt_ref[...] = _sigmoid(kv[:, 6 * KV_W:7 * KV_W]).T


def _prep_prompt(u, tab):
    l = u.shape[0]
    tl = TQ
    nt = l // tl
    row = lambda i: (i, 0)
    f32s = jax.ShapeDtypeStruct((l, KV_W), F32)
    return pl.pallas_call(
        _prep_kernel,
        grid=(nt,),
        in_specs=[pl.BlockSpec((tl, D_MODEL), lambda i: (i, Q_OFF // D_MODEL)),
                  pl.BlockSpec((tl, D_MODEL), lambda i: (i, KV_OFF // D_MODEL)),
                  pl.BlockSpec((3, tl, 128), lambda i: (0, i, 0))],
        out_specs=[pl.BlockSpec((tl, KV_W), row)] * 6 + [
            pl.BlockSpec((D_MODEL, tl), lambda i: (0, i)),
            pl.BlockSpec((D_MODEL, tl), lambda i: (0, i)),
            pl.BlockSpec((tl, KV_W), row),
            pl.BlockSpec((1, KV_W, tl), lambda i: (i, 0, 0)),
            pl.BlockSpec((tl, KV_W), row),
            pl.BlockSpec((1, KV_W, tl), lambda i: (i, 0, 0)),
            pl.BlockSpec((128, tl), lambda i: (0, i))],
        out_shape=[f32s] * 6 + [
            jax.ShapeDtypeStruct((D_MODEL, l), BF16),
            jax.ShapeDtypeStruct((D_MODEL, l), BF16),
            jax.ShapeDtypeStruct((l, KV_W), BF16),
            jax.ShapeDtypeStruct((nt, KV_W, tl), BF16),
            jax.ShapeDtypeStruct((l, KV_W), BF16),
            jax.ShapeDtypeStruct((nt, KV_W, tl), BF16),
            jax.ShapeDtypeStruct((128, l), F32)],
        compiler_params=_cparams("arbitrary"),
        name="prep_prompt",
    )(u, u, tab)


def _prep_sample_kernel(q_ref, kv_ref, tab_ref, qp_ref, qr_ref, kvo_ref):
    c = tab_ref[0]
    s1 = tab_ref[1]
    s2 = tab_ref[2]
    reps = D_MODEL // 128
    q = q_ref[...] * (HEAD_DIM ** -0.5)
    qp_ref[...] = q
    qr_ref[...] = _rope(q, jnp.concatenate([c] * reps, axis=1), jnp.concatenate([s1] * reps, axis=1),
                        jnp.concatenate([s2] * reps, axis=1))
    kv = kv_ref[...]
    one = jnp.ones_like(c)
    zero = jnp.zeros_like(c)
    ckv = jnp.concatenate([one, one, c, one, c, one, one, one], axis=1)
    s1kv = jnp.concatenate([zero, zero, s1, zero, s1, zero, zero, zero], axis=1)
    s2kv = jnp.concatenate([zero, zero, s2, zero, s2, zero, zero, zero], axis=1)
    kvr = _rope(kv, ckv, s1kv, s2kv)
    lane = lax.broadcasted_iota(jnp.int32, kv.shape, 1)
    kvo_ref[...] = jnp.where((lane >= 6 * KV_W) & (lane < 7 * KV_W), _sigmoid(kv), kvr)


def _prep_sample(u, tab):
    b = u.shape[0]
    full = lambda i: (0, 0)
    return pl.pallas_call(
        _prep_sample_kernel,
        grid=(1,),
        in_specs=[pl.BlockSpec((b, D_MODEL), lambda i: (0, Q_OFF // D_MODEL)),
                  pl.BlockSpec((b, D_MODEL), lambda i: (0, KV_OFF // D_MODEL)),
                  pl.BlockSpec((3, b, 128), lambda i: (0, 0, 0))],
        out_specs=[pl.BlockSpec((b, D_MODEL), full)] * 3,
        out_shape=[jax.ShapeDtypeStruct((b, D_MODEL), F32)] * 3,
        compiler_params=_cparams("arbitrary"),
        name="prep_sample",
    )(u, u, tab)


def _compress_kernel(x_ref, pe_ref, w1_ref, w2_ref, o_ref):
    x = (x_ref[...] + pe_ref[...]).astype(BF16)
    hid = _silu(_dot(x, w1_ref[...]))
    o_ref[...] = _dot(hid.astype(BF16), w2_ref[...])


def _compress(x2d, pe2, w1e, w2e):
    m = x2d.shape[0]
    tm = 256 if m % 256 == 0 else m
    kdim = BLOCK * KV_W
    return pl.pallas_call(
        _compress_kernel,
        grid=(m // tm,),
        in_specs=[pl.BlockSpec((tm, kdim), lambda i: (i, 0)),
                  pl.BlockSpec((1, kdim), lambda i: (0, 0)),
                  pl.BlockSpec((kdim, KV_GROUPS * CMP_HIDDEN), lambda i: (0, 0)),
                  pl.BlockSpec((KV_GROUPS * CMP_HIDDEN, KV_W), lambda i: (0, 0))],
        out_specs=pl.BlockSpec((tm, KV_W), lambda i: (i, 0)),
        out_shape=jax.ShapeDtypeStruct((m, KV_W), F32),
        compiler_params=_cparams("arbitrary"),
        name="compress",
    )(x2d, pe2, w1e, w2e)


def _select_topn(score, index, axis, n_index):
    sel = jnp.zeros(score.shape, F32)
    work = score
    for _ in range(TOP_N):
        m = jnp.max(work, axis=axis, keepdims=True)
        first = jnp.min(jnp.where(work == m, index, n_index), axis=axis, keepdims=True)
        pick = index == first
        sel = jnp.where(pick, 1.0, sel)
        work = jnp.where(pick, -jnp.inf, work)
    return sel


def _attn_kernel(qpt_ref, qrt_ref, gt_ref, kcmp_ref, vcmp_ref, ks_ref, vst_ref,
                 kwa_ref, kwb_ref, kwc_ref, vwa_ref, vwb_ref, vwc_ref, o_ref,
                 qpad, bias, m_sc, l_sc, acc, outacc):
    i = pl.program_id(0)
    nb = kcmp_ref.shape[0]
    t0 = i * TQ
    t_row = t0 + lax.broadcasted_iota(jnp.int32, (1, TQ), 1)
    zeros_half = jnp.zeros((HEAD_DIM, TQ), BF16)

    def padded(src_ref, h):
        qh = src_ref[h * HEAD_DIM:(h + 1) * HEAD_DIM, :]
        return jnp.concatenate([qh, zeros_half] if h < HEADS_PER_GROUP else [zeros_half, qh], axis=0)

    def group_rows(x, h):
        g = h // HEADS_PER_GROUP
        return x[g * HEAD_DIM:(g + 1) * HEAD_DIM, :]

    kcmp = kcmp_ref[...].astype(BF16)
    vcmp_t = vcmp_ref[...].T.astype(BF16)
    jrow = lax.broadcasted_iota(jnp.int32, (nb, TQ), 0)
    mask_c = (jrow + 1) * BLOCK - 1 <= t_row
    cur = lax.shift_right_logical(t_row, 6)
    valid = jrow <= cur
    forced = valid & ((jrow == 0) | (jrow == cur) | (jrow == cur - 1))
    for g in range(KV_GROUPS):
        imp = jnp.zeros((nb, TQ), F32)
        for r in range(HEADS_PER_GROUP):
            h = g * HEADS_PER_GROUP + r
            s = jnp.where(mask_c, _dot(kcmp, padded(qpt_ref, h)), NEG)
            m = jnp.max(s, axis=0, keepdims=True)
            e = jnp.where(mask_c, jnp.exp(s - m), 0.0)
            p = e * (1.0 / jnp.maximum(jnp.sum(e, axis=0, keepdims=True), 1e-30))
            imp = imp + p
            o_c = group_rows(_dot(vcmp_t, p.astype(BF16)), h)
            outacc[h * HEAD_DIM:(h + 1) * HEAD_DIM, :] = gt_ref[32 + h:33 + h, :] * o_c
            qpad[h] = padded(qrt_ref, h)
        score = jnp.where(forced, BIG, jnp.where(valid, imp, -BIG))
        sel = _select_topn(score, jrow, 0, nb)
        bias[g] = jnp.where((sel > 0.5) & (score > -0.5 * BIG), 0.0, NEG)

    m_sc[...] = jnp.full(m_sc.shape, NEG, F32)
    l_sc[...] = jnp.zeros(l_sc.shape, F32)
    acc[...] = jnp.zeros(acc.shape, F32)
    n_kt = (t0 + TQ + TK - 1) // TK
    bpt = TK // BLOCK

    def kt_body(kt, carry):
        k0 = pl.multiple_of(kt * TK, TK)
        k_tile = ks_ref[pl.ds(k0, TK), :]
        kpos = k0 + lax.broadcasted_iota(jnp.int32, (TK, TQ), 0)
        causal = jnp.where(kpos > t_row, NEG, 0.0)
        for g in range(KV_GROUPS):
            bt = bias[g, pl.ds(pl.multiple_of(kt * bpt, bpt), bpt), :]
            btile = causal + jnp.concatenate(
                [jnp.broadcast_to(bt[j:j + 1, :], (BLOCK, TQ)) for j in range(bpt)], axis=0)
            for r in range(HEADS_PER_GROUP):
                h = g * HEADS_PER_GROUP + r
                s = _dot(k_tile, qpad[h]) + btile
                m_old = m_sc[h:h + 1, :]
                m_new = jnp.maximum(m_old, jnp.max(s, axis=0, keepdims=True))
                alpha = jnp.exp(m_old - m_new)
                p = jnp.exp(s - m_new)
                l_sc[h:h + 1, :] = alpha * l_sc[h:h + 1, :] + jnp.sum(p, axis=0, keepdims=True)
                m_sc[h:h + 1, :] = m_new
                pv = None
                for c in range(TK // TQ):
                    d = _dot(vst_ref[kt * (TK // TQ) + c], p[c * TQ:(c + 1) * TQ, :].astype(BF16))
                    pv = d if pv is None else pv + d
                rows = slice(h * HEAD_DIM, (h + 1) * HEAD_DIM)
                acc[rows, :] = alpha * acc[rows, :] + group_rows(pv, h)
        return carry

    lax.fori_loop(0, n_kt, kt_body, 0)

    kw = jnp.concatenate([kwa_ref[...], kwb_ref[...], kwc_ref[...]], axis=0)
    wpos = (i - 2) * TQ + lax.broadcasted_iota(jnp.int32, (3 * TQ, TQ), 0)
    bias_w = jnp.where((wpos >= 0) & (wpos <= t_row) & (t_row - wpos < WINDOW), 0.0, NEG)
    for h in range(N_HEADS):
        s = _dot(kw, qpad[h]) + bias_w
        m = jnp.max(s, axis=0, keepdims=True)
        p = jnp.exp(s - m)
        lsum = jnp.sum(p, axis=0, keepdims=True)
        pv = (_dot(vwa_ref[0], p[0:TQ, :].astype(BF16)) + _dot(vwb_ref[0], p[TQ:2 * TQ, :].astype(BF16))
              + _dot(vwc_ref[0], p[2 * TQ:, :].astype(BF16)))
        rows = slice(h * HEAD_DIM, (h + 1) * HEAD_DIM)
        o_w = group_rows(pv, h) * (1.0 / lsum)
        o_s = acc[rows, :] * (1.0 / l_sc[h:h + 1, :])
        outacc[rows, :] = (outacc[rows, :] + gt_ref[48 + h:49 + h, :] * o_s
                           + gt_ref[64 + h:65 + h, :] * o_w)
    o_ref[...] = outacc[...].T


def _attention_prompt(qpt, qrt, gt, kcmp, vcmp, ksb, vst, kwb, vwt):
    l = qpt.shape[1]
    nb = kcmp.shape[0]
    nt = l // TQ
    const2 = lambda i: (0, 0)
    blk = lambda d: (lambda i: (jnp.maximum(i - d, 0), 0))
    blk3 = lambda d: (lambda i: (jnp.maximum(i - d, 0), 0, 0))
    return pl.pallas_call(
        _attn_kernel,
        grid=(nt,),
        in_specs=[pl.BlockSpec((D_MODEL, TQ), lambda i: (0, i)),
                  pl.BlockSpec((D_MODEL, TQ), lambda i: (0, i)),
                  pl.BlockSpec((128, TQ), lambda i: (0, i)),
                  pl.BlockSpec((nb, KV_W), const2),
                  pl.BlockSpec((nb, KV_W), const2),
                  pl.BlockSpec((l, KV_W), const2),
                  pl.BlockSpec((nt, KV_W, TQ), lambda i: (0, 0, 0)),
                  pl.BlockSpec((TQ, KV_W), blk(2)),
                  pl.BlockSpec((TQ, KV_W), blk(1)),
                  pl.BlockSpec((TQ, KV_W), blk(0)),
                  pl.BlockSpec((1, KV_W, TQ), blk3(2)),
                  pl.BlockSpec((1, KV_W, TQ), blk3(1)),
                  pl.BlockSpec((1, KV_W, TQ), blk3(0))],
        out_specs=pl.BlockSpec((TQ, D_MODEL), lambda i: (i, 0)),
        out_shape=jax.ShapeDtypeStruct((l, D_MODEL), F32),
        scratch_shapes=[pltpu.VMEM((N_HEADS, KV_W, TQ), BF16),
                        pltpu.VMEM((KV_GROUPS, nb, TQ), F32),
                        pltpu.VMEM((N_HEADS, TQ), F32),
                        pltpu.VMEM((N_HEADS, TQ), F32),
                        pltpu.VMEM((D_MODEL, TQ), F32),
                        pltpu.VMEM((D_MODEL, TQ), F32)],
        compiler_params=_cparams("arbitrary"),
        name="attention_prompt",
    )(qpt, qrt, gt, kcmp, vcmp, ksb, vst, kwb, kwb, kwb, vwt, vwt, vwt)


def _merge_kernel(x_ref, cf_ref, ssm_ref, att_ref, gm_ref, g1_ref, wa_ref, wo_ref, o_ref):
    gm = _sigmoid(gm_ref[...])
    br_att = _dot(att_ref[...].astype(BF16), wa_ref[...])
    mixed = (gm[:, :D_MODEL] * cf_ref[...] + gm[:, D_MODEL:2 * D_MODEL] * ssm_ref[...]
             + gm[:, 2 * D_MODEL:] * br_att)
    o_ref[...] = x_ref[...] + g1_ref[...] * _dot(mixed.astype(BF16), wo_ref[...])


def _merge(x, br_cf, br_ssm, att, u, g1, w_attn, w_o, tm):
    r = x.shape[0]
    mrows = tm if g1.shape[0] == r else 1
    mmap = (lambda i: (i, 0)) if g1.shape[0] == r else (lambda i: (0, 0))
    row = lambda i: (i, 0)
    const = lambda i: (0, 0)
    return pl.pallas_call(
        _merge_kernel,
        grid=(r // tm,),
        in_specs=[pl.BlockSpec((tm, D_MODEL), row)] * 4 + [
            pl.BlockSpec((tm, 3 * D_MODEL), lambda i: (i, GMIX_OFF // (3 * D_MODEL))),
            pl.BlockSpec((mrows, D_MODEL), mmap),
            pl.BlockSpec((D_MODEL, D_MODEL), const),
            pl.BlockSpec((D_MODEL, D_MODEL), const)],
        out_specs=pl.BlockSpec((tm, D_MODEL), row),
        out_shape=jax.ShapeDtypeStruct((r, D_MODEL), F32),
        compiler_params=_cparams("arbitrary"),
        name="merge",
    )(x, br_cf, br_ssm, att, u, g1, w_attn, w_o)


FF_T = D_FF // 2


def _ffn_kernel(x_ref, g_ref, sc_ref, sh_ref, g2_ref, wg_ref, wu_ref, wo_ref, o_ref, h_ref, acc_ref):
    j = pl.program_id(1)

    @pl.when(j == 0)
    def _():
        x = x_ref[...]
        y = x * lax.rsqrt(jnp.mean(x * x, axis=-1, keepdims=True) + EPS) * g_ref[...]
        h_ref[...] = (y * (1.0 + sc_ref[...]) + sh_ref[...]).astype(BF16)
        acc_ref[...] = jnp.zeros(acc_ref.shape, F32)

    h = h_ref[...]
    act = _silu(_dot(h, wg_ref[...])) * _dot(h, wu_ref[...])
    acc_ref[...] += _dot(act.astype(BF16), wo_ref[...])

    @pl.when(j == pl.num_programs(1) - 1)
    def _():
        o_ref[...] = x_ref[...] + g2_ref[...] * acc_ref[...]


def _ffn(x, g, sc, sh, g2, w_in, w_out, tm):
    r = x.shape[0]
    nj = D_FF // FF_T
    mrows = tm if sc.shape[0] == r else 1
    mmap = (lambda i, j: (i, 0)) if sc.shape[0] == r else (lambda i, j: (0, 0))
    row = lambda i, j: (i, 0)
    return pl.pallas_call(
        _ffn_kernel,
        grid=(r // tm, nj),
        in_specs=[pl.BlockSpec((tm, D_MODEL), row),
                  pl.BlockSpec((1, D_MODEL), lambda i, j: (0, 0)),
                  pl.BlockSpec((mrows, D_MODEL), mmap),
                  pl.BlockSpec((mrows, D_MODEL), mmap),
                  pl.BlockSpec((mrows, D_MODEL), mmap),
                  pl.BlockSpec((D_MODEL, FF_T), lambda i, j: (0, j)),
                  pl.BlockSpec((D_MODEL, FF_T), lambda i, j: (0, nj + j)),
                  pl.BlockSpec((FF_T, D_MODEL), lambda i, j: (j, 0))],
        out_specs=pl.BlockSpec((tm, D_MODEL), row),
        out_shape=jax.ShapeDtypeStruct((r, D_MODEL), F32),
        scratch_shapes=[pltpu.VMEM((tm, D_MODEL), BF16), pltpu.VMEM((tm, D_MODEL), F32)],
        compiler_params=_cparams("arbitrary", "arbitrary"),
        name="ffn",
    )(x, g, sc, sh, g2, w_in, w_in, w_out)


def _final_norm_kernel(x_ref, g_ref, o_ref):
    x = x_ref[...]
    o_ref[...] = x * lax.rsqrt(jnp.mean(x * x, axis=-1, keepdims=True) + EPS) * g_ref[...]


def _final_norm(x, g, tm):
    r = x.shape[0]
    return pl.pallas_call(
        _final_norm_kernel,
        grid=(r // tm,),
        in_specs=[pl.BlockSpec((tm, D_MODEL), lambda i: (i, 0)), pl.BlockSpec((1, D_MODEL), lambda i: (0, 0))],
        out_specs=pl.BlockSpec((tm, D_MODEL), lambda i: (i, 0)),
        out_shape=jax.ShapeDtypeStruct((r, D_MODEL), F32),
        compiler_params=_cparams("arbitrary"),
        name="final_norm",
    )(x, g)


def _sample_mix_kernel(ucf_ref, z_ref, xbc_ref, sm_ref, cst_ref, sst_ref, h0_ref,
                       dww_ref, dwb_ref, lg_ref, lb_ref, cw_ref, cb_ref, dtb_ref, alog_ref,
                       dsk_ref, ng_ref, rexp_ref,
                       cv_ref, yn_ref, glu_ref, hnew_ref):
    u = ucf_ref[...]
    glu = u[:, :D_MODEL] * _sigmoid(u[:, D_MODEL:])
    glu_ref[...] = glu
    conv = (dwb_ref[...] + dww_ref[CONV_K - 1:CONV_K, :] * glu
            + jnp.sum(dww_ref[0:CONV_K - 1, :] * cst_ref[...], axis=0, keepdims=True))
    cv_ref[...] = _layernorm_silu(conv, lg_ref[...], lb_ref[...])

    xbc = xbc_ref[...]
    conv = (cb_ref[...] + cw_ref[SSM_CONV_K - 1:SSM_CONV_K, :] * xbc
            + jnp.sum(cw_ref[0:SSM_CONV_K - 1, :] * sst_ref[...], axis=0, keepdims=True))
    xa = _silu(conv)
    xs = xa[:, :D_INNER]
    bm = xa[:, D_INNER:D_INNER + SSM_GROUPS * D_STATE]
    cm = xa[:, D_INNER + SSM_GROUPS * D_STATE:]
    dt = _softplus(sm_ref[...] + dtb_ref[...])
    dec = jnp.exp(dt * (-jnp.exp(alog_ref[...])))
    dtx = _dot_exact_rhs01(jnp.broadcast_to(dt, (8, 128)), rexp_ref[...])[0:1, :]
    xin = xs * dtx
    dec_col = jnp.broadcast_to(dec, (128, 128)).T
    nblk = D_INNER // 128
    xrows = jnp.concatenate([xin[:, a * 128:(a + 1) * 128] for a in range(nblk)]
                            + [jnp.zeros((128 - nblk, 128), F32)], axis=0)
    x_col = xrows.T
    blocks_per_group = (D_INNER // SSM_GROUPS) // 128
    for a in range(nblk):
        g = a // blocks_per_group
        bg = bm[:, g * D_STATE:(g + 1) * D_STATE]
        dcol = jnp.concatenate([jnp.broadcast_to(dec_col[2 * a:2 * a + 1, :], (SSM_HEADDIM, 128)),
                                jnp.broadcast_to(dec_col[2 * a + 1:2 * a + 2, :], (SSM_HEADDIM, 128))], axis=0)
        hnew_ref[a * 128:(a + 1) * 128, :] = h0_ref[a * 128:(a + 1) * 128, :] * dcol + x_col[:, a:a + 1] * bg
    c8 = jnp.concatenate([cm[:, g * D_STATE:(g + 1) * D_STATE] for g in range(SSM_GROUPS)]
                         + [jnp.zeros((8 - SSM_GROUPS, D_STATE), F32)], axis=0)
    yall = _dot_nt(c8.astype(BF16), hnew_ref[...].astype(BF16))
    lane = lax.broadcasted_iota(jnp.int32, (1, D_INNER), 1)
    gw = D_INNER // SSM_GROUPS
    y = jnp.zeros((1, D_INNER), F32)
    for g in range(SSM_GROUPS):
        y = y + jnp.where((lane >= g * gw) & (lane < (g + 1) * gw), yall[g:g + 1, :], 0.0)
    y = y + dsk_ref[...] * xs
    yn_ref[...] = _gated_group_norm(y, z_ref[...], ng_ref[...])


def _sample_mix(u3, conv_st, sconv_st, h0, dw_w, dw_b, ln_g, ln_b, conv_w, conv_b, dt_bias, a_log,
                dskip_x, norm_g, rexp):
    b = u3.shape[0]
    const = lambda i: (0, 0)
    ublk = lambda w, off: pl.BlockSpec((None, 1, w), lambda i: (i, 0, off // w))
    out1 = lambda w: pl.BlockSpec((None, 1, w), lambda i: (i, 0, 0))
    return pl.pallas_call(
        _sample_mix_kernel,
        grid=(b,),
        in_specs=[ublk(2 * D_MODEL, UCF_OFF), ublk(D_INNER, Z_OFF), ublk(SSM_CONV_DIM, XBC_OFF),
                  ublk(128, SMALL_OFF),
                  pl.BlockSpec((None, CONV_K - 1, D_MODEL), lambda i: (i, 0, 0)),
                  pl.BlockSpec((None, SSM_CONV_K - 1, SSM_CONV_DIM), lambda i: (i, 0, 0)),
                  pl.BlockSpec((None, D_INNER, D_STATE), lambda i: (i, 0, 0)),
                  pl.BlockSpec((CONV_K, D_MODEL), const),
                  pl.BlockSpec((1, D_MODEL), const),
                  pl.BlockSpec((1, D_MODEL), const),
                  pl.BlockSpec((1, D_MODEL), const),
                  pl.BlockSpec((SSM_CONV_K, SSM_CONV_DIM), const),
                  pl.BlockSpec((1, SSM_CONV_DIM), const),
                  pl.BlockSpec((1, 128), const),
                  pl.BlockSpec((1, 128), const),
                  pl.BlockSpec((1, D_INNER), const),
                  pl.BlockSpec((1, D_INNER), const),
                  pl.BlockSpec((128, D_INNER), const)],
        out_specs=[out1(D_MODEL), out1(D_INNER), out1(D_MODEL),
                   pl.BlockSpec((None, D_INNER, D_STATE), lambda i: (i, 0, 0))],
        out_shape=[jax.ShapeDtypeStruct((b, 1, D_MODEL), F32),
                   jax.ShapeDtypeStruct((b, 1, D_INNER), F32),
                   jax.ShapeDtypeStruct((b, 1, D_MODEL), F32),
                   jax.ShapeDtypeStruct((b, D_INNER, D_STATE), F32)],
        compiler_params=_cparams("arbitrary"),
        name="sample_mix",
    )(u3, u3, u3, u3, conv_st, sconv_st, h0, dw_w, dw_b, ln_g, ln_b, conv_w, conv_b,
      _pad_lanes(dt_bias), _pad_lanes(a_log), dskip_x, norm_g, rexp)


SEL_LANES = 128


def _sample_cmp_kernel(pt_ref, qp_ref, kpool_ref, vpool_ref, oc_ref, idx_ref, kbuf, vbuf, *, n_pages, t_pos):
    b = pl.program_id(0)
    nb = 2 * n_pages
    for pg in range(n_pages):
        src = pt_ref[b, pg] * 2
        kbuf[2 * pg:2 * pg + 2, :] = kpool_ref[pl.ds(src, 2), :]
        vbuf[2 * pg:2 * pg + 2, :] = vpool_ref[pl.ds(src, 2), :]
    qp = qp_ref[...]
    s = _dot_nt(qp, kbuf[...].astype(BF16))
    jl = lax.broadcasted_iota(jnp.int32, (N_HEADS, nb), 1)
    mask_c = (jl + 1) * BLOCK - 1 <= t_pos
    s = jnp.where(mask_c, s, NEG)
    m = jnp.max(s, axis=1, keepdims=True)
    e = jnp.where(mask_c, jnp.exp(s - m), 0.0)
    p = e * (1.0 / jnp.maximum(jnp.sum(e, axis=1, keepdims=True), 1e-30))
    oc_ref[...] = _dot(p.astype(BF16), vbuf[...].astype(BF16))

    nbs = t_pos // BLOCK + 1
    width = ((nbs + 127) // 128) * 128
    imp = jnp.concatenate(
        [jnp.sum(p[g * HEADS_PER_GROUP:(g + 1) * HEADS_PER_GROUP, :], axis=0, keepdims=True)
         for g in range(KV_GROUPS)] + [jnp.zeros((8 - KV_GROUPS, nb), F32)], axis=0)
    imp = jnp.concatenate([imp, jnp.zeros((8, width - nb), F32)], axis=1)
    jw = lax.broadcasted_iota(jnp.int32, (8, width), 1)
    cur = t_pos // BLOCK
    valid = jw <= cur
    forced = valid & ((jw == 0) | (jw == cur) | (jw == cur - 1))
    score = jnp.where(forced, BIG, jnp.where(valid, imp, -BIG))
    work = jnp.where(jw < nbs, score, -jnp.inf)
    lane = lax.broadcasted_iota(jnp.int32, (8, SEL_LANES), 1)
    out = jnp.full((8, SEL_LANES), -1, jnp.int32)
    for it in range(TOP_N):
        mx = jnp.max(work, axis=1, keepdims=True)
        first = jnp.min(jnp.where(work == mx, jw, width), axis=1, keepdims=True)
        out = jnp.where(lane == it, jnp.where(mx > -0.5 * BIG, first, -1), out)
        work = jnp.where(jw == first, -jnp.inf, work)
    idx_ref[...] = out


def _sample_cmp(page_table, qp_pad, kpool, vpool, t_pos):
    b, n_pages = page_table.shape
    npool = kpool.shape[0]
    nb = 2 * n_pages
    kern = functools.partial(_sample_cmp_kernel, n_pages=n_pages, t_pos=t_pos)
    return pl.pallas_call(
        kern,
        grid_spec=pltpu.PrefetchScalarGridSpec(
            num_scalar_prefetch=1,
            grid=(b,),
            in_specs=[pl.BlockSpec((None, N_HEADS, KV_W), lambda i, pt: (i, 0, 0)),
                      pl.BlockSpec((npool, KV_W), lambda i, pt: (0, 0)),
                      pl.BlockSpec((npool, KV_W), lambda i, pt: (0, 0))],
            out_specs=[pl.BlockSpec((None, N_HEADS, KV_W), lambda i, pt: (i, 0, 0)),
                       pl.BlockSpec((None, 8, SEL_LANES), lambda i, pt: (i, 0, 0))],
            scratch_shapes=[pltpu.VMEM((nb, KV_W), F32), pltpu.VMEM((nb, KV_W), F32)]),
        out_shape=[jax.ShapeDtypeStruct((b, N_HEADS, KV_W), F32),
                   jax.ShapeDtypeStruct((b, 8, SEL_LANES), jnp.int32)],
        compiler_params=_cparams("arbitrary"),
        name="sample_cmp",
    )(page_table, qp_pad, kpool, vpool)


def _sample_attn_kernel(idx_ref, pt_ref, qr_ref, new_ref, wk_ref, wv_ref, oc_ref, gate_ref,
                        kcache_ref, vcache_ref, o_ref, kbuf, vbuf, sem, *, n_pages, t_pos):
    b = pl.program_id(0)
    nb_past = 2 * n_pages

    def block_copy(cache_ref, buf, g, k, which):
        j = idx_ref[b, g, k]
        jj = jnp.where((j >= 0) & (j < nb_past), j, 0)
        src = pt_ref[b, jj // 2] * 2 + jj % 2
        return pltpu.make_async_copy(cache_ref.at[src], buf.at[g, pl.ds(k * BLOCK, BLOCK), :], sem.at[which, g, k])

    for g in range(KV_GROUPS):
        for k in range(TOP_N):
            block_copy(kcache_ref, kbuf, g, k, 0).start()
            block_copy(vcache_ref, vbuf, g, k, 1).start()

    qr = qr_ref[...]
    qf = qr.astype(F32)
    new = new_ref[...]
    grp_lo = lax.broadcasted_iota(jnp.int32, (N_HEADS, KV_W), 0) < HEADS_PER_GROUP
    lane_lo = lax.broadcasted_iota(jnp.int32, (N_HEADS, KV_W), 1) < HEAD_DIM
    own = grp_lo == lane_lo

    win = wk_ref.shape[0]
    s = _dot_nt(qr, wk_ref[...].astype(BF16))
    wpos = t_pos - win + lax.broadcasted_iota(jnp.int32, (N_HEADS, win), 1)
    mask_w = t_pos - wpos < WINDOW
    s = jnp.where(mask_w, s, NEG)
    s_new = jnp.sum(qf * new[2:3, :], axis=1, keepdims=True)
    m = jnp.maximum(jnp.max(s, axis=1, keepdims=True), s_new)
    p = jnp.where(mask_w, jnp.exp(s - m), 0.0)
    p_new = jnp.exp(s_new - m)
    lsum = jnp.sum(p, axis=1, keepdims=True) + p_new
    o_w = (_dot(p.astype(BF16), wv_ref[...].astype(BF16)) + p_new * new[3:4, :]) * (1.0 / lsum)

    for g in range(KV_GROUPS):
        for k in range(TOP_N):
            block_copy(kcache_ref, kbuf, g, k, 0).wait()
            block_copy(vcache_ref, vbuf, g, k, 1).wait()
    nsel = TOP_N * BLOCK
    lane_blk = lax.broadcasted_iota(jnp.int32, (1, nsel), 1) // BLOCK
    o_s_parts = []
    for g in range(KV_GROUPS):
        in_cache = jnp.zeros((1, nsel), jnp.int32)
        has_new = jnp.zeros((1, 1), jnp.int32)
        for k in range(TOP_N):
            j = idx_ref[b, g, k]
            in_cache = jnp.where(lane_blk == k, ((j >= 0) & (j < nb_past)).astype(jnp.int32), in_cache)
            has_new = jnp.maximum(has_new, (j == nb_past).astype(jnp.int32))
        mask_s = in_cache > 0
        mask_n = has_new > 0
        qg = qr[g * HEADS_PER_GROUP:(g + 1) * HEADS_PER_GROUP, :]
        s = jnp.where(mask_s, _dot_nt(qg, kbuf[g].astype(BF16)), NEG)
        s_new = jnp.where(mask_n, jnp.sum(qg.astype(F32) * new[0:1, :], axis=1, keepdims=True), NEG)
        m = jnp.maximum(jnp.max(s, axis=1, keepdims=True), s_new)
        p = jnp.where(mask_s, jnp.exp(s - m), 0.0)
        p_new = jnp.where(mask_n, jnp.exp(s_new - m), 0.0)
        lsum = jnp.maximum(jnp.sum(p, axis=1, keepdims=True) + p_new, 1e-30)
        o_s_parts.append((_dot(p.astype(BF16), vbuf[g].astype(BF16)) + p_new * new[1:2, :]) * (1.0 / lsum))
    o_s = jnp.concatenate(o_s_parts, axis=0)

    gate = gate_ref[...]
    mix = gate[:, 0:1] * oc_ref[...] + gate[:, 1:2] * o_s + gate[:, 2:3] * o_w
    mix = jnp.where(own, mix, 0.0)
    o_ref[...] = mix[:, :HEAD_DIM] + mix[:, HEAD_DIM:]


def _sample_attn(idx, page_table, qr_pad, new_rows, win_k, win_v, o_c, gates, kcache, vcache, t_pos):
    b, n_pages = page_table.shape
    win = win_k.shape[1]
    kern = functools.partial(_sample_attn_kernel, n_pages=n_pages, t_pos=t_pos)
    per_b = lambda r, c: pl.BlockSpec((None, r, c), lambda i, ix, pt: (i, 0, 0))
    return pl.pallas_call(
        kern,
        grid_spec=pltpu.PrefetchScalarGridSpec(
            num_scalar_prefetch=2,
            grid=(b,),
            in_specs=[per_b(N_HEADS, KV_W), per_b(8, KV_W), per_b(win, KV_W), per_b(win, KV_W),
                      per_b(N_HEADS, KV_W), per_b(N_HEADS, KV_W),
                      pl.BlockSpec(memory_space=pl.ANY), pl.BlockSpec(memory_space=pl.ANY)],
            out_specs=per_b(N_HEADS, HEAD_DIM),
            scratch_shapes=[pltpu.VMEM((KV_GROUPS, TOP_N * BLOCK, KV_W), F32),
                            pltpu.VMEM((KV_GROUPS, TOP_N * BLOCK, KV_W), F32),
                            pltpu.SemaphoreType.DMA((2, KV_GROUPS, TOP_N))]),
        out_shape=jax.ShapeDtypeStruct((b, N_HEADS, HEAD_DIM), F32),
        compiler_params=_cparams("arbitrary"),
        name="sample_attn",
    )(idx, page_table, qr_pad, new_rows, win_k, win_v, o_c, gates, kcache, vcache)


def _pack_w_in(w):
    segs = [w[:, 4096:7168], w[:, 9040:12112], w[:, 0:2048], w[:, 2048:4096], w[:, 7200:8224],
            w[:, 8224:8992], w[:, 7168:7200], w[:, 8992:9040],
            jnp.zeros((D_MODEL, N_PACK - 12112), w.dtype)]
    return jnp.concatenate(segs, axis=1).astype(BF16)


def _expand_cmp_weights(pe, w1, w2):
    eye = jnp.eye(KV_GROUPS, dtype=F32)
    w1r = w1.reshape(BLOCK, HEAD_DIM, CMP_HIDDEN)
    w1e = jnp.einsum('ldh,gk->lgdkh', w1r, eye).reshape(BLOCK * KV_W, KV_GROUPS * CMP_HIDDEN).astype(BF16)
    w2e = jnp.einsum('hd,gk->ghkd', w2, eye).reshape(KV_GROUPS * CMP_HIDDEN, KV_W).astype(BF16)
    pe2 = jnp.broadcast_to(pe[:, None, :], (BLOCK, KV_GROUPS, HEAD_DIM)).reshape(1, BLOCK * KV_W)
    return pe2, w1e, w2e


def _rope_tables(pos):
    half = ROT_DIM // 2
    inv = ROPE_THETA ** (-(jnp.arange(half, dtype=F32) * 2.0 / ROT_DIM))
    ang = pos.astype(F32)[:, None] * inv[None, :]
    cos = jnp.cos(ang)
    sin = jnp.sin(ang)
    n = pos.shape[0]
    one = jnp.ones((n, HEAD_DIM - ROT_DIM), F32)
    zero8 = jnp.zeros((n, half), F32)
    zero = jnp.zeros((n, HEAD_DIM - ROT_DIM), F32)
    c = jnp.concatenate([cos, cos, one], axis=1)
    s1 = jnp.concatenate([zero8, sin, zero], axis=1)
    s2 = jnp.concatenate([-sin, zero8, zero], axis=1)
    return jnp.stack([jnp.concatenate([t, t], axis=1) for t in (c, s1, s2)])


def _head_expand_matrix():
    h = np.arange(128)[:, None]
    lane = np.arange(D_INNER)[None, :]
    return jnp.asarray((lane // SSM_HEADDIM == h).astype(np.float32), dtype=BF16)


def _group_padded(q):
    b = q.shape[0]
    qh = q.reshape(b, KV_GROUPS, HEADS_PER_GROUP, HEAD_DIM)
    z = jnp.zeros_like(qh[:, 0])
    lo = jnp.concatenate([qh[:, 0], z], axis=-1)
    hi = jnp.concatenate([z, qh[:, 1]], axis=-1)
    return jnp.concatenate([lo, hi], axis=1).astype(BF16)


def _layer_weights(l, p):
    w = {k: v[l] for k, v in p.items()}
    w['w_in_p'] = _pack_w_in(w['w_in'])
    for nm in ('w_cf_out', 'w_ssm_out', 'w_attn_out', 'w_o', 'w_ffn_in', 'w_ffn_out'):
        w[nm] = w[nm].astype(BF16)
    w['cmp_k'] = _expand_cmp_weights(w['cmp_pe_k'], w['cmp_w1_k'], w['cmp_w2_k'])
    w['cmp_v'] = _expand_cmp_weights(w['cmp_pe_v'], w['cmp_w1_v'], w['cmp_w2_v'])
    w['dskip_x'] = jnp.repeat(w['d_skip'], SSM_HEADDIM)[None, :]
    for nm in ('norm1_g', 'norm2_g', 'cf_dw_b', 'cf_ln_g', 'cf_ln_b', 'ssm_conv_b', 'dt_bias', 'a_log',
               'ssm_norm_g'):
        w[nm] = w[nm][None, :]
    return w


def _prompt_layer(x, mod, w, tab, rexp):
    l = x.shape[0]
    sh1, sc1, g1, sh2, sc2, g2 = [mod[:, k * D_MODEL:(k + 1) * D_MODEL] for k in range(6)]
    u = _in_proj(x, w['norm1_g'], sc1, sh1, w['w_in_p'], 1024)
    br_cf, conv_tail = _conformer_prompt(u, w['cf_dw_w'], w['cf_dw_b'], w['cf_ln_g'], w['cf_ln_b'], w['w_cf_out'])
    br_ssm, h_new, sconv_tail = _ssd_prompt(u, w['ssm_conv_w'], w['ssm_conv_b'], w['dt_bias'], w['a_log'],
                                            w['dskip_x'], w['ssm_norm_g'], rexp, w['w_ssm_out'])
    kc, vc, ks, vs, kw, vw, qpt, qrt, ksb, vst, kwb, vwt, gt = _prep_prompt(u, tab)
    kcmp = _compress(kc.reshape(l // BLOCK, BLOCK * KV_W), *w['cmp_k'])
    vcmp = _compress(vc.reshape(l // BLOCK, BLOCK * KV_W), *w['cmp_v'])
    att = _attention_prompt(qpt, qrt, gt, kcmp, vcmp, ksb, vst, kwb, vwt)
    x = _merge(x, br_cf, br_ssm, att, u, g1, w['w_attn_out'], w['w_o'], 512)
    x = _ffn(x, w['norm2_g'], sc2, sh2, g2, w['w_ffn_in'], w['w_ffn_out'], 512)
    keep = min(WINDOW, l)
    kv4 = lambda a: a.reshape(1, -1, KV_GROUPS, HEAD_DIM)
    state = (kv4(kc), kv4(vc), kv4(ks), kv4(vs), kv4(kw[l - keep:]), kv4(vw[l - keep:]),
             conv_tail[CF_HALO - (CONV_K - 1):][None],
             sconv_tail[8 - (SSM_CONV_K - 1):][None],
             h_new.reshape(1, SSM_HEADS, SSM_HEADDIM, D_STATE))
    return x, state


def _sample_layer(x, mod, w, tab, rexp, caches, states, page_table, t_pos):
    b = x.shape[0]
    cache_ck, cache_cv, cache_sk, cache_sv = caches
    win_k, win_v, conv_st, sconv_st, ssm_st = states
    n_pool = cache_ck.shape[0]
    sh1, sc1, g1, sh2, sc2, g2 = [mod[:, k * D_MODEL:(k + 1) * D_MODEL] for k in range(6)]
    u = _in_proj(x, w['norm1_g'], sc1, sh1, w['w_in_p'], b)
    cv, yn, glu, h_new = _sample_mix(u.reshape(b, 1, N_PACK), conv_st, sconv_st,
                                     ssm_st.reshape(b, D_INNER, D_STATE),
                                     w['cf_dw_w'], w['cf_dw_b'], w['cf_ln_g'], w['cf_ln_b'],
                                     w['ssm_conv_w'], w['ssm_conv_b'], w['dt_bias'], w['a_log'],
                                     w['dskip_x'], w['ssm_norm_g'], rexp)
    br_cf = _matmul(cv.reshape(b, D_MODEL), w['w_cf_out'])
    br_ssm = _matmul(yn.reshape(b, D_INNER), w['w_ssm_out'])
    qp, qr, kvo = _prep_sample(u, tab)
    seg = lambda k: kvo[:, k * KV_W:(k + 1) * KV_W]
    kc, vc, ks, vs, kw, vw = [seg(k) for k in range(6)]
    gates = seg(6)[:, 32:32 + 3 * N_HEADS].reshape(b, 3, N_HEADS).transpose(0, 2, 1)
    gates = jnp.concatenate([gates, jnp.zeros((b, N_HEADS, KV_W - 3), F32)], axis=-1)
    kpool = _compress(cache_ck.reshape(n_pool * 2, BLOCK * KV_W), *w['cmp_k'])
    vpool = _compress(cache_cv.reshape(n_pool * 2, BLOCK * KV_W), *w['cmp_v'])
    o_c, idx = _sample_cmp(page_table, _group_padded(qp), kpool, vpool, t_pos)
    new_rows = jnp.stack([ks, vs, kw, vw] + [jnp.zeros_like(ks)] * 4, axis=1)
    att = _sample_attn(idx[:, :KV_GROUPS, :TOP_N], page_table, _group_padded(qr), new_rows,
                       win_k.reshape(b, -1, KV_W), win_v.reshape(b, -1, KV_W), o_c, gates,
                       cache_sk.reshape(n_pool * 2, BLOCK, KV_W), cache_sv.reshape(n_pool * 2, BLOCK, KV_W), t_pos)
    x = _merge(x, br_cf, br_ssm, att.reshape(b, D_MODEL), u, g1, w['w_attn_out'], w['w_o'], b)
    x = _ffn(x, w['norm2_g'], sc2, sh2, g2, w['w_ffn_in'], w['w_ffn_out'], b)
    keep = min(WINDOW, t_pos + 1)
    kv4 = lambda a: a.reshape(b, 1, KV_GROUPS, HEAD_DIM)
    new_kw = jnp.concatenate([win_k, kv4(kw)], axis=1)
    new_vw = jnp.concatenate([win_v, kv4(vw)], axis=1)
    state = (kv4(kc), kv4(vc), kv4(ks), kv4(vs),
             new_kw[:, new_kw.shape[1] - keep:], new_vw[:, new_vw.shape[1] - keep:],
             jnp.concatenate([conv_st[:, 1:], glu], axis=1),
             jnp.concatenate([sconv_st[:, 1:], u[:, None, XBC_OFF:XBC_OFF + SSM_CONV_DIM]], axis=1),
             h_new.reshape(b, SSM_HEADS, SSM_HEADDIM, D_STATE))
    return x, state


def kernel(x_prompt, x_sample, c_prompt, c_sample, cache_cmp_k, cache_cmp_v, cache_slc_k, cache_slc_v, state_win_k, state_win_v, state_conv, state_ssm_conv, state_ssm, page_table, norm1_g, w_ada, b_ada, w_in, cf_dw_w, cf_dw_b, cf_ln_g, cf_ln_b, w_cf_out, ssm_conv_w, ssm_conv_b, dt_bias, a_log, d_skip, ssm_norm_g, w_ssm_out, cmp_pe_k, cmp_pe_v, cmp_w1_k, cmp_w2_k, cmp_w1_v, cmp_w2_v, w_attn_out, w_o, norm2_g, w_ffn_in, w_ffn_out, final_g):
    depth = w_in.shape[0]
    bp, lp = x_prompt.shape[0], x_prompt.shape[1]
    bs = x_sample.shape[0]
    assert bp == 1 and x_sample.shape[1] == 1
    assert lp % TK == 0 and lp // BLOCK >= TOP_N
    t_pos = page_table.shape[1] * PAGE_SIZE
    params = dict(norm1_g=norm1_g, w_in=w_in, cf_dw_w=cf_dw_w, cf_dw_b=cf_dw_b, cf_ln_g=cf_ln_g, cf_ln_b=cf_ln_b,
                  w_cf_out=w_cf_out, ssm_conv_w=ssm_conv_w, ssm_conv_b=ssm_conv_b, dt_bias=dt_bias, a_log=a_log,
                  d_skip=d_skip, ssm_norm_g=ssm_norm_g, w_ssm_out=w_ssm_out, cmp_pe_k=cmp_pe_k, cmp_pe_v=cmp_pe_v,
                  cmp_w1_k=cmp_w1_k, cmp_w2_k=cmp_w2_k, cmp_w1_v=cmp_w1_v, cmp_w2_v=cmp_w2_v,
                  w_attn_out=w_attn_out, w_o=w_o, norm2_g=norm2_g, w_ffn_in=w_ffn_in, w_ffn_out=w_ffn_out)
    n_c = bp + bs
    c_pad = (-n_c) % 8
    c_all = jnp.concatenate([c_prompt, c_sample, jnp.zeros((c_pad, D_MODEL), F32)], axis=0)
    mod = _modulation(c_all, w_ada, b_ada)
    rexp = _head_expand_matrix()
    tab_p = _rope_tables(jnp.arange(lp, dtype=jnp.int32))
    tab_s = _rope_tables(jnp.full((bs,), t_pos, jnp.int32))
    pool4 = lambda c: c.reshape(c.shape[0], PAGE_SIZE, KV_W)

    xp = x_prompt.reshape(lp, D_MODEL)
    xs = x_sample.reshape(bs, D_MODEL)
    outs_p = [[] for _ in range(9)]
    outs_s = [[] for _ in range(9)]
    for l in range(depth):
        w = _layer_weights(l, params)
        xp, st_p = _prompt_layer(xp, mod[l, 0:bp], w, tab_p, rexp)
        caches = tuple(pool4(c[l]) for c in (cache_cmp_k, cache_cmp_v, cache_slc_k, cache_slc_v))
        states = (state_win_k[l], state_win_v[l], state_conv[l], state_ssm_conv[l], state_ssm[l])
        xs, st_s = _sample_layer(xs, mod[l, bp:bp + bs], w, tab_s, rexp, caches, states, page_table, t_pos)
        for k in range(9):
            outs_p[k].append(st_p[k])
            outs_s[k].append(st_s[k])
    y_prompt = _final_norm(xp, final_g[None, :], 512).reshape(bp, lp, D_MODEL)
    y_sample = _final_norm(xs, final_g[None, :], bs).reshape(bs, 1, D_MODEL)
    np_ = [jnp.stack(o) for o in outs_p]
    ns_ = [jnp.stack(o) for o in outs_s]
    return (y_prompt, y_sample, *np_, *ns_)
```

```python
import functools

import numpy as np
import jax
import jax.numpy as jnp
from jax import lax
from jax.experimental import pallas as pl
from jax.experimental.pallas import tpu as pltpu

F32 = jnp.float32
BF16 = jnp.bfloat16

D_MODEL = 1024
PAGE_SIZE = 128
CONV_K = 31
D_INNER = 2 * D_MODEL
SSM_HEADDIM = 64
SSM_HEADS = D_INNER // SSM_HEADDIM
SSM_GROUPS = 4
D_STATE = 128
SSM_CONV_K = 4
SSM_CONV_DIM = D_INNER + 2 * SSM_GROUPS * D_STATE
SSM_CHUNK = 128
N_HEADS = 16
HEAD_DIM = 64
KV_GROUPS = 2
HEADS_PER_GROUP = N_HEADS // KV_GROUPS
BLOCK = 64
TOP_N = 16
WINDOW = 512
CMP_HIDDEN = 256
ROPE_THETA = 500000.0
ROT_DIM = HEAD_DIM // 4
BIG = 1e9
NEG = -1e30
D_FF = ((8 * D_MODEL + 3 * 256 - 1) // (3 * 256)) * 256
EPS = 1e-6
KV_W = KV_GROUPS * HEAD_DIM

XBC_OFF = 0
GMIX_OFF = 3072
UCF_OFF = 6144
Z_OFF = 8192
Q_OFF = 10240
KV_OFF = 11264
SMALL_OFF = KV_OFF + 6 * KV_W
N_PACK = 12288

TQ = 256
TK = 512
V_ROWS = HEAD_DIM + 16
Q_SCALE = HEAD_DIM ** -0.5 * 1.4426950408889634

VMEM_LIMIT = 56 * 1024 * 1024


def _cparams(*sem):
    return pltpu.CompilerParams(dimension_semantics=sem, vmem_limit_bytes=VMEM_LIMIT)


def _dot(a, b):
    return jnp.dot(a, b, preferred_element_type=F32)


def _dot_nt(a, b):
    return lax.dot_general(a, b, (((1,), (1,)), ((), ())), preferred_element_type=F32)


def _dot_tn(a, b):
    return lax.dot_general(a, b, (((0,), (0,)), ((), ())), preferred_element_type=F32)


def _sigmoid(x):
    return jax.nn.sigmoid(x)


def _silu(x):
    return x * jax.nn.sigmoid(x)


def _softplus(x):
    return jnp.maximum(x, 0.0) + jnp.log(1.0 + jnp.exp(-jnp.abs(x)))


def _split3(x):
    hi = x.astype(BF16)
    r1 = x - hi.astype(F32)
    mid = r1.astype(BF16)
    lo = (r1 - mid.astype(F32)).astype(BF16)
    return hi, mid, lo


def _dot_exact_rhs01(x, m01):
    hi, mid, lo = _split3(x)
    return _dot(hi, m01) + _dot(mid, m01) + _dot(lo, m01)


def _dot_exact_lhs01(m01, x):
    hi, mid, lo = _split3(x)
    return _dot(m01, hi) + _dot(m01, mid) + _dot(m01, lo)


def _mod_kernel(c_ref, w_ref, b_ref, o_ref):
    c = c_ref[...]
    o_ref[0] = _dot(_silu(c).astype(BF16), w_ref[0].astype(BF16)) + b_ref[0]


def _modulation(c_all, w_ada, b_ada):
    depth = w_ada.shape[0]
    bc = c_all.shape[0]
    n = w_ada.shape[2]
    tn = 1536
    return pl.pallas_call(
        _mod_kernel,
        grid=(depth, n // tn),
        in_specs=[pl.BlockSpec((bc, D_MODEL), lambda l, j: (0, 0)),
                  pl.BlockSpec((1, D_MODEL, tn), lambda l, j: (l, 0, j)),
                  pl.BlockSpec((1, 1, tn), lambda l, j: (l, 0, j))],
        out_specs=pl.BlockSpec((1, bc, tn), lambda l, j: (l, 0, j)),
        out_shape=jax.ShapeDtypeStruct((depth, bc, n), F32),
        compiler_params=_cparams("arbitrary", "arbitrary"),
        name="modulation",
    )(c_all, w_ada, b_ada.reshape(depth, 1, n))


def _in_proj_kernel(x_ref, g_ref, sc_ref, sh_ref, w_ref, o_ref, h_ref):
    @pl.when(pl.program_id(1) == 0)
    def _():
        x = x_ref[...]
        y = x * lax.rsqrt(jnp.mean(x * x, axis=-1, keepdims=True) + EPS) * g_ref[...]
        h_ref[...] = (y * (1.0 + sc_ref[...]) + sh_ref[...]).astype(BF16)

    o_ref[...] = _dot(h_ref[...], w_ref[...])


def _in_proj(x, g, sc, sh, w, tm):
    r = x.shape[0]
    n = w.shape[1]
    tn = 1024
    mrows = tm if sc.shape[0] == r else 1
    mmap = (lambda i, j: (i, 0)) if sc.shape[0] == r else (lambda i, j: (0, 0))
    return pl.pallas_call(
        _in_proj_kernel,
        grid=(r // tm, n // tn),
        in_specs=[pl.BlockSpec((tm, D_MODEL), lambda i, j: (i, 0)),
                  pl.BlockSpec((1, D_MODEL), lambda i, j: (0, 0)),
                  pl.BlockSpec((mrows, D_MODEL), mmap),
                  pl.BlockSpec((mrows, D_MODEL), mmap),
                  pl.BlockSpec((D_MODEL, tn), lambda i, j: (0, j))],
        out_specs=pl.BlockSpec((tm, tn), lambda i, j: (i, j)),
        out_shape=jax.ShapeDtypeStruct((r, n), F32),
        scratch_shapes=[pltpu.VMEM((tm, D_MODEL), BF16)],
        compiler_params=_cparams("arbitrary", "arbitrary"),
        name="in_proj",
    )(x, g, sc, sh, w)


def _matmul_kernel(x_ref, w_ref, o_ref):
    o_ref[...] = _dot(x_ref[...].astype(BF16), w_ref[...])


def _matmul(x, w):
    r, k = x.shape
    n = w.shape[1]
    return pl.pallas_call(
        _matmul_kernel,
        grid=(1,),
        in_specs=[pl.BlockSpec((r, k), lambda i: (0, 0)), pl.BlockSpec((k, n), lambda i: (0, 0))],
        out_specs=pl.BlockSpec((r, n), lambda i: (0, 0)),
        out_shape=jax.ShapeDtypeStruct((r, n), F32),
        compiler_params=_cparams("arbitrary"),
        name="row_matmul",
    )(x, w)


CF_TL = 256
CF_RC = 32
CF_HALO = 32


def _layernorm_silu(x, g, b):
    xc = x - jnp.mean(x, axis=-1, keepdims=True)
    y = xc * lax.rsqrt(jnp.mean(xc * xc, axis=-1, keepdims=True) + EPS)
    return _silu(y * g + b)


def _conformer_kernel(u_ref, w_ref, b_ref, lg_ref, lb_ref, wo_ref, o_ref, nc_ref, bufs, cvb):
    i = pl.program_id(0)
    tl = CF_TL

    @pl.when(i == 0)
    def _():
        bufs[0, 0:CF_HALO, :] = jnp.zeros((CF_HALO, D_MODEL), F32)

    u = u_ref[...]
    bufs[0, CF_HALO:CF_HALO + tl, :] = u[:, :D_MODEL] * _sigmoid(u[:, D_MODEL:])
    for s in range(1, 8):
        bufs[s, 0:tl + 24, :] = bufs[0, s:s + tl + 24, :]

    def chunk(c, carry):
        off = pl.multiple_of(c * CF_RC, CF_RC)
        acc = jnp.broadcast_to(b_ref[...], (CF_RC, D_MODEL))
        for k in range(CONV_K):
            a, s = divmod(CF_HALO - (CONV_K - 1) + k, 8)
            acc = acc + w_ref[k:k + 1, :] * bufs[s, pl.ds(off + 8 * a, CF_RC), :]
        cvb[pl.ds(off, CF_RC), :] = _layernorm_silu(acc, lg_ref[...], lb_ref[...]).astype(BF16)
        return carry

    lax.fori_loop(0, tl // CF_RC, chunk, 0)
    tail = bufs[0, tl:tl + CF_HALO, :]
    nc_ref[...] = tail
    bufs[0, 0:CF_HALO, :] = tail
    o_ref[...] = _dot(cvb[...], wo_ref[...])


def _conformer_prompt(u, dw_w, dw_b, ln_g, ln_b, w_out):
    l = u.shape[0]
    tl = CF_TL
    wpad = jnp.concatenate([dw_w, jnp.zeros((1, D_MODEL), F32)], axis=0)
    return pl.pallas_call(
        _conformer_kernel,
        grid=(l // tl,),
        in_specs=[pl.BlockSpec((tl, 2 * D_MODEL), lambda i: (i, UCF_OFF // (2 * D_MODEL))),
                  pl.BlockSpec((CONV_K + 1, D_MODEL), lambda i: (0, 0)),
                  pl.BlockSpec((1, D_MODEL), lambda i: (0, 0)),
                  pl.BlockSpec((1, D_MODEL), lambda i: (0, 0)),
                  pl.BlockSpec((1, D_MODEL), lambda i: (0, 0)),
                  pl.BlockSpec((D_MODEL, D_MODEL), lambda i: (0, 0))],
        out_specs=[pl.BlockSpec((tl, D_MODEL), lambda i: (i, 0)),
                   pl.BlockSpec((CF_HALO, D_MODEL), lambda i: (0, 0))],
        out_shape=[jax.ShapeDtypeStruct((l, D_MODEL), F32),
                   jax.ShapeDtypeStruct((CF_HALO, D_MODEL), F32)],
        scratch_shapes=[pltpu.VMEM((8, tl + CF_HALO, D_MODEL), F32),
                        pltpu.VMEM((tl, D_MODEL), BF16)],
        compiler_params=_cparams("arbitrary"),
        name="conformer_prompt",
    )(u, wpad, dw_b, ln_g, ln_b, w_out)


def _gated_group_norm(y, z, g):
    y = y * _silu(z)
    gw = D_INNER // SSM_GROUPS
    parts = []
    for k in range(SSM_GROUPS):
        yg = y[:, k * gw:(k + 1) * gw]
        parts.append(yg * lax.rsqrt(jnp.mean(yg * yg, axis=-1, keepdims=True) + EPS))
    return jnp.concatenate(parts, axis=1) * g


def _ssd_kernel(xbc_ref, z_ref, sm_ref, cw_ref, cb_ref, dtb_ref, alog_ref, dsk_ref, ng_ref, rexp_ref,
                wo_ref, o_ref, hout_ref, sc_ref, cbuf, hst):
    i = pl.program_id(0)
    q = SSM_CHUNK

    @pl.when(i == 0)
    def _():
        cbuf[0:8, :] = jnp.zeros((8, SSM_CONV_DIM), F32)
        hst[...] = jnp.zeros(hst.shape, F32)

    x = xbc_ref[...]
    cbuf[8:8 + q, :] = x
    conv = cb_ref[...] + cw_ref[3:4, :] * x
    for k in range(SSM_CONV_K - 1):
        conv = conv + cw_ref[k:k + 1, :] * cbuf[5 + k:5 + k + q, :]
    tail = cbuf[q:q + 8, :]
    sc_ref[...] = tail
    cbuf[0:8, :] = tail
    xa = _silu(conv)
    xs = xa[:, :D_INNER]
    bm = xa[:, D_INNER:D_INNER + SSM_GROUPS * D_STATE]
    cm = xa[:, D_INNER + SSM_GROUPS * D_STATE:]

    dt = _softplus(sm_ref[...] + dtb_ref[...])
    a = dt * (-jnp.exp(alog_ref[...]))
    row = lax.broadcasted_iota(jnp.int32, (q, q), 0)
    col = lax.broadcasted_iota(jnp.int32, (q, q), 1)
    tri = row >= col
    a_cum = _dot_exact_lhs01(tri.astype(BF16), a)
    a_cum_t = a_cum.T
    a_last = a_cum[q - 1:q, :]
    rexp = rexp_ref[...]
    dtx = _dot_exact_rhs01(dt, rexp)
    eax = _dot_exact_rhs01(jnp.exp(a_cum), rexp)
    decx = _dot_exact_rhs01(jnp.exp(a_last - a_cum), rexp)
    xin = xs * dtx
    xdec = (xin * decx).astype(BF16)
    xin_b = xin.astype(BF16)
    cd_col = jnp.broadcast_to(jnp.exp(a_last), (q, q)).T
    lane_lo = lax.broadcasted_iota(jnp.int32, (q, 2 * SSM_HEADDIM), 1) < SSM_HEADDIM

    hpg = SSM_HEADS // SSM_GROUPS
    gw = hpg * SSM_HEADDIM
    y_groups = []
    for g in range(SSM_GROUPS):
        cg = cm[:, g * D_STATE:(g + 1) * D_STATE].astype(BF16)
        bg = bm[:, g * D_STATE:(g + 1) * D_STATE].astype(BF16)
        cb = _dot_nt(cg, bg)
        hprev = hst[g * gw:(g + 1) * gw, :]
        y_off = _dot_nt(cg, hprev.astype(BF16)) * eax[:, g * gw:(g + 1) * gw]
        st = _dot_tn(xdec[:, g * gw:(g + 1) * gw], bg)
        pair_parts = []
        for pr in range(hpg // 2):
            xp = xin_b[:, g * gw + pr * 128:g * gw + (pr + 1) * 128]
            yp = None
            for sub in range(2):
                h = g * hpg + pr * 2 + sub
                seg = a_cum[:, h:h + 1] - a_cum_t[h:h + 1, :]
                lm = jnp.exp(jnp.where(tri, seg, -jnp.inf))
                mh = (cb * lm).astype(BF16)
                xm = jnp.where(lane_lo if sub == 0 else jnp.logical_not(lane_lo), xp, jnp.zeros_like(xp))
                d = _dot(mh, xm)
                yp = d if yp is None else yp + d
                r0 = h * SSM_HEADDIM
                lo = (pr * 2 + sub) * SSM_HEADDIM
                hst[r0:r0 + SSM_HEADDIM, :] = (hprev[lo:lo + SSM_HEADDIM, :] * cd_col[h:h + 1, :]
                                              + st[lo:lo + SSM_HEADDIM, :])
            pair_parts.append(yp)
        y_groups.append(jnp.concatenate(pair_parts, axis=1) + y_off)
    y = jnp.concatenate(y_groups, axis=1) + dsk_ref[...] * xs
    yn = _gated_group_norm(y, z_ref[...], ng_ref[...])
    o_ref[...] = _dot(yn.astype(BF16), wo_ref[...])

    @pl.when(i == pl.num_programs(0) - 1)
    def _():
        hout_ref[...] = hst[...]


def _pad_lanes(v, n=128):
    return jnp.concatenate([v, jnp.zeros((v.shape[0], n - v.shape[1]), v.dtype)], axis=1)


def _ssd_prompt(u, conv_w, conv_b, dt_bias, a_log, dskip_x, norm_g, rexp, w_out):
    l = u.shape[0]
    q = SSM_CHUNK
    cw = jnp.concatenate([conv_w, jnp.zeros((4, SSM_CONV_DIM), F32)], axis=0)
    const = lambda i: (0, 0)
    return pl.pallas_call(
        _ssd_kernel,
        grid=(l // q,),
        in_specs=[pl.BlockSpec((q, SSM_CONV_DIM), lambda i: (i, XBC_OFF // SSM_CONV_DIM)),
                  pl.BlockSpec((q, D_INNER), lambda i: (i, Z_OFF // D_INNER)),
                  pl.BlockSpec((q, 128), lambda i: (i, SMALL_OFF // 128)),
                  pl.BlockSpec((8, SSM_CONV_DIM), const),
                  pl.BlockSpec((1, SSM_CONV_DIM), const),
                  pl.BlockSpec((1, 128), const),
                  pl.BlockSpec((1, 128), const),
                  pl.BlockSpec((1, D_INNER), const),
                  pl.BlockSpec((1, D_INNER), const),
                  pl.BlockSpec((128, D_INNER), const),
                  pl.BlockSpec((D_INNER, D_MODEL), const)],
        out_specs=[pl.BlockSpec((q, D_MODEL), lambda i: (i, 0)),
                   pl.BlockSpec((D_INNER, D_STATE), const),
                   pl.BlockSpec((8, SSM_CONV_DIM), const)],
        out_shape=[jax.ShapeDtypeStruct((l, D_MODEL), F32),
                   jax.ShapeDtypeStruct((D_INNER, D_STATE), F32),
                   jax.ShapeDtypeStruct((8, SSM_CONV_DIM), F32)],
        scratch_shapes=[pltpu.VMEM((q + 8, SSM_CONV_DIM), F32),
                        pltpu.VMEM((D_INNER, D_STATE), F32)],
        compiler_params=_cparams("arbitrary"),
        name="ssd_prompt",
    )(u, u, u, cw, conv_b, _pad_lanes(dt_bias), _pad_lanes(a_log), dskip_x, norm_g, rexp, w_out)


def _rope(x, c, s1, s2):
    n = x.shape[1]
    return x * c + pltpu.roll(x, ROT_DIM // 2, 1) * s1 + pltpu.roll(x, n - ROT_DIM // 2, 1) * s2


def _aug_vt(v):
    vt = v.T.astype(BF16)
    ones = jnp.ones((V_ROWS - HEAD_DIM, v.shape[0]), BF16)
    return [jnp.concatenate([vt[g * HEAD_DIM:(g + 1) * HEAD_DIM, :], ones], axis=0) for g in range(KV_GROUPS)]


def _prep_kernel(q_ref, kv_ref, tab_ref, kc_ref, vc_ref, ks_ref, vs_ref, kw_ref, vw_ref,
                 qpt_ref, qrt_ref, ksb_ref, vst_ref, kwb_ref, vwt_ref, gt_ref):
    i = pl.program_id(0)
    tl = q_ref.shape[0]
    c = tab_ref[0]
    s1 = tab_ref[1]
    s2 = tab_ref[2]
    reps = D_MODEL // 128
    q = q_ref[...] * Q_SCALE
    qr = _rope(q, jnp.concatenate([c] * reps, axis=1), jnp.concatenate([s1] * reps, axis=1),
               jnp.concatenate([s2] * reps, axis=1))
    qpt_ref[...] = q.T.astype(BF16)
    qrt_ref[...] = qr.T.astype(BF16)
    kv = kv_ref[...]
    kc_ref[...] = kv[:, 0:KV_W]
    vc_ref[...] = kv[:, KV_W:2 * KV_W]
    ks = _rope(kv[:, 2 * KV_W:3 * KV_W], c, s1, s2)
    ks_ref[...] = ks
    row = lax.broadcasted_iota(jnp.int32, (tl, KV_W), 0)
    lane = lax.broadcasted_iota(jnp.int32, (tl, KV_W), 1)
    blk_in_tile = lax.shift_right_logical((i * tl) % TK + row, 6)
    ksb_ref[...] = jnp.concatenate([ks, jnp.where(lane == blk_in_tile, 1.0, 0.0)], axis=1).astype(BF16)
    vs = kv[:, 3 * KV_W:4 * KV_W]
    vs_ref[...] = vs
    kw = _rope(kv[:, 4 * KV_W:5 * KV_W], c, s1, s2)
    kw_ref[...] = kw
    kwb_ref[...] = kw.astype(BF16)
    vw = kv[:, 5 * KV_W:6 * KV_W]
    vw_ref[...] = vw
    for g, (a, b) in enumerate(zip(_aug_vt(vs), _aug_vt(vw))):
        vst_ref[0, g] = a
        vwt_ref[0, g] = b
    gt_ref[...] = _sigmoid(kv[:, 6 * KV_W:7 * KV_W]).T


def _prep_prompt(u, tab):
    l = u.shape[0]
    tl = TQ
    nt = l // tl
    row = lambda i: (i, 0)
    f32s = jax.ShapeDtypeStruct((l, KV_W), F32)
    vspec = pl.BlockSpec((1, KV_GROUPS, V_ROWS, tl), lambda i: (i, 0, 0, 0))
    vshape = jax.ShapeDtypeStruct((nt, KV_GROUPS, V_ROWS, tl), BF16)
    return pl.pallas_call(
        _prep_kernel,
        grid=(nt,),
        in_specs=[pl.BlockSpec((tl, D_MODEL), lambda i: (i, Q_OFF // D_MODEL)),
                  pl.BlockSpec((tl, D_MODEL), lambda i: (i, KV_OFF // D_MODEL)),
                  pl.BlockSpec((3, tl, 128), lambda i: (0, i, 0))],
        out_specs=[pl.BlockSpec((tl, KV_W), row)] * 6 + [
            pl.BlockSpec((D_MODEL, tl), lambda i: (0, i)),
            pl.BlockSpec((D_MODEL, tl), lambda i: (0, i)),
            pl.BlockSpec((tl, 2 * KV_W), row),
            vspec,
            pl.BlockSpec((tl, KV_W), row),
            vspec,
            pl.BlockSpec((128, tl), lambda i: (0, i))],
        out_shape=[f32s] * 6 + [
            jax.ShapeDtypeStruct((D_MODEL, l), BF16),
            jax.ShapeDtypeStruct((D_MODEL, l), BF16),
            jax.ShapeDtypeStruct((l, 2 * KV_W), BF16),
            vshape,
            jax.ShapeDtypeStruct((l, KV_W), BF16),
            vshape,
            jax.ShapeDtypeStruct((128, l), F32)],
        compiler_params=_cparams("arbitrary"),
        name="prep_prompt",
    )(u, u, tab)


def _prep_sample_kernel(q_ref, kv_ref, tab_ref, qp_ref, qr_ref, kvo_ref):
    c = tab_ref[0]
    s1 = tab_ref[1]
    s2 = tab_ref[2]
    reps = D_MODEL // 128
    q = q_ref[...] * Q_SCALE
    qp_ref[...] = q
    qr_ref[...] = _rope(q, jnp.concatenate([c] * reps, axis=1), jnp.concatenate([s1] * reps, axis=1),
                        jnp.concatenate([s2] * reps, axis=1))
    kv = kv_ref[...]
    one = jnp.ones_like(c)
    zero = jnp.zeros_like(c)
    ckv = jnp.concatenate([one, one, c, one, c, one, one, one], axis=1)
    s1kv = jnp.concatenate([zero, zero, s1, zero, s1, zero, zero, zero], axis=1)
    s2kv = jnp.concatenate([zero, zero, s2, zero, s2, zero, zero, zero], axis=1)
    kvr = _rope(kv, ckv, s1kv, s2kv)
    lane = lax.broadcasted_iota(jnp.int32, kv.shape, 1)
    kvo_ref[...] = jnp.where((lane >= 6 * KV_W) & (lane < 7 * KV_W), _sigmoid(kv), kvr)


def _prep_sample(u, tab):
    b = u.shape[0]
    full = lambda i: (0, 0)
    return pl.pallas_call(
        _prep_sample_kernel,
        grid=(1,),
        in_specs=[pl.BlockSpec((b, D_MODEL), lambda i: (0, Q_OFF // D_MODEL)),
                  pl.BlockSpec((b, D_MODEL), lambda i: (0, KV_OFF // D_MODEL)),
                  pl.BlockSpec((3, b, 128), lambda i: (0, 0, 0))],
        out_specs=[pl.BlockSpec((b, D_MODEL), full)] * 3,
        out_shape=[jax.ShapeDtypeStruct((b, D_MODEL), F32)] * 3,
        compiler_params=_cparams("arbitrary"),
        name="prep_sample",
    )(u, u, tab)


def _compress_kernel(x_ref, pe_ref, w1_ref, w2_ref, o_ref):
    x = (x_ref[...] + pe_ref[...]).astype(BF16)
    hid = _silu(_dot(x, w1_ref[...]))
    o_ref[...] = _dot(hid.astype(BF16), w2_ref[...])


def _compress(x2d, pe2, w1e, w2e):
    m = x2d.shape[0]
    tm = 256 if m % 256 == 0 else m
    kdim = BLOCK * KV_W
    return pl.pallas_call(
        _compress_kernel,
        grid=(m // tm,),
        in_specs=[pl.BlockSpec((tm, kdim), lambda i: (i, 0)),
                  pl.BlockSpec((1, kdim), lambda i: (0, 0)),
                  pl.BlockSpec((kdim, KV_GROUPS * CMP_HIDDEN), lambda i: (0, 0)),
                  pl.BlockSpec((KV_GROUPS * CMP_HIDDEN, KV_W), lambda i: (0, 0))],
        out_specs=pl.BlockSpec((tm, KV_W), lambda i: (i, 0)),
        out_shape=jax.ShapeDtypeStruct((m, KV_W), F32),
        compiler_params=_cparams("arbitrary"),
        name="compress",
    )(x2d, pe2, w1e, w2e)


def _compress_pool_kernel(x_ref, pe_ref, w1_ref, w2_ref, o_ref, xs, *, tp):
    stride = KV_GROUPS * HEAD_DIM
    for g in range(KV_GROUPS):
        for d in range(HEAD_DIM):
            rows = x_ref[pl.ds(g * HEAD_DIM + d, tp, stride=stride), :]
            xs[g * tp:(g + 1) * tp, d * PAGE_SIZE:(d + 1) * PAGE_SIZE] = (rows + pe_ref[d:d + 1, :]).astype(BF16)
    hid = _silu(_dot(xs[...], w1_ref[...]))
    o_ref[0] = _dot(hid.astype(BF16), w2_ref[...])


def _compress_pool(cache_t, layer, pe_t, w1t, w2e):
    n_pool = cache_t.shape[1] // (KV_GROUPS * HEAD_DIM)
    tp = 128 if n_pool % 128 == 0 else n_pool
    ns = n_pool // tp
    kdim = HEAD_DIM * PAGE_SIZE
    halves = PAGE_SIZE // BLOCK
    out = pl.pallas_call(
        functools.partial(_compress_pool_kernel, tp=tp),
        grid=(ns,),
        in_specs=[pl.BlockSpec((None, tp * KV_GROUPS * HEAD_DIM, PAGE_SIZE), lambda i: (layer, i, 0)),
                  pl.BlockSpec((HEAD_DIM, PAGE_SIZE), lambda i: (0, 0)),
                  pl.BlockSpec((kdim, halves * CMP_HIDDEN), lambda i: (0, 0)),
                  pl.BlockSpec((halves * CMP_HIDDEN, halves * HEAD_DIM), lambda i: (0, 0))],
        out_specs=pl.BlockSpec((1, KV_GROUPS * tp, halves * HEAD_DIM), lambda i: (i, 0, 0)),
        out_shape=jax.ShapeDtypeStruct((ns, KV_GROUPS * tp, halves * HEAD_DIM), F32),
        scratch_shapes=[pltpu.VMEM((KV_GROUPS * tp, kdim), BF16)],
        compiler_params=_cparams("arbitrary"),
        name="compress_pool",
    )(cache_t, pe_t, w1t, w2e)
    out = out.reshape(ns, KV_GROUPS, tp, halves, HEAD_DIM).transpose(0, 2, 3, 1, 4)
    return out.reshape(n_pool * halves, KV_W)


def _select_topn(score, index, axis, n_index):
    sel = jnp.zeros(score.shape, F32)
    work = score
    for _ in range(TOP_N):
        m = jnp.max(work, axis=axis, keepdims=True)
        first = jnp.min(jnp.where(work == m, index, n_index), axis=axis, keepdims=True)
        pick = index == first
        sel = jnp.where(pick, 1.0, sel)
        work = jnp.where(pick, -jnp.inf, work)
    return sel


def _attn_kernel(qpt_ref, qrt_ref, gt_ref, kcmp_ref, vcmp_ref, ks_ref, vst_ref,
                 kwa_ref, kwb_ref, kwc_ref, vwa_ref, vwb_ref, vwc_ref, o_ref,
                 rhs, bias16, m_sc, acc, outacc):
    i = pl.program_id(0)
    nb = kcmp_ref.shape[0]
    bpt = TK // BLOCK
    hw = HEADS_PER_GROUP * TQ
    t0 = i * TQ
    t_row = t0 + lax.broadcasted_iota(jnp.int32, (1, TQ), 1)
    zeros_half = jnp.zeros((HEAD_DIM, TQ), BF16)
    wide = lambda x: jnp.concatenate([x] * HEADS_PER_GROUP, axis=1)

    def group_queries(src_ref, g):
        cols = []
        for r in range(HEADS_PER_GROUP):
            h = g * HEADS_PER_GROUP + r
            qh = src_ref[h * HEAD_DIM:(h + 1) * HEAD_DIM, :]
            cols.append(jnp.concatenate([qh, zeros_half] if g == 0 else [zeros_half, qh], axis=0))
        return jnp.concatenate(cols, axis=1)

    def gate_row(branch, g):
        base = 32 + branch * N_HEADS + g * HEADS_PER_GROUP
        return jnp.concatenate([gt_ref[base + r:base + r + 1, :] for r in range(HEADS_PER_GROUP)], axis=1)

    @pl.when(i == 0)
    def _():
        rhs[...] = jnp.zeros(rhs.shape, BF16)

    kcmp = kcmp_ref[...].astype(BF16)
    vcmp_t = vcmp_ref[...].T.astype(BF16)
    jrow = lax.broadcasted_iota(jnp.int32, (nb, TQ), 0)
    mask_c1 = (jrow + 1) * BLOCK - 1 <= t_row
    mask_c = wide(mask_c1.astype(jnp.int32)) > 0
    cur = lax.shift_right_logical(t_row, 6)
    valid = jrow <= cur
    forced = valid & ((jrow == 0) | (jrow == cur) | (jrow == cur - 1))
    for g in range(KV_GROUPS):
        s = jnp.where(mask_c, _dot(kcmp, group_queries(qpt_ref, g)), NEG)
        m = jnp.max(s, axis=0, keepdims=True)
        e = jnp.where(mask_c, jnp.exp2(s - m), 0.0)
        p = e * (1.0 / jnp.maximum(jnp.sum(e, axis=0, keepdims=True), 1e-30))
        imp = p[:, 0:TQ]
        for r in range(1, HEADS_PER_GROUP):
            imp = imp + p[:, r * TQ:(r + 1) * TQ]
        o_c = _dot(vcmp_t, p.astype(BF16))[g * HEAD_DIM:(g + 1) * HEAD_DIM, :]
        outacc[g] = gate_row(0, g) * o_c
        rhs[g, 0:KV_W, :] = group_queries(qrt_ref, g)
        score = jnp.where(forced, BIG, jnp.where(valid, imp, -BIG))
        sel = _select_topn(score, jrow, 0, nb)
        bsel = jnp.where((sel > 0.5) & (score > -0.5 * BIG), 0.0, NEG)
        for c in range(nb // bpt):
            bias16[g, c] = jnp.concatenate([bsel[c * bpt:(c + 1) * bpt, :], jnp.zeros((16 - bpt, TQ), F32)],
                                           axis=0).astype(BF16)

    m_sc[...] = jnp.full(m_sc.shape, NEG, F32)
    acc[...] = jnp.zeros(acc.shape, F32)
    n_kt = (t0 + TQ + TK - 1) // TK

    def key_tile(kt, diagonal):
        k0 = pl.multiple_of(kt * TK, TK)
        k_aug = ks_ref[pl.ds(k0, TK), :]
        if diagonal:
            kpos = k0 + lax.broadcasted_iota(jnp.int32, (TK, TQ), 0)
            causal = wide(jnp.where(kpos > t_row, NEG, 0.0))
        for g in range(KV_GROUPS):
            rhs[g, KV_W:KV_W + 16, :] = wide(bias16[g, kt])
            s = _dot(k_aug, rhs[g])
            if diagonal:
                s = s + causal
            m_old8 = m_sc[g]
            m_new8 = jnp.maximum(m_old8, jnp.max(s, axis=0, keepdims=True))
            m_new = m_new8[0:1, :]
            alpha = jnp.exp2(m_old8[0:1, :] - m_new)
            p = jnp.exp2(s - m_new).astype(BF16)
            pv = None
            for c in range(TK // TQ):
                d = _dot(vst_ref[kt * (TK // TQ) + c, g], p[c * TQ:(c + 1) * TQ, :])
                pv = d if pv is None else pv + d
            acc[g] = alpha * acc[g] + pv
            m_sc[g] = m_new8

    def full_tile(kt, carry):
        key_tile(kt, False)
        return carry

    lax.fori_loop(0, n_kt - 1, full_tile, 0)
    key_tile(n_kt - 1, True)

    kw = jnp.concatenate([kwa_ref[...], kwb_ref[...], kwc_ref[...]], axis=0)
    wpos = (i - 2) * TQ + lax.broadcasted_iota(jnp.int32, (3 * TQ, TQ), 0)
    bias_w = wide(jnp.where((wpos >= 0) & (wpos <= t_row) & (t_row - wpos < WINDOW), 0.0, NEG))
    for g in range(KV_GROUPS):
        s = _dot(kw, rhs[g, 0:KV_W, :]) + bias_w
        m = jnp.max(s, axis=0, keepdims=True)
        p = jnp.exp2(s - m).astype(BF16)
        pv = (_dot(vwa_ref[0, g], p[0:TQ, :]) + _dot(vwb_ref[0, g], p[TQ:2 * TQ, :])
              + _dot(vwc_ref[0, g], p[2 * TQ:, :]))
        o_w = pv[0:HEAD_DIM, :] * (1.0 / pv[HEAD_DIM:HEAD_DIM + 1, :])
        a = acc[g]
        o_s = a[0:HEAD_DIM, :] * (1.0 / a[HEAD_DIM:HEAD_DIM + 1, :])
        o_g = outacc[g] + gate_row(1, g) * o_s + gate_row(2, g) * o_w
        for r in range(0, HEADS_PER_GROUP, 2):
            h = g * HEADS_PER_GROUP + r
            pair = jnp.concatenate([o_g[:, r * TQ:(r + 1) * TQ], o_g[:, (r + 1) * TQ:(r + 2) * TQ]], axis=0)
            o_ref[:, h * HEAD_DIM:(h + 2) * HEAD_DIM] = pair.T


def _attention_prompt(qpt, qrt, gt, kcmp, vcmp, ksb, vst, kwb, vwt):
    l = qpt.shape[1]
    nb = kcmp.shape[0]
    nt = l // TQ
    const2 = lambda i: (0, 0)
    blk = lambda d: (lambda i: (jnp.maximum(i - d, 0), 0))
    blk4 = lambda d: (lambda i: (jnp.maximum(i - d, 0), 0, 0, 0))
    vblk = lambda d: pl.BlockSpec((1, KV_GROUPS, V_ROWS, TQ), blk4(d))
    return pl.pallas_call(
        _attn_kernel,
        grid=(nt,),
        in_specs=[pl.BlockSpec((D_MODEL, TQ), lambda i: (0, i)),
                  pl.BlockSpec((D_MODEL, TQ), lambda i: (0, i)),
                  pl.BlockSpec((128, TQ), lambda i: (0, i)),
                  pl.BlockSpec((nb, KV_W), const2),
                  pl.BlockSpec((nb, KV_W), const2),
                  pl.BlockSpec((l, 2 * KV_W), const2),
                  pl.BlockSpec((nt, KV_GROUPS, V_ROWS, TQ), lambda i: (0, 0, 0, 0)),
                  pl.BlockSpec((TQ, KV_W), blk(2)),
                  pl.BlockSpec((TQ, KV_W), blk(1)),
                  pl.BlockSpec((TQ, KV_W), blk(0)),
                  vblk(2), vblk(1), vblk(0)],
        out_specs=pl.BlockSpec((TQ, D_MODEL), lambda i: (i, 0)),
        out_shape=jax.ShapeDtypeStruct((l, D_MODEL), F32),
        scratch_shapes=[pltpu.VMEM((KV_GROUPS, 2 * KV_W, HEADS_PER_GROUP * TQ), BF16),
                        pltpu.VMEM((KV_GROUPS, nb // (TK // BLOCK), 16, TQ), BF16),
                        pltpu.VMEM((KV_GROUPS, 8, HEADS_PER_GROUP * TQ), F32),
                        pltpu.VMEM((KV_GROUPS, V_ROWS, HEADS_PER_GROUP * TQ), F32),
                        pltpu.VMEM((KV_GROUPS, HEAD_DIM, HEADS_PER_GROUP * TQ), F32)],
        compiler_params=_cparams("arbitrary"),
        name="attention_prompt",
    )(qpt, qrt, gt, kcmp, vcmp, ksb, vst, kwb, kwb, kwb, vwt, vwt, vwt)


def _merge_kernel(x_ref, cf_ref, ssm_ref, att_ref, gm_ref, g1_ref, wa_ref, wo_ref, o_ref):
    gm = _sigmoid(gm_ref[...])
    br_att = _dot(att_ref[...].astype(BF16), wa_ref[...])
    mixed = (gm[:, :D_MODEL] * cf_ref[...] + gm[:, D_MODEL:2 * D_MODEL] * ssm_ref[...]
             + gm[:, 2 * D_MODEL:] * br_att)
    o_ref[...] = x_ref[...] + g1_ref[...] * _dot(mixed.astype(BF16), wo_ref[...])


def _merge(x, br_cf, br_ssm, att, u, g1, w_attn, w_o, tm):
    r = x.shape[0]
    mrows = tm if g1.shape[0] == r else 1
    mmap = (lambda i: (i, 0)) if g1.shape[0] == r else (lambda i: (0, 0))
    row = lambda i: (i, 0)
    const = lambda i: (0, 0)
    return pl.pallas_call(
        _merge_kernel,
        grid=(r // tm,),
        in_specs=[pl.BlockSpec((tm, D_MODEL), row)] * 4 + [
            pl.BlockSpec((tm, 3 * D_MODEL), lambda i: (i, GMIX_OFF // (3 * D_MODEL))),
            pl.BlockSpec((mrows, D_MODEL), mmap),
            pl.BlockSpec((D_MODEL, D_MODEL), const),
            pl.BlockSpec((D_MODEL, D_MODEL), const)],
        out_specs=pl.BlockSpec((tm, D_MODEL), row),
        out_shape=jax.ShapeDtypeStruct((r, D_MODEL), F32),
        compiler_params=_cparams("arbitrary"),
        name="merge",
    )(x, br_cf, br_ssm, att, u, g1, w_attn, w_o)


FF_T = D_FF // 2


def _ffn_kernel(x_ref, g_ref, sc_ref, sh_ref, g2_ref, wg_ref, wu_ref, wo_ref, o_ref, h_ref, acc_ref):
    j = pl.program_id(1)

    @pl.when(j == 0)
    def _():
        x = x_ref[...]
        y = x * lax.rsqrt(jnp.mean(x * x, axis=-1, keepdims=True) + EPS) * g_ref[...]
        h_ref[...] = (y * (1.0 + sc_ref[...]) + sh_ref[...]).astype(BF16)
        acc_ref[...] = jnp.zeros(acc_ref.shape, F32)

    h = h_ref[...]
    act = _silu(_dot(h, wg_ref[...])) * _dot(h, wu_ref[...])
    acc_ref[...] += _dot(act.astype(BF16), wo_ref[...])

    @pl.when(j == pl.num_programs(1) - 1)
    def _():
        o_ref[...] = x_ref[...] + g2_ref[...] * acc_ref[...]


def _ffn(x, g, sc, sh, g2, w_in, w_out, tm):
    r = x.shape[0]
    nj = D_FF // FF_T
    mrows = tm if sc.shape[0] == r else 1
    mmap = (lambda i, j: (i, 0)) if sc.shape[0] == r else (lambda i, j: (0, 0))
    row = lambda i, j: (i, 0)
    return pl.pallas_call(
        _ffn_kernel,
        grid=(r // tm, nj),
        in_specs=[pl.BlockSpec((tm, D_MODEL), row),
                  pl.BlockSpec((1, D_MODEL), lambda i, j: (0, 0)),
                  pl.BlockSpec((mrows, D_MODEL), mmap),
                  pl.BlockSpec((mrows, D_MODEL), mmap),
                  pl.BlockSpec((mrows, D_MODEL), mmap),
                  pl.BlockSpec((D_MODEL, FF_T), lambda i, j: (0, j)),
                  pl.BlockSpec((D_MODEL, FF_T), lambda i, j: (0, nj + j)),
                  pl.BlockSpec((FF_T, D_MODEL), lambda i, j: (j, 0))],
        out_specs=pl.BlockSpec((tm, D_MODEL), row),
        out_shape=jax.ShapeDtypeStruct((r, D_MODEL), F32),
        scratch_shapes=[pltpu.VMEM((tm, D_MODEL), BF16), pltpu.VMEM((tm, D_MODEL), F32)],
        compiler_params=_cparams("arbitrary", "arbitrary"),
        name="ffn",
    )(x, g, sc, sh, g2, w_in, w_in, w_out)


def _final_norm_kernel(x_ref, g_ref, o_ref):
    x = x_ref[...]
    o_ref[...] = x * lax.rsqrt(jnp.mean(x * x, axis=-1, keepdims=True) + EPS) * g_ref[...]


def _final_norm(x, g, tm):
    r = x.shape[0]
    return pl.pallas_call(
        _final_norm_kernel,
        grid=(r // tm,),
        in_specs=[pl.BlockSpec((tm, D_MODEL), lambda i: (i, 0)), pl.BlockSpec((1, D_MODEL), lambda i: (0, 0))],
        out_specs=pl.BlockSpec((tm, D_MODEL), lambda i: (i, 0)),
        out_shape=jax.ShapeDtypeStruct((r, D_MODEL), F32),
        compiler_params=_cparams("arbitrary"),
        name="final_norm",
    )(x, g)


def _sample_mix_kernel(ucf_ref, z_ref, xbc_ref, sm_ref, cst_ref, sst_ref, h0_ref,
                       dww_ref, dwb_ref, lg_ref, lb_ref, cw_ref, cb_ref, dtb_ref, alog_ref,
                       dsk_ref, ng_ref, rexp_ref,
                       cv_ref, yn_ref, glu_ref, hnew_ref):
    u = ucf_ref[...]
    glu = u[:, :D_MODEL] * _sigmoid(u[:, D_MODEL:])
    glu_ref[...] = glu
    conv = (dwb_ref[...] + dww_ref[CONV_K - 1:CONV_K, :] * glu
            + jnp.sum(dww_ref[0:CONV_K - 1, :] * cst_ref[...], axis=0, keepdims=True))
    cv_ref[...] = _layernorm_silu(conv, lg_ref[...], lb_ref[...])

    xbc = xbc_ref[...]
    conv = (cb_ref[...] + cw_ref[SSM_CONV_K - 1:SSM_CONV_K, :] * xbc
            + jnp.sum(cw_ref[0:SSM_CONV_K - 1, :] * sst_ref[...], axis=0, keepdims=True))
    xa = _silu(conv)
    xs = xa[:, :D_INNER]
    bm = xa[:, D_INNER:D_INNER + SSM_GROUPS * D_STATE]
    cm = xa[:, D_INNER + SSM_GROUPS * D_STATE:]
    dt = _softplus(sm_ref[...] + dtb_ref[...])
    dec = jnp.exp(dt * (-jnp.exp(alog_ref[...])))
    dtx = _dot_exact_rhs01(jnp.broadcast_to(dt, (8, 128)), rexp_ref[...])[0:1, :]
    xin = xs * dtx
    dec_col = jnp.broadcast_to(dec, (128, 128)).T
    nblk = D_INNER // 128
    xrows = jnp.concatenate([xin[:, a * 128:(a + 1) * 128] for a in range(nblk)]
                            + [jnp.zeros((128 - nblk, 128), F32)], axis=0)
    x_col = xrows.T
    blocks_per_group = (D_INNER // SSM_GROUPS) // 128
    for a in range(nblk):
        g = a // blocks_per_group
        bg = bm[:, g * D_STATE:(g + 1) * D_STATE]
        dcol = jnp.concatenate([jnp.broadcast_to(dec_col[2 * a:2 * a + 1, :], (SSM_HEADDIM, 128)),
                                jnp.broadcast_to(dec_col[2 * a + 1:2 * a + 2, :], (SSM_HEADDIM, 128))], axis=0)
        hnew_ref[a * 128:(a + 1) * 128, :] = h0_ref[a * 128:(a + 1) * 128, :] * dcol + x_col[:, a:a + 1] * bg
    c8 = jnp.concatenate([cm[:, g * D_STATE:(g + 1) * D_STATE] for g in range(SSM_GROUPS)]
                         + [jnp.zeros((8 - SSM_GROUPS, D_STATE), F32)], axis=0)
    yall = _dot_nt(c8.astype(BF16), hnew_ref[...].astype(BF16))
    lane = lax.broadcasted_iota(jnp.int32, (1, D_INNER), 1)
    gw = D_INNER // SSM_GROUPS
    y = jnp.zeros((1, D_INNER), F32)
    for g in range(SSM_GROUPS):
        y = y + jnp.where((lane >= g * gw) & (lane < (g + 1) * gw), yall[g:g + 1, :], 0.0)
    y = y + dsk_ref[...] * xs
    yn_ref[...] = _gated_group_norm(y, z_ref[...], ng_ref[...])


def _sample_mix(u3, conv_st, sconv_st, h0, dw_w, dw_b, ln_g, ln_b, conv_w, conv_b, dt_bias, a_log,
                dskip_x, norm_g, rexp):
    b = u3.shape[0]
    const = lambda i: (0, 0)
    ublk = lambda w, off: pl.BlockSpec((None, 1, w), lambda i: (i, 0, off // w))
    out1 = lambda w: pl.BlockSpec((None, 1, w), lambda i: (i, 0, 0))
    return pl.pallas_call(
        _sample_mix_kernel,
        grid=(b,),
        in_specs=[ublk(2 * D_MODEL, UCF_OFF), ublk(D_INNER, Z_OFF), ublk(SSM_CONV_DIM, XBC_OFF),
                  ublk(128, SMALL_OFF),
                  pl.BlockSpec((None, CONV_K - 1, D_MODEL), lambda i: (i, 0, 0)),
                  pl.BlockSpec((None, SSM_CONV_K - 1, SSM_CONV_DIM), lambda i: (i, 0, 0)),
                  pl.BlockSpec((None, D_INNER, D_STATE), lambda i: (i, 0, 0)),
                  pl.BlockSpec((CONV_K, D_MODEL), const),
                  pl.BlockSpec((1, D_MODEL), const),
                  pl.BlockSpec((1, D_MODEL), const),
                  pl.BlockSpec((1, D_MODEL), const),
                  pl.BlockSpec((SSM_CONV_K, SSM_CONV_DIM), const),
                  pl.BlockSpec((1, SSM_CONV_DIM), const),
                  pl.BlockSpec((1, 128), const),
                  pl.BlockSpec((1, 128), const),
                  pl.BlockSpec((1, D_INNER), const),
                  pl.BlockSpec((1, D_INNER), const),
                  pl.BlockSpec((128, D_INNER), const)],
        out_specs=[out1(D_MODEL), out1(D_INNER), out1(D_MODEL),
                   pl.BlockSpec((None, D_INNER, D_STATE), lambda i: (i, 0, 0))],
        out_shape=[jax.ShapeDtypeStruct((b, 1, D_MODEL), F32),
                   jax.ShapeDtypeStruct((b, 1, D_INNER), F32),
                   jax.ShapeDtypeStruct((b, 1, D_MODEL), F32),
                   jax.ShapeDtypeStruct((b, D_INNER, D_STATE), F32)],
        compiler_params=_cparams("arbitrary"),
        name="sample_mix",
    )(u3, u3, u3, u3, conv_st, sconv_st, h0, dw_w, dw_b, ln_g, ln_b, conv_w, conv_b,
      _pad_lanes(dt_bias), _pad_lanes(a_log), dskip_x, norm_g, rexp)


SEL_LANES = 128


def _sample_cmp_kernel(pt_ref, qp_ref, kpool_ref, vpool_ref, oc_ref, idx_ref, kbuf, vbuf, *, n_pages, t_pos):
    b = pl.program_id(0)
    nb = 2 * n_pages
    for pg in range(n_pages):
        src = pt_ref[b, pg] * 2
        kbuf[2 * pg:2 * pg + 2, :] = kpool_ref[pl.ds(src, 2), :]
        vbuf[2 * pg:2 * pg + 2, :] = vpool_ref[pl.ds(src, 2), :]
    qp = qp_ref[...]
    s = _dot_nt(qp, kbuf[...].astype(BF16))
    jl = lax.broadcasted_iota(jnp.int32, (N_HEADS, nb), 1)
    mask_c = (jl + 1) * BLOCK - 1 <= t_pos
    s = jnp.where(mask_c, s, NEG)
    m = jnp.max(s, axis=1, keepdims=True)
    e = jnp.where(mask_c, jnp.exp2(s - m), 0.0)
    p = e * (1.0 / jnp.maximum(jnp.sum(e, axis=1, keepdims=True), 1e-30))
    oc_ref[...] = _dot(p.astype(BF16), vbuf[...].astype(BF16))

    nbs = t_pos // BLOCK + 1
    width = ((nbs + 127) // 128) * 128
    imp = jnp.concatenate(
        [jnp.sum(p[g * HEADS_PER_GROUP:(g + 1) * HEADS_PER_GROUP, :], axis=0, keepdims=True)
         for g in range(KV_GROUPS)] + [jnp.zeros((8 - KV_GROUPS, nb), F32)], axis=0)
    imp = jnp.concatenate([imp, jnp.zeros((8, width - nb), F32)], axis=1)
    jw = lax.broadcasted_iota(jnp.int32, (8, width), 1)
    cur = t_pos // BLOCK
    valid = jw <= cur
    forced = valid & ((jw == 0) | (jw == cur) | (jw == cur - 1))
    score = jnp.where(forced, BIG, jnp.where(valid, imp, -BIG))
    work = jnp.where(jw < nbs, score, -jnp.inf)
    lane = lax.broadcasted_iota(jnp.int32, (8, SEL_LANES), 1)
    out = jnp.full((8, SEL_LANES), -1, jnp.int32)
    for it in range(TOP_N):
        mx = jnp.max(work, axis=1, keepdims=True)
        first = jnp.min(jnp.where(work == mx, jw, width), axis=1, keepdims=True)
        out = jnp.where(lane == it, jnp.where(mx > -0.5 * BIG, first, -1), out)
        work = jnp.where(jw == first, -jnp.inf, work)
    idx_ref[...] = out


def _sample_cmp(page_table, qp_pad, kpool, vpool, t_pos):
    b, n_pages = page_table.shape
    npool = kpool.shape[0]
    nb = 2 * n_pages
    kern = functools.partial(_sample_cmp_kernel, n_pages=n_pages, t_pos=t_pos)
    return pl.pallas_call(
        kern,
        grid_spec=pltpu.PrefetchScalarGridSpec(
            num_scalar_prefetch=1,
            grid=(b,),
            in_specs=[pl.BlockSpec((None, N_HEADS, KV_W), lambda i, pt: (i, 0, 0)),
                      pl.BlockSpec((npool, KV_W), lambda i, pt: (0, 0)),
                      pl.BlockSpec((npool, KV_W), lambda i, pt: (0, 0))],
            out_specs=[pl.BlockSpec((None, N_HEADS, KV_W), lambda i, pt: (i, 0, 0)),
                       pl.BlockSpec((None, 8, SEL_LANES), lambda i, pt: (i, 0, 0))],
            scratch_shapes=[pltpu.VMEM((nb, KV_W), F32), pltpu.VMEM((nb, KV_W), F32)]),
        out_shape=[jax.ShapeDtypeStruct((b, N_HEADS, KV_W), F32),
                   jax.ShapeDtypeStruct((b, 8, SEL_LANES), jnp.int32)],
        compiler_params=_cparams("arbitrary"),
        name="sample_cmp",
    )(page_table, qp_pad, kpool, vpool)


def _sample_attn_kernel(idx_ref, pt_ref, q16_ref, qr_ref, new_ref, wk_ref, wv_ref, oc_ref, gate_ref,
                        kcache_ref, vcache_ref, o_ref, kbuf, vbuf, sem, *, n_pages, t_pos, layer):
    b = pl.program_id(0)
    nb_past = 2 * n_pages

    def page_copy(cache_ref, buf, g, k, which):
        j = idx_ref[b, g, k]
        jj = jnp.where((j >= 0) & (j < nb_past), j, 0)
        page = pt_ref[b, jj // 2]
        return pltpu.make_async_copy(cache_ref.at[layer, page, g],
                                     buf.at[g, :, pl.ds(k * PAGE_SIZE, PAGE_SIZE)], sem.at[which, g, k])

    for g in range(KV_GROUPS):
        for k in range(TOP_N):
            page_copy(kcache_ref, kbuf, g, k, 0).start()
            page_copy(vcache_ref, vbuf, g, k, 1).start()

    qr = qr_ref[...]
    qf = qr.astype(F32)
    new = new_ref[...]
    grp_lo = lax.broadcasted_iota(jnp.int32, (N_HEADS, KV_W), 0) < HEADS_PER_GROUP
    lane_lo = lax.broadcasted_iota(jnp.int32, (N_HEADS, KV_W), 1) < HEAD_DIM
    own = grp_lo == lane_lo

    def own_half(x):
        x = jnp.where(own, x, 0.0)
        return x[:, :HEAD_DIM] + x[:, HEAD_DIM:]

    win = wk_ref.shape[0]
    s = _dot_nt(qr, wk_ref[...].astype(BF16))
    wpos = t_pos - win + lax.broadcasted_iota(jnp.int32, (N_HEADS, win), 1)
    mask_w = t_pos - wpos < WINDOW
    s = jnp.where(mask_w, s, NEG)
    s_new = jnp.sum(qf * new[2:3, :], axis=1, keepdims=True)
    m = jnp.maximum(jnp.max(s, axis=1, keepdims=True), s_new)
    p = jnp.where(mask_w, jnp.exp2(s - m), 0.0)
    p_new = jnp.exp2(s_new - m)
    lsum = jnp.sum(p, axis=1, keepdims=True) + p_new
    o_w = own_half((_dot(p.astype(BF16), wv_ref[...].astype(BF16)) + p_new * new[3:4, :]) * (1.0 / lsum))

    for g in range(KV_GROUPS):
        for k in range(TOP_N):
            page_copy(kcache_ref, kbuf, g, k, 0).wait()
            page_copy(vcache_ref, vbuf, g, k, 1).wait()
    nsel = TOP_N * PAGE_SIZE
    lane = lax.broadcasted_iota(jnp.int32, (1, nsel), 1)
    lane_slot = lane // PAGE_SIZE
    lane_half = (lane // BLOCK) % 2
    o_s_parts = []
    for g in range(KV_GROUPS):
        ok = jnp.zeros((1, nsel), jnp.int32)
        has_new = jnp.zeros((1, 1), jnp.int32)
        for k in range(TOP_N):
            j = idx_ref[b, g, k]
            in_cache = (j >= 0) & (j < nb_past)
            hit = jnp.where(in_cache, (lane_half == j % 2).astype(jnp.int32), 0)
            ok = jnp.where(lane_slot == k, hit, ok)
            has_new = jnp.maximum(has_new, (j == nb_past).astype(jnp.int32))
        mask_s = ok > 0
        mask_n = has_new > 0
        qg = q16_ref[g * HEADS_PER_GROUP:(g + 1) * HEADS_PER_GROUP, :]
        k_new = new[0:1, g * HEAD_DIM:(g + 1) * HEAD_DIM]
        v_new = new[1:2, g * HEAD_DIM:(g + 1) * HEAD_DIM]
        s = jnp.where(mask_s, _dot(qg, kbuf[g].astype(BF16)), NEG)
        s_new = jnp.where(mask_n, jnp.sum(qg.astype(F32) * k_new, axis=1, keepdims=True), NEG)
        m = jnp.maximum(jnp.max(s, axis=1, keepdims=True), s_new)
        p = jnp.where(mask_s, jnp.exp2(s - m), 0.0)
        p_new = jnp.where(mask_n, jnp.exp2(s_new - m), 0.0)
        lsum = jnp.maximum(jnp.sum(p, axis=1, keepdims=True) + p_new, 1e-30)
        o_s_parts.append((_dot_nt(p.astype(BF16), vbuf[g].astype(BF16)) + p_new * v_new) * (1.0 / lsum))
    o_s = jnp.concatenate(o_s_parts, axis=0)

    gate = gate_ref[...]
    o_ref[...] = gate[:, 0:1] * own_half(oc_ref[...]) + gate[:, 1:2] * o_s + gate[:, 2:3] * o_w


def _sample_attn(idx, page_table, q16, qr_pad, new_rows, win_k, win_v, o_c, gates, kcache, vcache, t_pos, layer):
    b, n_pages = page_table.shape
    win = win_k.shape[1]
    kern = functools.partial(_sample_attn_kernel, n_pages=n_pages, t_pos=t_pos, layer=layer)
    per_b = lambda r, c: pl.BlockSpec((None, r, c), lambda i, ix, pt: (i, 0, 0))
    return pl.pallas_call(
        kern,
        grid_spec=pltpu.PrefetchScalarGridSpec(
            num_scalar_prefetch=2,
            grid=(b,),
            in_specs=[per_b(N_HEADS, HEAD_DIM), per_b(N_HEADS, KV_W), per_b(8, KV_W), per_b(win, KV_W),
                      per_b(win, KV_W), per_b(N_HEADS, KV_W), per_b(N_HEADS, KV_W),
                      pl.BlockSpec(memory_space=pl.ANY), pl.BlockSpec(memory_space=pl.ANY)],
            out_specs=per_b(N_HEADS, HEAD_DIM),
            scratch_shapes=[pltpu.VMEM((KV_GROUPS, HEAD_DIM, TOP_N * PAGE_SIZE), F32),
                            pltpu.VMEM((KV_GROUPS, HEAD_DIM, TOP_N * PAGE_SIZE), F32),
                            pltpu.SemaphoreType.DMA((2, KV_GROUPS, TOP_N))]),
        out_shape=jax.ShapeDtypeStruct((b, N_HEADS, HEAD_DIM), F32),
        compiler_params=_cparams("arbitrary"),
        name="sample_attn",
    )(idx, page_table, q16, qr_pad, new_rows, win_k, win_v, o_c, gates, kcache, vcache)


def _pack_w_in(w):
    segs = [w[:, 4096:7168], w[:, 9040:12112], w[:, 0:2048], w[:, 2048:4096], w[:, 7200:8224],
            w[:, 8224:8992], w[:, 7168:7200], w[:, 8992:9040],
            jnp.zeros((D_MODEL, N_PACK - 12112), w.dtype)]
    return jnp.concatenate(segs, axis=1).astype(BF16)


def _expand_cmp_weights(pe, w1, w2):
    eye = jnp.eye(KV_GROUPS, dtype=F32)
    w1r = w1.reshape(BLOCK, HEAD_DIM, CMP_HIDDEN)
    w1e = jnp.einsum('ldh,gk->lgdkh', w1r, eye).reshape(BLOCK * KV_W, KV_GROUPS * CMP_HIDDEN).astype(BF16)
    w2e = jnp.einsum('hd,gk->ghkd', w2, eye).reshape(KV_GROUPS * CMP_HIDDEN, KV_W).astype(BF16)
    pe2 = jnp.broadcast_to(pe[:, None, :], (BLOCK, KV_GROUPS, HEAD_DIM)).reshape(1, BLOCK * KV_W)
    halves = PAGE_SIZE // BLOCK
    w1t = jnp.einsum('ldh,ab->dalbh', w1r, jnp.eye(halves, dtype=F32))
    w1t = w1t.reshape(HEAD_DIM * PAGE_SIZE, halves * CMP_HIDDEN).astype(BF16)
    pe_t = jnp.tile(pe.T, (1, halves))
    return (pe2, w1e, w2e), (pe_t, w1t, w2e)


def _rope_tables(pos):
    half = ROT_DIM // 2
    inv = ROPE_THETA ** (-(jnp.arange(half, dtype=F32) * 2.0 / ROT_DIM))
    ang = pos.astype(F32)[:, None] * inv[None, :]
    cos = jnp.cos(ang)
    sin = jnp.sin(ang)
    n = pos.shape[0]
    one = jnp.ones((n, HEAD_DIM - ROT_DIM), F32)
    zero8 = jnp.zeros((n, half), F32)
    zero = jnp.zeros((n, HEAD_DIM - ROT_DIM), F32)
    c = jnp.concatenate([cos, cos, one], axis=1)
    s1 = jnp.concatenate([zero8, sin, zero], axis=1)
    s2 = jnp.concatenate([-sin, zero8, zero], axis=1)
    return jnp.stack([jnp.concatenate([t, t], axis=1) for t in (c, s1, s2)])


def _head_expand_matrix():
    h = np.arange(128)[:, None]
    lane = np.arange(D_INNER)[None, :]
    return jnp.asarray((lane // SSM_HEADDIM == h).astype(np.float32), dtype=BF16)


def _group_padded(q):
    b = q.shape[0]
    qh = q.reshape(b, KV_GROUPS, HEADS_PER_GROUP, HEAD_DIM)
    z = jnp.zeros_like(qh[:, 0])
    lo = jnp.concatenate([qh[:, 0], z], axis=-1)
    hi = jnp.concatenate([z, qh[:, 1]], axis=-1)
    return jnp.concatenate([lo, hi], axis=1).astype(BF16)


def _layer_weights(l, p):
    w = {k: v[l] for k, v in p.items()}
    w['w_in_p'] = _pack_w_in(w['w_in'])
    for nm in ('w_cf_out', 'w_ssm_out', 'w_attn_out', 'w_o', 'w_ffn_in', 'w_ffn_out'):
        w[nm] = w[nm].astype(BF16)
    w['cmp_k'] = _expand_cmp_weights(w['cmp_pe_k'], w['cmp_w1_k'], w['cmp_w2_k'])
    w['cmp_v'] = _expand_cmp_weights(w['cmp_pe_v'], w['cmp_w1_v'], w['cmp_w2_v'])
    w['dskip_x'] = jnp.repeat(w['d_skip'], SSM_HEADDIM)[None, :]
    for nm in ('norm1_g', 'norm2_g', 'cf_dw_b', 'cf_ln_g', 'cf_ln_b', 'ssm_conv_b', 'dt_bias', 'a_log',
               'ssm_norm_g'):
        w[nm] = w[nm][None, :]
    return w


def _prompt_layer(x, mod, w, tab, rexp):
    l = x.shape[0]
    sh1, sc1, g1, sh2, sc2, g2 = [mod[:, k * D_MODEL:(k + 1) * D_MODEL] for k in range(6)]
    u = _in_proj(x, w['norm1_g'], sc1, sh1, w['w_in_p'], 1024)
    br_cf, conv_tail = _conformer_prompt(u, w['cf_dw_w'], w['cf_dw_b'], w['cf_ln_g'], w['cf_ln_b'], w['w_cf_out'])
    br_ssm, h_new, sconv_tail = _ssd_prompt(u, w['ssm_conv_w'], w['ssm_conv_b'], w['dt_bias'], w['a_log'],
                                            w['dskip_x'], w['ssm_norm_g'], rexp, w['w_ssm_out'])
    kc, vc, ks, vs, kw, vw, qpt, qrt, ksb, vst, kwb, vwt, gt = _prep_prompt(u, tab)
    kcmp = _compress(kc.reshape(l // BLOCK, BLOCK * KV_W), *w['cmp_k'][0])
    vcmp = _compress(vc.reshape(l // BLOCK, BLOCK * KV_W), *w['cmp_v'][0])
    att = _attention_prompt(qpt, qrt, gt, kcmp, vcmp, ksb, vst, kwb, vwt)
    x = _merge(x, br_cf, br_ssm, att, u, g1, w['w_attn_out'], w['w_o'], 512)
    x = _ffn(x, w['norm2_g'], sc2, sh2, g2, w['w_ffn_in'], w['w_ffn_out'], 512)
    keep = min(WINDOW, l)
    kv4 = lambda a: a.reshape(1, -1, KV_GROUPS, HEAD_DIM)
    state = (kv4(kc), kv4(vc), kv4(ks), kv4(vs), kv4(kw[l - keep:]), kv4(vw[l - keep:]),
             conv_tail[CF_HALO - (CONV_K - 1):][None],
             sconv_tail[8 - (SSM_CONV_K - 1):][None],
             h_new.reshape(1, SSM_HEADS, SSM_HEADDIM, D_STATE))
    return x, state


def _sample_layer(x, mod, w, tab, rexp, caches, layer, states, page_table, t_pos):
    b = x.shape[0]
    cache_ck, cache_cv, cache_sk, cache_sv = caches
    win_k, win_v, conv_st, sconv_st, ssm_st = states
    depth, n_pool = cache_ck.shape[0], cache_ck.shape[1]
    rows2d = lambda c: c.reshape(depth, n_pool * KV_GROUPS * HEAD_DIM, PAGE_SIZE)
    sh1, sc1, g1, sh2, sc2, g2 = [mod[:, k * D_MODEL:(k + 1) * D_MODEL] for k in range(6)]
    u = _in_proj(x, w['norm1_g'], sc1, sh1, w['w_in_p'], b)
    cv, yn, glu, h_new = _sample_mix(u.reshape(b, 1, N_PACK), conv_st, sconv_st,
                                     ssm_st.reshape(b, D_INNER, D_STATE),
                                     w['cf_dw_w'], w['cf_dw_b'], w['cf_ln_g'], w['cf_ln_b'],
                                     w['ssm_conv_w'], w['ssm_conv_b'], w['dt_bias'], w['a_log'],
                                     w['dskip_x'], w['ssm_norm_g'], rexp)
    br_cf = _matmul(cv.reshape(b, D_MODEL), w['w_cf_out'])
    br_ssm = _matmul(yn.reshape(b, D_INNER), w['w_ssm_out'])
    qp, qr, kvo = _prep_sample(u, tab)
    seg = lambda k: kvo[:, k * KV_W:(k + 1) * KV_W]
    kc, vc, ks, vs, kw, vw = [seg(k) for k in range(6)]
    gates = seg(6)[:, 32:32 + 3 * N_HEADS].reshape(b, 3, N_HEADS).transpose(0, 2, 1)
    gates = jnp.concatenate([gates, jnp.zeros((b, N_HEADS, KV_W - 3), F32)], axis=-1)
    kpool = _compress_pool(rows2d(cache_ck), layer, *w['cmp_k'][1])
    vpool = _compress_pool(rows2d(cache_cv), layer, *w['cmp_v'][1])
    o_c, idx = _sample_cmp(page_table, _group_padded(qp), kpool, vpool, t_pos)
    new_rows = jnp.stack([ks, vs, kw, vw] + [jnp.zeros_like(ks)] * 4, axis=1)
    att = _sample_attn(idx[:, :KV_GROUPS, :TOP_N], page_table, qr.reshape(b, N_HEADS, HEAD_DIM).astype(BF16),
                       _group_padded(qr), new_rows,
                       win_k.reshape(b, -1, KV_W), win_v.reshape(b, -1, KV_W), o_c, gates,
                       cache_sk, cache_sv, t_pos, layer)
    x = _merge(x, br_cf, br_ssm, att.reshape(b, D_MODEL), u, g1, w['w_attn_out'], w['w_o'], b)
    x = _ffn(x, w['norm2_g'], sc2, sh2, g2, w['w_ffn_in'], w['w_ffn_out'], b)
    keep = min(WINDOW, t_pos + 1)
    kv4 = lambda a: a.reshape(b, 1, KV_GROUPS, HEAD_DIM)
    new_kw = jnp.concatenate([win_k, kv4(kw)], axis=1)
    new_vw = jnp.concatenate([win_v, kv4(vw)], axis=1)
    state = (kv4(kc), kv4(vc), kv4(ks), kv4(vs),
             new_kw[:, new_kw.shape[1] - keep:], new_vw[:, new_vw.shape[1] - keep:],
             jnp.concatenate([conv_st[:, 1:], glu], axis=1),
             jnp.concatenate([sconv_st[:, 1:], u[:, None, XBC_OFF:XBC_OFF + SSM_CONV_DIM]], axis=1),
             h_new.reshape(b, SSM_HEADS, SSM_HEADDIM, D_STATE))
    return x, state


def kernel(x_prompt, x_sample, c_prompt, c_sample, cache_cmp_k, cache_cmp_v, cache_slc_k, cache_slc_v, state_win_k, state_win_v, state_conv, state_ssm_conv, state_ssm, page_table, norm1_g, w_ada, b_ada, w_in, cf_dw_w, cf_dw_b, cf_ln_g, cf_ln_b, w_cf_out, ssm_conv_w, ssm_conv_b, dt_bias, a_log, d_skip, ssm_norm_g, w_ssm_out, cmp_pe_k, cmp_pe_v, cmp_w1_k, cmp_w2_k, cmp_w1_v, cmp_w2_v, w_attn_out, w_o, norm2_g, w_ffn_in, w_ffn_out, final_g):
    depth = w_in.shape[0]
    bp, lp = x_prompt.shape[0], x_prompt.shape[1]
    bs = x_sample.shape[0]
    assert bp == 1 and x_sample.shape[1] == 1
    assert lp % TK == 0 and lp // BLOCK >= TOP_N
    t_pos = page_table.shape[1] * PAGE_SIZE
    params = dict(norm1_g=norm1_g, w_in=w_in, cf_dw_w=cf_dw_w, cf_dw_b=cf_dw_b, cf_ln_g=cf_ln_g, cf_ln_b=cf_ln_b,
                  w_cf_out=w_cf_out, ssm_conv_w=ssm_conv_w, ssm_conv_b=ssm_conv_b, dt_bias=dt_bias, a_log=a_log,
                  d_skip=d_skip, ssm_norm_g=ssm_norm_g, w_ssm_out=w_ssm_out, cmp_pe_k=cmp_pe_k, cmp_pe_v=cmp_pe_v,
                  cmp_w1_k=cmp_w1_k, cmp_w2_k=cmp_w2_k, cmp_w1_v=cmp_w1_v, cmp_w2_v=cmp_w2_v,
                  w_attn_out=w_attn_out, w_o=w_o, norm2_g=norm2_g, w_ffn_in=w_ffn_in, w_ffn_out=w_ffn_out)
    n_c = bp + bs
    c_pad = (-n_c) % 8
    c_all = jnp.concatenate([c_prompt, c_sample, jnp.zeros((c_pad, D_MODEL), F32)], axis=0)
    mod = _modulation(c_all, w_ada, b_ada)
    rexp = _head_expand_matrix()
    tab_p = _rope_tables(jnp.arange(lp, dtype=jnp.int32))
    tab_s = _rope_tables(jnp.full((bs,), t_pos, jnp.int32))
    caches = tuple(jnp.transpose(c, (0, 1, 3, 4, 2)) for c in (cache_cmp_k, cache_cmp_v, cache_slc_k, cache_slc_v))

    xp = x_prompt.reshape(lp, D_MODEL)
    xs = x_sample.reshape(bs, D_MODEL)
    outs_p = [[] for _ in range(9)]
    outs_s = [[] for _ in range(9)]
    for l in range(depth):
        w = _layer_weights(l, params)
        xp, st_p = _prompt_layer(xp, mod[l, 0:bp], w, tab_p, rexp)
        states = (state_win_k[l], state_win_v[l], state_conv[l], state_ssm_conv[l], state_ssm[l])
        xs, st_s = _sample_layer(xs, mod[l, bp:bp + bs], w, tab_s, rexp, caches, l, states, page_table, t_pos)
        for k in range(9):
            outs_p[k].append(st_p[k])
            outs_s[k].append(st_s[k])
    y_prompt = _final_norm(xp, final_g[None, :], 512).reshape(bp, lp, D_MODEL)
    y_sample = _final_norm(xs, final_g[None, :], bs).reshape(bs, 1, D_MODEL)
    np_ = [jnp.stack(o) for o in outs_p]
    ns_ = [jnp.stack(o) for o in outs_s]
    return (y_prompt, y_sample, *np_, *ns_)
```

```python
import functools

import numpy as np
import jax
import jax.numpy as jnp
from jax import lax
from jax.experimental import pallas as pl
from jax.experimental.pallas import tpu as pltpu

F32 = jnp.float32
BF16 = jnp.bfloat16

D_MODEL = 1024
PAGE_SIZE = 128
CONV_K = 31
D_INNER = 2 * D_MODEL
SSM_HEADDIM = 64
SSM_HEADS = D_INNER // SSM_HEADDIM
SSM_GROUPS = 4
D_STATE = 128
SSM_CONV_K = 4
SSM_CONV_DIM = D_INNER + 2 * SSM_GROUPS * D_STATE
SSM_CHUNK = 128
N_HEADS = 16
HEAD_DIM = 64
KV_GROUPS = 2
HEADS_PER_GROUP = N_HEADS // KV_GROUPS
BLOCK = 64
TOP_N = 16
WINDOW = 512
CMP_HIDDEN = 256
ROPE_THETA = 500000.0
ROT_DIM = HEAD_DIM // 4
BIG = 1e9
NEG = -1e30
D_FF = ((8 * D_MODEL + 3 * 256 - 1) // (3 * 256)) * 256
EPS = 1e-6
KV_W = KV_GROUPS * HEAD_DIM

XBC_OFF = 0
GMIX_OFF = 3072
UCF_OFF = 6144
Z_OFF = 8192
Q_OFF = 10240
KV_OFF = 11264
SMALL_OFF = KV_OFF + 6 * KV_W
N_PACK = 12288

TQ = 256
TK = 512
V_ROWS = HEAD_DIM + 16
Q_SCALE = HEAD_DIM ** -0.5 * 1.4426950408889634

VMEM_LIMIT = 56 * 1024 * 1024


def _cparams(*sem):
    return pltpu.CompilerParams(dimension_semantics=sem, vmem_limit_bytes=VMEM_LIMIT)


def _dot(a, b):
    return jnp.dot(a, b, preferred_element_type=F32)


def _dot_nt(a, b):
    return lax.dot_general(a, b, (((1,), (1,)), ((), ())), preferred_element_type=F32)


def _dot_tn(a, b):
    return lax.dot_general(a, b, (((0,), (0,)), ((), ())), preferred_element_type=F32)


def _sigmoid(x):
    return jax.nn.sigmoid(x)


def _silu(x):
    return x * jax.nn.sigmoid(x)


def _softplus(x):
    return jnp.maximum(x, 0.0) + jnp.log(1.0 + jnp.exp(-jnp.abs(x)))


def _split3(x):
    hi = x.astype(BF16)
    r1 = x - hi.astype(F32)
    mid = r1.astype(BF16)
    lo = (r1 - mid.astype(F32)).astype(BF16)
    return hi, mid, lo


def _dot_exact_rhs01(x, m01):
    hi, mid, lo = _split3(x)
    return _dot(hi, m01) + _dot(mid, m01) + _dot(lo, m01)


def _dot_exact_lhs01(m01, x):
    hi, mid, lo = _split3(x)
    return _dot(m01, hi) + _dot(m01, mid) + _dot(m01, lo)


def _mod_kernel(c_ref, w_ref, b_ref, o_ref):
    c = c_ref[...]
    o_ref[0] = _dot(_silu(c).astype(BF16), w_ref[0].astype(BF16)) + b_ref[0]


def _modulation(c_all, w_ada, b_ada):
    depth = w_ada.shape[0]
    bc = c_all.shape[0]
    n = w_ada.shape[2]
    tn = 1536
    return pl.pallas_call(
        _mod_kernel,
        grid=(depth, n // tn),
        in_specs=[pl.BlockSpec((bc, D_MODEL), lambda l, j: (0, 0)),
                  pl.BlockSpec((1, D_MODEL, tn), lambda l, j: (l, 0, j)),
                  pl.BlockSpec((1, 1, tn), lambda l, j: (l, 0, j))],
        out_specs=pl.BlockSpec((1, bc, tn), lambda l, j: (l, 0, j)),
        out_shape=jax.ShapeDtypeStruct((depth, bc, n), F32),
        compiler_params=_cparams("arbitrary", "arbitrary"),
        name="modulation",
    )(c_all, w_ada, b_ada.reshape(depth, 1, n))


def _in_proj_kernel(x_ref, g_ref, sc_ref, sh_ref, w_ref, o_ref, h_ref):
    @pl.when(pl.program_id(1) == 0)
    def _():
        x = x_ref[...]
        y = x * lax.rsqrt(jnp.mean(x * x, axis=-1, keepdims=True) + EPS) * g_ref[...]
        h_ref[...] = (y * (1.0 + sc_ref[...]) + sh_ref[...]).astype(BF16)

    o_ref[...] = _dot(h_ref[...], w_ref[...])


def _in_proj(x, g, sc, sh, w, tm):
    r = x.shape[0]
    n = w.shape[1]
    tn = 1024
    mrows = tm if sc.shape[0] == r else 1
    mmap = (lambda i, j: (i, 0)) if sc.shape[0] == r else (lambda i, j: (0, 0))
    return pl.pallas_call(
        _in_proj_kernel,
        grid=(r // tm, n // tn),
        in_specs=[pl.BlockSpec((tm, D_MODEL), lambda i, j: (i, 0)),
                  pl.BlockSpec((1, D_MODEL), lambda i, j: (0, 0)),
                  pl.BlockSpec((mrows, D_MODEL), mmap),
                  pl.BlockSpec((mrows, D_MODEL), mmap),
                  pl.BlockSpec((D_MODEL, tn), lambda i, j: (0, j))],
        out_specs=pl.BlockSpec((tm, tn), lambda i, j: (i, j)),
        out_shape=jax.ShapeDtypeStruct((r, n), F32),
        scratch_shapes=[pltpu.VMEM((tm, D_MODEL), BF16)],
        compiler_params=_cparams("arbitrary", "arbitrary"),
        name="in_proj",
    )(x, g, sc, sh, w)


def _matmul_kernel(x_ref, w_ref, o_ref):
    o_ref[...] = _dot(x_ref[...].astype(BF16), w_ref[...])


def _matmul(x, w):
    r, k = x.shape
    n = w.shape[1]
    return pl.pallas_call(
        _matmul_kernel,
        grid=(1,),
        in_specs=[pl.BlockSpec((r, k), lambda i: (0, 0)), pl.BlockSpec((k, n), lambda i: (0, 0))],
        out_specs=pl.BlockSpec((r, n), lambda i: (0, 0)),
        out_shape=jax.ShapeDtypeStruct((r, n), F32),
        compiler_params=_cparams("arbitrary"),
        name="row_matmul",
    )(x, w)


CF_TL = 256
CF_RC = 32
CF_HALO = 32


def _layernorm_silu(x, g, b):
    xc = x - jnp.mean(x, axis=-1, keepdims=True)
    y = xc * lax.rsqrt(jnp.mean(xc * xc, axis=-1, keepdims=True) + EPS)
    return _silu(y * g + b)


def _conformer_kernel(u_ref, w_ref, b_ref, lg_ref, lb_ref, wo_ref, o_ref, nc_ref, bufs, cvb):
    i = pl.program_id(0)
    tl = CF_TL

    @pl.when(i == 0)
    def _():
        bufs[0, 0:CF_HALO, :] = jnp.zeros((CF_HALO, D_MODEL), F32)

    u = u_ref[...]
    bufs[0, CF_HALO:CF_HALO + tl, :] = u[:, :D_MODEL] * _sigmoid(u[:, D_MODEL:])
    for s in range(1, 8):
        bufs[s, 0:tl + 24, :] = bufs[0, s:s + tl + 24, :]

    def chunk(c, carry):
        off = pl.multiple_of(c * CF_RC, CF_RC)
        acc = jnp.broadcast_to(b_ref[...], (CF_RC, D_MODEL))
        for k in range(CONV_K):
            a, s = divmod(CF_HALO - (CONV_K - 1) + k, 8)
            acc = acc + w_ref[k:k + 1, :] * bufs[s, pl.ds(off + 8 * a, CF_RC), :]
        cvb[pl.ds(off, CF_RC), :] = _layernorm_silu(acc, lg_ref[...], lb_ref[...]).astype(BF16)
        return carry

    lax.fori_loop(0, tl // CF_RC, chunk, 0)
    tail = bufs[0, tl:tl + CF_HALO, :]
    nc_ref[...] = tail
    bufs[0, 0:CF_HALO, :] = tail
    o_ref[...] = _dot(cvb[...], wo_ref[...])


def _conformer_prompt(u, dw_w, dw_b, ln_g, ln_b, w_out):
    l = u.shape[0]
    tl = CF_TL
    wpad = jnp.concatenate([dw_w, jnp.zeros((1, D_MODEL), F32)], axis=0)
    return pl.pallas_call(
        _conformer_kernel,
        grid=(l // tl,),
        in_specs=[pl.BlockSpec((tl, 2 * D_MODEL), lambda i: (i, UCF_OFF // (2 * D_MODEL))),
                  pl.BlockSpec((CONV_K + 1, D_MODEL), lambda i: (0, 0)),
                  pl.BlockSpec((1, D_MODEL), lambda i: (0, 0)),
                  pl.BlockSpec((1, D_MODEL), lambda i: (0, 0)),
                  pl.BlockSpec((1, D_MODEL), lambda i: (0, 0)),
                  pl.BlockSpec((D_MODEL, D_MODEL), lambda i: (0, 0))],
        out_specs=[pl.BlockSpec((tl, D_MODEL), lambda i: (i, 0)),
                   pl.BlockSpec((CF_HALO, D_MODEL), lambda i: (0, 0))],
        out_shape=[jax.ShapeDtypeStruct((l, D_MODEL), F32),
                   jax.ShapeDtypeStruct((CF_HALO, D_MODEL), F32)],
        scratch_shapes=[pltpu.VMEM((8, tl + CF_HALO, D_MODEL), F32),
                        pltpu.VMEM((tl, D_MODEL), BF16)],
        compiler_params=_cparams("arbitrary"),
        name="conformer_prompt",
    )(u, wpad, dw_b, ln_g, ln_b, w_out)


def _gated_group_norm(y, z, g):
    y = y * _silu(z)
    gw = D_INNER // SSM_GROUPS
    parts = []
    for k in range(SSM_GROUPS):
        yg = y[:, k * gw:(k + 1) * gw]
        parts.append(yg * lax.rsqrt(jnp.mean(yg * yg, axis=-1, keepdims=True) + EPS))
    return jnp.concatenate(parts, axis=1) * g


def _ssd_kernel(xbc_ref, z_ref, sm_ref, cw_ref, cb_ref, dtb_ref, alog_ref, dsk_ref, ng_ref, rexp_ref,
                wo_ref, o_ref, hout_ref, sc_ref, cbuf, hst):
    i = pl.program_id(0)
    q = SSM_CHUNK

    @pl.when(i == 0)
    def _():
        cbuf[0:8, :] = jnp.zeros((8, SSM_CONV_DIM), F32)
        hst[...] = jnp.zeros(hst.shape, F32)

    x = xbc_ref[...]
    cbuf[8:8 + q, :] = x
    conv = cb_ref[...] + cw_ref[3:4, :] * x
    for k in range(SSM_CONV_K - 1):
        conv = conv + cw_ref[k:k + 1, :] * cbuf[5 + k:5 + k + q, :]
    tail = cbuf[q:q + 8, :]
    sc_ref[...] = tail
    cbuf[0:8, :] = tail
    xa = _silu(conv)
    xs = xa[:, :D_INNER]
    bm = xa[:, D_INNER:D_INNER + SSM_GROUPS * D_STATE]
    cm = xa[:, D_INNER + SSM_GROUPS * D_STATE:]

    dt = _softplus(sm_ref[...] + dtb_ref[...])
    a = dt * (-jnp.exp(alog_ref[...]))
    row = lax.broadcasted_iota(jnp.int32, (q, q), 0)
    col = lax.broadcasted_iota(jnp.int32, (q, q), 1)
    tri = row >= col
    a_cum = _dot_exact_lhs01(tri.astype(BF16), a)
    a_cum_t = a_cum.T
    a_last = a_cum[q - 1:q, :]
    rexp = rexp_ref[...]
    dtx = _dot_exact_rhs01(dt, rexp)
    eax = _dot_exact_rhs01(jnp.exp(a_cum), rexp)
    decx = _dot_exact_rhs01(jnp.exp(a_last - a_cum), rexp)
    xin = xs * dtx
    xdec = (xin * decx).astype(BF16)
    xin_b = xin.astype(BF16)
    cd_col = jnp.broadcast_to(jnp.exp(a_last), (q, q)).T
    lane_lo = lax.broadcasted_iota(jnp.int32, (q, 2 * SSM_HEADDIM), 1) < SSM_HEADDIM

    hpg = SSM_HEADS // SSM_GROUPS
    gw = hpg * SSM_HEADDIM
    y_groups = []
    for g in range(SSM_GROUPS):
        cg = cm[:, g * D_STATE:(g + 1) * D_STATE].astype(BF16)
        bg = bm[:, g * D_STATE:(g + 1) * D_STATE].astype(BF16)
        cb = _dot_nt(cg, bg)
        hprev = hst[g * gw:(g + 1) * gw, :]
        y_off = _dot_nt(cg, hprev.astype(BF16)) * eax[:, g * gw:(g + 1) * gw]
        st = _dot_tn(xdec[:, g * gw:(g + 1) * gw], bg)
        pair_parts = []
        for pr in range(hpg // 2):
            xp = xin_b[:, g * gw + pr * 128:g * gw + (pr + 1) * 128]
            yp = None
            for sub in range(2):
                h = g * hpg + pr * 2 + sub
                seg = a_cum[:, h:h + 1] - a_cum_t[h:h + 1, :]
                lm = jnp.exp(jnp.where(tri, seg, -jnp.inf))
                mh = (cb * lm).astype(BF16)
                xm = jnp.where(lane_lo if sub == 0 else jnp.logical_not(lane_lo), xp, jnp.zeros_like(xp))
                d = _dot(mh, xm)
                yp = d if yp is None else yp + d
                r0 = h * SSM_HEADDIM
                lo = (pr * 2 + sub) * SSM_HEADDIM
                hst[r0:r0 + SSM_HEADDIM, :] = (hprev[lo:lo + SSM_HEADDIM, :] * cd_col[h:h + 1, :]
                                              + st[lo:lo + SSM_HEADDIM, :])
            pair_parts.append(yp)
        y_groups.append(jnp.concatenate(pair_parts, axis=1) + y_off)
    y = jnp.concatenate(y_groups, axis=1) + dsk_ref[...] * xs
    yn = _gated_group_norm(y, z_ref[...], ng_ref[...])
    o_ref[...] = _dot(yn.astype(BF16), wo_ref[...])

    @pl.when(i == pl.num_programs(0) - 1)
    def _():
        hout_ref[...] = hst[...]


def _pad_lanes(v, n=128):
    return jnp.concatenate([v, jnp.zeros((v.shape[0], n - v.shape[1]), v.dtype)], axis=1)


def _ssd_prompt(u, conv_w, conv_b, dt_bias, a_log, dskip_x, norm_g, rexp, w_out):
    l = u.shape[0]
    q = SSM_CHUNK
    cw = jnp.concatenate([conv_w, jnp.zeros((4, SSM_CONV_DIM), F32)], axis=0)
    const = lambda i: (0, 0)
    return pl.pallas_call(
        _ssd_kernel,
        grid=(l // q,),
        in_specs=[pl.BlockSpec((q, SSM_CONV_DIM), lambda i: (i, XBC_OFF // SSM_CONV_DIM)),
                  pl.BlockSpec((q, D_INNER), lambda i: (i, Z_OFF // D_INNER)),
                  pl.BlockSpec((q, 128), lambda i: (i, SMALL_OFF // 128)),
                  pl.BlockSpec((8, SSM_CONV_DIM), const),
                  pl.BlockSpec((1, SSM_CONV_DIM), const),
                  pl.BlockSpec((1, 128), const),
                  pl.BlockSpec((1, 128), const),
                  pl.BlockSpec((1, D_INNER), const),
                  pl.BlockSpec((1, D_INNER), const),
                  pl.BlockSpec((128, D_INNER), const),
                  pl.BlockSpec((D_INNER, D_MODEL), const)],
        out_specs=[pl.BlockSpec((q, D_MODEL), lambda i: (i, 0)),
                   pl.BlockSpec((D_INNER, D_STATE), const),
                   pl.BlockSpec((8, SSM_CONV_DIM), const)],
        out_shape=[jax.ShapeDtypeStruct((l, D_MODEL), F32),
                   jax.ShapeDtypeStruct((D_INNER, D_STATE), F32),
                   jax.ShapeDtypeStruct((8, SSM_CONV_DIM), F32)],
        scratch_shapes=[pltpu.VMEM((q + 8, SSM_CONV_DIM), F32),
                        pltpu.VMEM((D_INNER, D_STATE), F32)],
        compiler_params=_cparams("arbitrary"),
        name="ssd_prompt",
    )(u, u, u, cw, conv_b, _pad_lanes(dt_bias), _pad_lanes(a_log), dskip_x, norm_g, rexp, w_out)


def _rope(x, c, s1, s2):
    n = x.shape[1]
    return x * c + pltpu.roll(x, ROT_DIM // 2, 1) * s1 + pltpu.roll(x, n - ROT_DIM // 2, 1) * s2


def _aug_vt(v):
    vt = v.T.astype(BF16)
    ones = jnp.ones((V_ROWS - HEAD_DIM, v.shape[0]), BF16)
    return [jnp.concatenate([vt[g * HEAD_DIM:(g + 1) * HEAD_DIM, :], ones], axis=0) for g in range(KV_GROUPS)]


def _prep_kernel(q_ref, kv_ref, tab_ref, kc_ref, vc_ref, ks_ref, vs_ref, kw_ref, vw_ref,
                 qpt_ref, qrt_ref, ksb_ref, vst_ref, kwb_ref, vwt_ref, gt_ref):
    i = pl.program_id(0)
    tl = q_ref.shape[0]
    c = tab_ref[0]
    s1 = tab_ref[1]
    s2 = tab_ref[2]
    reps = D_MODEL // 128
    q = q_ref[...] * Q_SCALE
    qr = _rope(q, jnp.concatenate([c] * reps, axis=1), jnp.concatenate([s1] * reps, axis=1),
               jnp.concatenate([s2] * reps, axis=1))
    qpt_ref[...] = q.T.astype(BF16)
    qrt_ref[...] = qr.T.astype(BF16)
    kv = kv_ref[...]
    kc_ref[...] = kv[:, 0:KV_W]
    vc_ref[...] = kv[:, KV_W:2 * KV_W]
    ks = _rope(kv[:, 2 * KV_W:3 * KV_W], c, s1, s2)
    ks_ref[...] = ks
    row = lax.broadcasted_iota(jnp.int32, (tl, KV_W), 0)
    lane = lax.broadcasted_iota(jnp.int32, (tl, KV_W), 1)
    blk_in_tile = lax.shift_right_logical((i * tl) % TK + row, 6)
    ksb_ref[...] = jnp.concatenate([ks, jnp.where(lane == blk_in_tile, 1.0, 0.0)], axis=1).astype(BF16)
    vs = kv[:, 3 * KV_W:4 * KV_W]
    vs_ref[...] = vs
    kw = _rope(kv[:, 4 * KV_W:5 * KV_W], c, s1, s2)
    kw_ref[...] = kw
    kwb_ref[...] = kw.astype(BF16)
    vw = kv[:, 5 * KV_W:6 * KV_W]
    vw_ref[...] = vw
    for g, (a, b) in enumerate(zip(_aug_vt(vs), _aug_vt(vw))):
        vst_ref[0, g] = a
        vwt_ref[0, g] = b
    gt_ref[...] = _sigmoid(kv[:, 6 * KV_W:7 * KV_W]).T


def _prep_prompt(u, tab):
    l = u.shape[0]
    tl = TQ
    nt = l // tl
    row = lambda i: (i, 0)
    f32s = jax.ShapeDtypeStruct((l, KV_W), F32)
    vspec = pl.BlockSpec((1, KV_GROUPS, V_ROWS, tl), lambda i: (i, 0, 0, 0))
    vshape = jax.ShapeDtypeStruct((nt, KV_GROUPS, V_ROWS, tl), BF16)
    return pl.pallas_call(
        _prep_kernel,
        grid=(nt,),
        in_specs=[pl.BlockSpec((tl, D_MODEL), lambda i: (i, Q_OFF // D_MODEL)),
                  pl.BlockSpec((tl, D_MODEL), lambda i: (i, KV_OFF // D_MODEL)),
                  pl.BlockSpec((3, tl, 128), lambda i: (0, i, 0))],
        out_specs=[pl.BlockSpec((tl, KV_W), row)] * 6 + [
            pl.BlockSpec((D_MODEL, tl), lambda i: (0, i)),
            pl.BlockSpec((D_MODEL, tl), lambda i: (0, i)),
            pl.BlockSpec((tl, 2 * KV_W), row),
            vspec,
            pl.BlockSpec((tl, KV_W), row),
            vspec,
            pl.BlockSpec((128, tl), lambda i: (0, i))],
        out_shape=[f32s] * 6 + [
            jax.ShapeDtypeStruct((D_MODEL, l), BF16),
            jax.ShapeDtypeStruct((D_MODEL, l), BF16),
            jax.ShapeDtypeStruct((l, 2 * KV_W), BF16),
            vshape,
            jax.ShapeDtypeStruct((l, KV_W), BF16),
            vshape,
            jax.ShapeDtypeStruct((128, l), F32)],
        compiler_params=_cparams("arbitrary"),
        name="prep_prompt",
    )(u, u, tab)


def _prep_sample_kernel(q_ref, kv_ref, tab_ref, qp_ref, qr_ref, kvo_ref):
    c = tab_ref[0]
    s1 = tab_ref[1]
    s2 = tab_ref[2]
    reps = D_MODEL // 128
    q = q_ref[...] * Q_SCALE
    qp_ref[...] = q
    qr_ref[...] = _rope(q, jnp.concatenate([c] * reps, axis=1), jnp.concatenate([s1] * reps, axis=1),
                        jnp.concatenate([s2] * reps, axis=1))
    kv = kv_ref[...]
    one = jnp.ones_like(c)
    zero = jnp.zeros_like(c)
    ckv = jnp.concatenate([one, one, c, one, c, one, one, one], axis=1)
    s1kv = jnp.concatenate([zero, zero, s1, zero, s1, zero, zero, zero], axis=1)
    s2kv = jnp.concatenate([zero, zero, s2, zero, s2, zero, zero, zero], axis=1)
    kvr = _rope(kv, ckv, s1kv, s2kv)
    lane = lax.broadcasted_iota(jnp.int32, kv.shape, 1)
    kvo_ref[...] = jnp.where((lane >= 6 * KV_W) & (lane < 7 * KV_W), _sigmoid(kv), kvr)


def _prep_sample(u, tab):
    b = u.shape[0]
    full = lambda i: (0, 0)
    return pl.pallas_call(
        _prep_sample_kernel,
        grid=(1,),
        in_specs=[pl.BlockSpec((b, D_MODEL), lambda i: (0, Q_OFF // D_MODEL)),
                  pl.BlockSpec((b, D_MODEL), lambda i: (0, KV_OFF // D_MODEL)),
                  pl.BlockSpec((3, b, 128), lambda i: (0, 0, 0))],
        out_specs=[pl.BlockSpec((b, D_MODEL), full)] * 3,
        out_shape=[jax.ShapeDtypeStruct((b, D_MODEL), F32)] * 3,
        compiler_params=_cparams("arbitrary"),
        name="prep_sample",
    )(u, u, tab)


def _compress_kernel(x_ref, pe_ref, w1_ref, w2_ref, o_ref):
    x = (x_ref[...] + pe_ref[...]).astype(BF16)
    hid = _silu(_dot(x, w1_ref[...]))
    o_ref[...] = _dot(hid.astype(BF16), w2_ref[...])


def _compress(x2d, pe2, w1e, w2e):
    m = x2d.shape[0]
    tm = 256 if m % 256 == 0 else m
    kdim = BLOCK * KV_W
    return pl.pallas_call(
        _compress_kernel,
        grid=(m // tm,),
        in_specs=[pl.BlockSpec((tm, kdim), lambda i: (i, 0)),
                  pl.BlockSpec((1, kdim), lambda i: (0, 0)),
                  pl.BlockSpec((kdim, KV_GROUPS * CMP_HIDDEN), lambda i: (0, 0)),
                  pl.BlockSpec((KV_GROUPS * CMP_HIDDEN, KV_W), lambda i: (0, 0))],
        out_specs=pl.BlockSpec((tm, KV_W), lambda i: (i, 0)),
        out_shape=jax.ShapeDtypeStruct((m, KV_W), F32),
        compiler_params=_cparams("arbitrary"),
        name="compress",
    )(x2d, pe2, w1e, w2e)


def _compress_pool_kernel(x_ref, pe_ref, w1_ref, w2_ref, o_ref, xs, *, tp):
    stride = KV_GROUPS * HEAD_DIM
    for g in range(KV_GROUPS):
        for d in range(HEAD_DIM):
            rows = x_ref[pl.ds(g * HEAD_DIM + d, tp, stride=stride), :]
            xs[g * tp:(g + 1) * tp, d * PAGE_SIZE:(d + 1) * PAGE_SIZE] = (rows + pe_ref[d:d + 1, :]).astype(BF16)
    hid = _silu(_dot(xs[...], w1_ref[...]))
    o_ref[0] = _dot(hid.astype(BF16), w2_ref[...])


def _compress_pool(cache_t, layer, pe_t, w1t, w2e):
    n_pool = cache_t.shape[1] // (KV_GROUPS * HEAD_DIM)
    tp = 128 if n_pool % 128 == 0 else n_pool
    ns = n_pool // tp
    kdim = HEAD_DIM * PAGE_SIZE
    halves = PAGE_SIZE // BLOCK
    out = pl.pallas_call(
        functools.partial(_compress_pool_kernel, tp=tp),
        grid=(ns,),
        in_specs=[pl.BlockSpec((None, tp * KV_GROUPS * HEAD_DIM, PAGE_SIZE), lambda i: (layer, i, 0)),
                  pl.BlockSpec((HEAD_DIM, PAGE_SIZE), lambda i: (0, 0)),
                  pl.BlockSpec((kdim, halves * CMP_HIDDEN), lambda i: (0, 0)),
                  pl.BlockSpec((halves * CMP_HIDDEN, halves * HEAD_DIM), lambda i: (0, 0))],
        out_specs=pl.BlockSpec((1, KV_GROUPS * tp, halves * HEAD_DIM), lambda i: (i, 0, 0)),
        out_shape=jax.ShapeDtypeStruct((ns, KV_GROUPS * tp, halves * HEAD_DIM), F32),
        scratch_shapes=[pltpu.VMEM((KV_GROUPS * tp, kdim), BF16)],
        compiler_params=_cparams("arbitrary"),
        name="compress_pool",
    )(cache_t, pe_t, w1t, w2e)
    out = out.reshape(ns, KV_GROUPS, tp, halves, HEAD_DIM).transpose(0, 2, 3, 1, 4)
    return out.reshape(n_pool * halves, KV_W)


def _select_topn(score, index, axis, n_index):
    work = score
    for _ in range(TOP_N):
        m = jnp.max(work, axis=axis, keepdims=True)
        first = jnp.min(jnp.where(work == m, index, n_index), axis=axis, keepdims=True)
        work = jnp.where(index == first, -jnp.inf, work)
    return jnp.where(work == -jnp.inf, 1.0, 0.0)


def _attn_kernel(qpt_ref, qrt_ref, gt_ref, kcmp_ref, vcmp_ref, ks_ref, vst_ref,
                 kwa_ref, kwb_ref, kwc_ref, vwa_ref, vwb_ref, vwc_ref, o_ref,
                 rhs, bias16, m_sc, acc, outacc):
    i = pl.program_id(0)
    nb = kcmp_ref.shape[0]
    bpt = TK // BLOCK
    hw = HEADS_PER_GROUP * TQ
    t0 = i * TQ
    t_row = t0 + lax.broadcasted_iota(jnp.int32, (1, TQ), 1)
    zeros_half = jnp.zeros((HEAD_DIM, TQ), BF16)
    wide = lambda x: jnp.concatenate([x] * HEADS_PER_GROUP, axis=1)

    def group_queries(src_ref, g):
        cols = []
        for r in range(HEADS_PER_GROUP):
            h = g * HEADS_PER_GROUP + r
            qh = src_ref[h * HEAD_DIM:(h + 1) * HEAD_DIM, :]
            cols.append(jnp.concatenate([qh, zeros_half] if g == 0 else [zeros_half, qh], axis=0))
        return jnp.concatenate(cols, axis=1)

    def gate_row(branch, g):
        base = 32 + branch * N_HEADS + g * HEADS_PER_GROUP
        return jnp.concatenate([gt_ref[base + r:base + r + 1, :] for r in range(HEADS_PER_GROUP)], axis=1)

    @pl.when(i == 0)
    def _():
        rhs[...] = jnp.zeros(rhs.shape, BF16)

    kcmp = kcmp_ref[...].astype(BF16)
    vcmp_t = vcmp_ref[...].T.astype(BF16)
    jrow = lax.broadcasted_iota(jnp.int32, (nb, TQ), 0)
    mask_c1 = (jrow + 1) * BLOCK - 1 <= t_row
    mask_c = wide(mask_c1.astype(jnp.int32)) > 0
    cur = lax.shift_right_logical(t_row, 6)
    valid = jrow <= cur
    forced = valid & ((jrow == 0) | (jrow == cur) | (jrow == cur - 1))
    for g in range(KV_GROUPS):
        s = jnp.where(mask_c, _dot(kcmp, group_queries(qpt_ref, g)), NEG)
        m = jnp.max(s, axis=0, keepdims=True)
        e = jnp.where(mask_c, jnp.exp2(s - m), 0.0)
        p = e * (1.0 / jnp.maximum(jnp.sum(e, axis=0, keepdims=True), 1e-30))
        imp = p[:, 0:TQ]
        for r in range(1, HEADS_PER_GROUP):
            imp = imp + p[:, r * TQ:(r + 1) * TQ]
        o_c = _dot(vcmp_t, p.astype(BF16))[g * HEAD_DIM:(g + 1) * HEAD_DIM, :]
        outacc[g] = gate_row(0, g) * o_c
        rhs[g, 0:KV_W, :] = group_queries(qrt_ref, g)
        score = jnp.where(forced, BIG, jnp.where(valid, imp, -BIG))
        sel = _select_topn(score, jrow, 0, nb)
        bsel = jnp.where((sel > 0.5) & (score > -0.5 * BIG), 0.0, NEG)
        for c in range(nb // bpt):
            bias16[g, c] = jnp.concatenate([bsel[c * bpt:(c + 1) * bpt, :], jnp.zeros((16 - bpt, TQ), F32)],
                                           axis=0).astype(BF16)

    m_sc[...] = jnp.full(m_sc.shape, NEG, F32)
    acc[...] = jnp.zeros(acc.shape, F32)
    n_kt = (t0 + TQ + TK - 1) // TK

    def key_tile(kt, diagonal):
        k0 = pl.multiple_of(kt * TK, TK)
        k_aug = ks_ref[pl.ds(k0, TK), :]
        if diagonal:
            kpos = k0 + lax.broadcasted_iota(jnp.int32, (TK, TQ), 0)
            causal = wide(jnp.where(kpos > t_row, NEG, 0.0))
        scores = []
        for g in range(KV_GROUPS):
            rhs[g, KV_W:KV_W + 16, :] = wide(bias16[g, kt])
            scores.append(_dot(k_aug, rhs[g]))
        for g in range(KV_GROUPS):
            s = scores[g]
            if diagonal:
                s = s + causal
            m_old8 = m_sc[g]
            m_new8 = jnp.maximum(m_old8, jnp.max(s, axis=0, keepdims=True))
            m_new = m_new8[0:1, :]
            alpha = jnp.exp2(m_old8[0:1, :] - m_new)
            p = jnp.exp2(s - m_new).astype(BF16)
            vt = jnp.concatenate([vst_ref[kt * (TK // TQ) + c, g] for c in range(TK // TQ)], axis=1)
            acc[g] = alpha * acc[g] + _dot(vt, p)
            m_sc[g] = m_new8

    def full_tile(kt, carry):
        key_tile(kt, False)
        return carry

    lax.fori_loop(0, n_kt - 1, full_tile, 0)
    key_tile(n_kt - 1, True)

    kw = jnp.concatenate([kwa_ref[...], kwb_ref[...], kwc_ref[...]], axis=0)
    wpos = (i - 2) * TQ + lax.broadcasted_iota(jnp.int32, (3 * TQ, TQ), 0)
    bias_w = wide(jnp.where((wpos >= 0) & (wpos <= t_row) & (t_row - wpos < WINDOW), 0.0, NEG))
    for g in range(KV_GROUPS):
        s = _dot(kw, rhs[g, 0:KV_W, :]) + bias_w
        m = jnp.max(s, axis=0, keepdims=True)
        p = jnp.exp2(s - m).astype(BF16)
        pv = (_dot(vwa_ref[0, g], p[0:TQ, :]) + _dot(vwb_ref[0, g], p[TQ:2 * TQ, :])
              + _dot(vwc_ref[0, g], p[2 * TQ:, :]))
        o_w = pv[0:HEAD_DIM, :] * (1.0 / pv[HEAD_DIM:HEAD_DIM + 1, :])
        a = acc[g]
        o_s = a[0:HEAD_DIM, :] * (1.0 / a[HEAD_DIM:HEAD_DIM + 1, :])
        o_g = outacc[g] + gate_row(1, g) * o_s + gate_row(2, g) * o_w
        for r in range(0, HEADS_PER_GROUP, 2):
            h = g * HEADS_PER_GROUP + r
            pair = jnp.concatenate([o_g[:, r * TQ:(r + 1) * TQ], o_g[:, (r + 1) * TQ:(r + 2) * TQ]], axis=0)
            o_ref[:, h * HEAD_DIM:(h + 2) * HEAD_DIM] = pair.T


def _attention_prompt(qpt, qrt, gt, kcmp, vcmp, ksb, vst, kwb, vwt):
    l = qpt.shape[1]
    nb = kcmp.shape[0]
    nt = l // TQ
    const2 = lambda i: (0, 0)
    blk = lambda d: (lambda i: (jnp.maximum(i - d, 0), 0))
    blk4 = lambda d: (lambda i: (jnp.maximum(i - d, 0), 0, 0, 0))
    vblk = lambda d: pl.BlockSpec((1, KV_GROUPS, V_ROWS, TQ), blk4(d))
    return pl.pallas_call(
        _attn_kernel,
        grid=(nt,),
        in_specs=[pl.BlockSpec((D_MODEL, TQ), lambda i: (0, i)),
                  pl.BlockSpec((D_MODEL, TQ), lambda i: (0, i)),
                  pl.BlockSpec((128, TQ), lambda i: (0, i)),
                  pl.BlockSpec((nb, KV_W), const2),
                  pl.BlockSpec((nb, KV_W), const2),
                  pl.BlockSpec((l, 2 * KV_W), const2),
                  pl.BlockSpec((nt, KV_GROUPS, V_ROWS, TQ), lambda i: (0, 0, 0, 0)),
                  pl.BlockSpec((TQ, KV_W), blk(2)),
                  pl.BlockSpec((TQ, KV_W), blk(1)),
                  pl.BlockSpec((TQ, KV_W), blk(0)),
                  vblk(2), vblk(1), vblk(0)],
        out_specs=pl.BlockSpec((TQ, D_MODEL), lambda i: (i, 0)),
        out_shape=jax.ShapeDtypeStruct((l, D_MODEL), F32),
        scratch_shapes=[pltpu.VMEM((KV_GROUPS, 2 * KV_W, HEADS_PER_GROUP * TQ), BF16),
                        pltpu.VMEM((KV_GROUPS, nb // (TK // BLOCK), 16, TQ), BF16),
                        pltpu.VMEM((KV_GROUPS, 8, HEADS_PER_GROUP * TQ), F32),
                        pltpu.VMEM((KV_GROUPS, V_ROWS, HEADS_PER_GROUP * TQ), F32),
                        pltpu.VMEM((KV_GROUPS, HEAD_DIM, HEADS_PER_GROUP * TQ), F32)],
        compiler_params=_cparams("arbitrary"),
        name="attention_prompt",
    )(qpt, qrt, gt, kcmp, vcmp, ksb, vst, kwb, kwb, kwb, vwt, vwt, vwt)


def _merge_kernel(x_ref, cf_ref, ssm_ref, att_ref, gm_ref, g1_ref, wa_ref, wo_ref, o_ref):
    gm = _sigmoid(gm_ref[...])
    br_att = _dot(att_ref[...].astype(BF16), wa_ref[...])
    mixed = (gm[:, :D_MODEL] * cf_ref[...] + gm[:, D_MODEL:2 * D_MODEL] * ssm_ref[...]
             + gm[:, 2 * D_MODEL:] * br_att)
    o_ref[...] = x_ref[...] + g1_ref[...] * _dot(mixed.astype(BF16), wo_ref[...])


def _merge(x, br_cf, br_ssm, att, u, g1, w_attn, w_o, tm):
    r = x.shape[0]
    mrows = tm if g1.shape[0] == r else 1
    mmap = (lambda i: (i, 0)) if g1.shape[0] == r else (lambda i: (0, 0))
    row = lambda i: (i, 0)
    const = lambda i: (0, 0)
    return pl.pallas_call(
        _merge_kernel,
        grid=(r // tm,),
        in_specs=[pl.BlockSpec((tm, D_MODEL), row)] * 4 + [
            pl.BlockSpec((tm, 3 * D_MODEL), lambda i: (i, GMIX_OFF // (3 * D_MODEL))),
            pl.BlockSpec((mrows, D_MODEL), mmap),
            pl.BlockSpec((D_MODEL, D_MODEL), const),
            pl.BlockSpec((D_MODEL, D_MODEL), const)],
        out_specs=pl.BlockSpec((tm, D_MODEL), row),
        out_shape=jax.ShapeDtypeStruct((r, D_MODEL), F32),
        compiler_params=_cparams("arbitrary"),
        name="merge",
    )(x, br_cf, br_ssm, att, u, g1, w_attn, w_o)


FF_T = D_FF // 2


def _ffn_kernel(x_ref, g_ref, sc_ref, sh_ref, g2_ref, wg_ref, wu_ref, wo_ref, o_ref, h_ref, acc_ref):
    j = pl.program_id(1)

    @pl.when(j == 0)
    def _():
        x = x_ref[...]
        y = x * lax.rsqrt(jnp.mean(x * x, axis=-1, keepdims=True) + EPS) * g_ref[...]
        h_ref[...] = (y * (1.0 + sc_ref[...]) + sh_ref[...]).astype(BF16)
        acc_ref[...] = jnp.zeros(acc_ref.shape, F32)

    h = h_ref[...]
    act = _silu(_dot(h, wg_ref[...])) * _dot(h, wu_ref[...])
    acc_ref[...] += _dot(act.astype(BF16), wo_ref[...])

    @pl.when(j == pl.num_programs(1) - 1)
    def _():
        o_ref[...] = x_ref[...] + g2_ref[...] * acc_ref[...]


def _ffn(x, g, sc, sh, g2, w_in, w_out, tm):
    r = x.shape[0]
    nj = D_FF // FF_T
    mrows = tm if sc.shape[0] == r else 1
    mmap = (lambda i, j: (i, 0)) if sc.shape[0] == r else (lambda i, j: (0, 0))
    row = lambda i, j: (i, 0)
    return pl.pallas_call(
        _ffn_kernel,
        grid=(r // tm, nj),
        in_specs=[pl.BlockSpec((tm, D_MODEL), row),
                  pl.BlockSpec((1, D_MODEL), lambda i, j: (0, 0)),
                  pl.BlockSpec((mrows, D_MODEL), mmap),
                  pl.BlockSpec((mrows, D_MODEL), mmap),
                  pl.BlockSpec((mrows, D_MODEL), mmap),
                  pl.BlockSpec((D_MODEL, FF_T), lambda i, j: (0, j)),
                  pl.BlockSpec((D_MODEL, FF_T), lambda i, j: (0, nj + j)),
                  pl.BlockSpec((FF_T, D_MODEL), lambda i, j: (j, 0))],
        out_specs=pl.BlockSpec((tm, D_MODEL), row),
        out_shape=jax.ShapeDtypeStruct((r, D_MODEL), F32),
        scratch_shapes=[pltpu.VMEM((tm, D_MODEL), BF16), pltpu.VMEM((tm, D_MODEL), F32)],
        compiler_params=_cparams("arbitrary", "arbitrary"),
        name="ffn",
    )(x, g, sc, sh, g2, w_in, w_in, w_out)


def _final_norm_kernel(x_ref, g_ref, o_ref):
    x = x_ref[...]
    o_ref[...] = x * lax.rsqrt(jnp.mean(x * x, axis=-1, keepdims=True) + EPS) * g_ref[...]


def _final_norm(x, g, tm):
    r = x.shape[0]
    return pl.pallas_call(
        _final_norm_kernel,
        grid=(r // tm,),
        in_specs=[pl.BlockSpec((tm, D_MODEL), lambda i: (i, 0)), pl.BlockSpec((1, D_MODEL), lambda i: (0, 0))],
        out_specs=pl.BlockSpec((tm, D_MODEL), lambda i: (i, 0)),
        out_shape=jax.ShapeDtypeStruct((r, D_MODEL), F32),
        compiler_params=_cparams("arbitrary"),
        name="final_norm",
    )(x, g)


def _sample_mix_kernel(ucf_ref, z_ref, xbc_ref, sm_ref, cst_ref, sst_ref, h0_ref,
                       dww_ref, dwb_ref, lg_ref, lb_ref, cw_ref, cb_ref, dtb_ref, alog_ref,
                       dsk_ref, ng_ref, rexp_ref,
                       cv_ref, yn_ref, glu_ref, hnew_ref):
    u = ucf_ref[...]
    glu = u[:, :D_MODEL] * _sigmoid(u[:, D_MODEL:])
    glu_ref[...] = glu
    conv = (dwb_ref[...] + dww_ref[CONV_K - 1:CONV_K, :] * glu
            + jnp.sum(dww_ref[0:CONV_K - 1, :] * cst_ref[...], axis=0, keepdims=True))
    cv_ref[...] = _layernorm_silu(conv, lg_ref[...], lb_ref[...])

    xbc = xbc_ref[...]
    conv = (cb_ref[...] + cw_ref[SSM_CONV_K - 1:SSM_CONV_K, :] * xbc
            + jnp.sum(cw_ref[0:SSM_CONV_K - 1, :] * sst_ref[...], axis=0, keepdims=True))
    xa = _silu(conv)
    xs = xa[:, :D_INNER]
    bm = xa[:, D_INNER:D_INNER + SSM_GROUPS * D_STATE]
    cm = xa[:, D_INNER + SSM_GROUPS * D_STATE:]
    dt = _softplus(sm_ref[...] + dtb_ref[...])
    dec = jnp.exp(dt * (-jnp.exp(alog_ref[...])))
    dtx = _dot_exact_rhs01(jnp.broadcast_to(dt, (8, 128)), rexp_ref[...])[0:1, :]
    xin = xs * dtx
    dec_col = jnp.broadcast_to(dec, (128, 128)).T
    nblk = D_INNER // 128
    xrows = jnp.concatenate([xin[:, a * 128:(a + 1) * 128] for a in range(nblk)]
                            + [jnp.zeros((128 - nblk, 128), F32)], axis=0)
    x_col = xrows.T
    blocks_per_group = (D_INNER // SSM_GROUPS) // 128
    for a in range(nblk):
        g = a // blocks_per_group
        bg = bm[:, g * D_STATE:(g + 1) * D_STATE]
        dcol = jnp.concatenate([jnp.broadcast_to(dec_col[2 * a:2 * a + 1, :], (SSM_HEADDIM, 128)),
                                jnp.broadcast_to(dec_col[2 * a + 1:2 * a + 2, :], (SSM_HEADDIM, 128))], axis=0)
        hnew_ref[a * 128:(a + 1) * 128, :] = h0_ref[a * 128:(a + 1) * 128, :] * dcol + x_col[:, a:a + 1] * bg
    c8 = jnp.concatenate([cm[:, g * D_STATE:(g + 1) * D_STATE] for g in range(SSM_GROUPS)]
                         + [jnp.zeros((8 - SSM_GROUPS, D_STATE), F32)], axis=0)
    yall = _dot_nt(c8.astype(BF16), hnew_ref[...].astype(BF16))
    lane = lax.broadcasted_iota(jnp.int32, (1, D_INNER), 1)
    gw = D_INNER // SSM_GROUPS
    y = jnp.zeros((1, D_INNER), F32)
    for g in range(SSM_GROUPS):
        y = y + jnp.where((lane >= g * gw) & (lane < (g + 1) * gw), yall[g:g + 1, :], 0.0)
    y = y + dsk_ref[...] * xs
    yn_ref[...] = _gated_group_norm(y, z_ref[...], ng_ref[...])


def _sample_mix(u3, conv_st, sconv_st, h0, dw_w, dw_b, ln_g, ln_b, conv_w, conv_b, dt_bias, a_log,
                dskip_x, norm_g, rexp):
    b = u3.shape[0]
    const = lambda i: (0, 0)
    ublk = lambda w, off: pl.BlockSpec((None, 1, w), lambda i: (i, 0, off // w))
    out1 = lambda w: pl.BlockSpec((None, 1, w), lambda i: (i, 0, 0))
    return pl.pallas_call(
        _sample_mix_kernel,
        grid=(b,),
        in_specs=[ublk(2 * D_MODEL, UCF_OFF), ublk(D_INNER, Z_OFF), ublk(SSM_CONV_DIM, XBC_OFF),
                  ublk(128, SMALL_OFF),
                  pl.BlockSpec((None, CONV_K - 1, D_MODEL), lambda i: (i, 0, 0)),
                  pl.BlockSpec((None, SSM_CONV_K - 1, SSM_CONV_DIM), lambda i: (i, 0, 0)),
                  pl.BlockSpec((None, D_INNER, D_STATE), lambda i: (i, 0, 0)),
                  pl.BlockSpec((CONV_K, D_MODEL), const),
                  pl.BlockSpec((1, D_MODEL), const),
                  pl.BlockSpec((1, D_MODEL), const),
                  pl.BlockSpec((1, D_MODEL), const),
                  pl.BlockSpec((SSM_CONV_K, SSM_CONV_DIM), const),
                  pl.BlockSpec((1, SSM_CONV_DIM), const),
                  pl.BlockSpec((1, 128), const),
                  pl.BlockSpec((1, 128), const),
                  pl.BlockSpec((1, D_INNER), const),
                  pl.BlockSpec((1, D_INNER), const),
                  pl.BlockSpec((128, D_INNER), const)],
        out_specs=[out1(D_MODEL), out1(D_INNER), out1(D_MODEL),
                   pl.BlockSpec((None, D_INNER, D_STATE), lambda i: (i, 0, 0))],
        out_shape=[jax.ShapeDtypeStruct((b, 1, D_MODEL), F32),
                   jax.ShapeDtypeStruct((b, 1, D_INNER), F32),
                   jax.ShapeDtypeStruct((b, 1, D_MODEL), F32),
                   jax.ShapeDtypeStruct((b, D_INNER, D_STATE), F32)],
        compiler_params=_cparams("arbitrary"),
        name="sample_mix",
    )(u3, u3, u3, u3, conv_st, sconv_st, h0, dw_w, dw_b, ln_g, ln_b, conv_w, conv_b,
      _pad_lanes(dt_bias), _pad_lanes(a_log), dskip_x, norm_g, rexp)


SEL_LANES = 128


def _sample_cmp_kernel(pt_ref, qp_ref, kpool_ref, vpool_ref, oc_ref, idx_ref, kbuf, vbuf, *, n_pages, t_pos):
    b = pl.program_id(0)
    nb = 2 * n_pages
    for pg in range(n_pages):
        src = pt_ref[b, pg] * 2
        kbuf[2 * pg:2 * pg + 2, :] = kpool_ref[pl.ds(src, 2), :]
        vbuf[2 * pg:2 * pg + 2, :] = vpool_ref[pl.ds(src, 2), :]
    qp = qp_ref[...]
    s = _dot_nt(qp, kbuf[...].astype(BF16))
    jl = lax.broadcasted_iota(jnp.int32, (N_HEADS, nb), 1)
    mask_c = (jl + 1) * BLOCK - 1 <= t_pos
    s = jnp.where(mask_c, s, NEG)
    m = jnp.max(s, axis=1, keepdims=True)
    e = jnp.where(mask_c, jnp.exp2(s - m), 0.0)
    p = e * (1.0 / jnp.maximum(jnp.sum(e, axis=1, keepdims=True), 1e-30))
    oc_ref[...] = _dot(p.astype(BF16), vbuf[...].astype(BF16))

    nbs = t_pos // BLOCK + 1
    width = ((nbs + 127) // 128) * 128
    imp = jnp.concatenate(
        [jnp.sum(p[g * HEADS_PER_GROUP:(g + 1) * HEADS_PER_GROUP, :], axis=0, keepdims=True)
         for g in range(KV_GROUPS)] + [jnp.zeros((8 - KV_GROUPS, nb), F32)], axis=0)
    imp = jnp.concatenate([imp, jnp.zeros((8, width - nb), F32)], axis=1)
    jw = lax.broadcasted_iota(jnp.int32, (8, width), 1)
    cur = t_pos // BLOCK
    valid = jw <= cur
    forced = valid & ((jw == 0) | (jw == cur) | (jw == cur - 1))
    score = jnp.where(forced, BIG, jnp.where(valid, imp, -BIG))
    work = jnp.where(jw < nbs, score, -jnp.inf)
    lane = lax.broadcasted_iota(jnp.int32, (8, SEL_LANES), 1)
    out = jnp.full((8, SEL_LANES), -1, jnp.int32)
    for it in range(TOP_N):
        mx = jnp.max(work, axis=1, keepdims=True)
        first = jnp.min(jnp.where(work == mx, jw, width), axis=1, keepdims=True)
        out = jnp.where(lane == it, jnp.where(mx > -0.5 * BIG, first, -1), out)
        work = jnp.where(jw == first, -jnp.inf, work)
    idx_ref[...] = out


def _sample_cmp(page_table, qp_pad, kpool, vpool, t_pos):
    b, n_pages = page_table.shape
    npool = kpool.shape[0]
    nb = 2 * n_pages
    kern = functools.partial(_sample_cmp_kernel, n_pages=n_pages, t_pos=t_pos)
    return pl.pallas_call(
        kern,
        grid_spec=pltpu.PrefetchScalarGridSpec(
            num_scalar_prefetch=1,
            grid=(b,),
            in_specs=[pl.BlockSpec((None, N_HEADS, KV_W), lambda i, pt: (i, 0, 0)),
                      pl.BlockSpec((npool, KV_W), lambda i, pt: (0, 0)),
                      pl.BlockSpec((npool, KV_W), lambda i, pt: (0, 0))],
            out_specs=[pl.BlockSpec((None, N_HEADS, KV_W), lambda i, pt: (i, 0, 0)),
                       pl.BlockSpec((None, 8, SEL_LANES), lambda i, pt: (i, 0, 0))],
            scratch_shapes=[pltpu.VMEM((nb, KV_W), F32), pltpu.VMEM((nb, KV_W), F32)]),
        out_shape=[jax.ShapeDtypeStruct((b, N_HEADS, KV_W), F32),
                   jax.ShapeDtypeStruct((b, 8, SEL_LANES), jnp.int32)],
        compiler_params=_cparams("arbitrary"),
        name="sample_cmp",
    )(page_table, qp_pad, kpool, vpool)


def _sample_attn_kernel(idx_ref, pt_ref, q16_ref, qr_ref, new_ref, wk_ref, wv_ref, oc_ref, gate_ref,
                        kcache_ref, vcache_ref, o_ref, kbuf, vbuf, sem, *, n_pages, t_pos, layer):
    b = pl.program_id(0)
    nb_past = 2 * n_pages

    def page_copy(cache_ref, buf, g, k, which):
        j = idx_ref[b, g, k]
        jj = jnp.where((j >= 0) & (j < nb_past), j, 0)
        page = pt_ref[b, jj // 2]
        return pltpu.make_async_copy(cache_ref.at[layer, page, g],
                                     buf.at[g, :, pl.ds(k * PAGE_SIZE, PAGE_SIZE)], sem.at[which, g, k])

    for g in range(KV_GROUPS):
        for k in range(TOP_N):
            page_copy(kcache_ref, kbuf, g, k, 0).start()
            page_copy(vcache_ref, vbuf, g, k, 1).start()

    qr = qr_ref[...]
    qf = qr.astype(F32)
    new = new_ref[...]
    grp_lo = lax.broadcasted_iota(jnp.int32, (N_HEADS, KV_W), 0) < HEADS_PER_GROUP
    lane_lo = lax.broadcasted_iota(jnp.int32, (N_HEADS, KV_W), 1) < HEAD_DIM
    own = grp_lo == lane_lo

    def own_half(x):
        x = jnp.where(own, x, 0.0)
        return x[:, :HEAD_DIM] + x[:, HEAD_DIM:]

    win = wk_ref.shape[0]
    s = _dot_nt(qr, wk_ref[...].astype(BF16))
    wpos = t_pos - win + lax.broadcasted_iota(jnp.int32, (N_HEADS, win), 1)
    mask_w = t_pos - wpos < WINDOW
    s = jnp.where(mask_w, s, NEG)
    s_new = jnp.sum(qf * new[2:3, :], axis=1, keepdims=True)
    m = jnp.maximum(jnp.max(s, axis=1, keepdims=True), s_new)
    p = jnp.where(mask_w, jnp.exp2(s - m), 0.0)
    p_new = jnp.exp2(s_new - m)
    lsum = jnp.sum(p, axis=1, keepdims=True) + p_new
    o_w = own_half((_dot(p.astype(BF16), wv_ref[...].astype(BF16)) + p_new * new[3:4, :]) * (1.0 / lsum))

    for g in range(KV_GROUPS):
        for k in range(TOP_N):
            page_copy(kcache_ref, kbuf, g, k, 0).wait()
            page_copy(vcache_ref, vbuf, g, k, 1).wait()
    nsel = TOP_N * PAGE_SIZE
    lane = lax.broadcasted_iota(jnp.int32, (1, nsel), 1)
    lane_slot = lane // PAGE_SIZE
    lane_half = (lane // BLOCK) % 2
    o_s_parts = []
    for g in range(KV_GROUPS):
        ok = jnp.zeros((1, nsel), jnp.int32)
        has_new = jnp.zeros((1, 1), jnp.int32)
        for k in range(TOP_N):
            j = idx_ref[b, g, k]
            in_cache = (j >= 0) & (j < nb_past)
            hit = jnp.where(in_cache, (lane_half == j % 2).astype(jnp.int32), 0)
            ok = jnp.where(lane_slot == k, hit, ok)
            has_new = jnp.maximum(has_new, (j == nb_past).astype(jnp.int32))
        mask_s = ok > 0
        mask_n = has_new > 0
        qg = q16_ref[g * HEADS_PER_GROUP:(g + 1) * HEADS_PER_GROUP, :]
        k_new = new[0:1, g * HEAD_DIM:(g + 1) * HEAD_DIM]
        v_new = new[1:2, g * HEAD_DIM:(g + 1) * HEAD_DIM]
        s = jnp.where(mask_s, _dot(qg, kbuf[g].astype(BF16)), NEG)
        s_new = jnp.where(mask_n, jnp.sum(qg.astype(F32) * k_new, axis=1, keepdims=True), NEG)
        m = jnp.maximum(jnp.max(s, axis=1, keepdims=True), s_new)
        p = jnp.where(mask_s, jnp.exp2(s - m), 0.0)
        p_new = jnp.where(mask_n, jnp.exp2(s_new - m), 0.0)
        lsum = jnp.maximum(jnp.sum(p, axis=1, keepdims=True) + p_new, 1e-30)
        o_s_parts.append((_dot_nt(p.astype(BF16), vbuf[g].astype(BF16)) + p_new * v_new) * (1.0 / lsum))
    o_s = jnp.concatenate(o_s_parts, axis=0)

    gate = gate_ref[...]
    o_ref[...] = gate[:, 0:1] * own_half(oc_ref[...]) + gate[:, 1:2] * o_s + gate[:, 2:3] * o_w


def _sample_attn(idx, page_table, q16, qr_pad, new_rows, win_k, win_v, o_c, gates, kcache, vcache, t_pos, layer):
    b, n_pages = page_table.shape
    win = win_k.shape[1]
    kern = functools.partial(_sample_attn_kernel, n_pages=n_pages, t_pos=t_pos, layer=layer)
    per_b = lambda r, c: pl.BlockSpec((None, r, c), lambda i, ix, pt: (i, 0, 0))
    return pl.pallas_call(
        kern,
        grid_spec=pltpu.PrefetchScalarGridSpec(
            num_scalar_prefetch=2,
            grid=(b,),
            in_specs=[per_b(N_HEADS, HEAD_DIM), per_b(N_HEADS, KV_W), per_b(8, KV_W), per_b(win, KV_W),
                      per_b(win, KV_W), per_b(N_HEADS, KV_W), per_b(N_HEADS, KV_W),
                      pl.BlockSpec(memory_space=pl.ANY), pl.BlockSpec(memory_space=pl.ANY)],
            out_specs=per_b(N_HEADS, HEAD_DIM),
            scratch_shapes=[pltpu.VMEM((KV_GROUPS, HEAD_DIM, TOP_N * PAGE_SIZE), F32),
                            pltpu.VMEM((KV_GROUPS, HEAD_DIM, TOP_N * PAGE_SIZE), F32),
                            pltpu.SemaphoreType.DMA((2, KV_GROUPS, TOP_N))]),
        out_shape=jax.ShapeDtypeStruct((b, N_HEADS, HEAD_DIM), F32),
        compiler_params=_cparams("arbitrary"),
        name="sample_attn",
    )(idx, page_table, q16, qr_pad, new_rows, win_k, win_v, o_c, gates, kcache, vcache)


def _pack_w_in(w):
    segs = [w[:, 4096:7168], w[:, 9040:12112], w[:, 0:2048], w[:, 2048:4096], w[:, 7200:8224],
            w[:, 8224:8992], w[:, 7168:7200], w[:, 8992:9040],
            jnp.zeros((D_MODEL, N_PACK - 12112), w.dtype)]
    return jnp.concatenate(segs, axis=1).astype(BF16)


def _expand_cmp_weights(pe, w1, w2):
    eye = jnp.eye(KV_GROUPS, dtype=F32)
    w1r = w1.reshape(BLOCK, HEAD_DIM, CMP_HIDDEN)
    w1e = jnp.einsum('ldh,gk->lgdkh', w1r, eye).reshape(BLOCK * KV_W, KV_GROUPS * CMP_HIDDEN).astype(BF16)
    w2e = jnp.einsum('hd,gk->ghkd', w2, eye).reshape(KV_GROUPS * CMP_HIDDEN, KV_W).astype(BF16)
    pe2 = jnp.broadcast_to(pe[:, None, :], (BLOCK, KV_GROUPS, HEAD_DIM)).reshape(1, BLOCK * KV_W)
    halves = PAGE_SIZE // BLOCK
    w1t = jnp.einsum('ldh,ab->dalbh', w1r, jnp.eye(halves, dtype=F32))
    w1t = w1t.reshape(HEAD_DIM * PAGE_SIZE, halves * CMP_HIDDEN).astype(BF16)
    pe_t = jnp.tile(pe.T, (1, halves))
    return (pe2, w1e, w2e), (pe_t, w1t, w2e)


def _rope_tables(pos):
    half = ROT_DIM // 2
    inv = ROPE_THETA ** (-(jnp.arange(half, dtype=F32) * 2.0 / ROT_DIM))
    ang = pos.astype(F32)[:, None] * inv[None, :]
    cos = jnp.cos(ang)
    sin = jnp.sin(ang)
    n = pos.shape[0]
    one = jnp.ones((n, HEAD_DIM - ROT_DIM), F32)
    zero8 = jnp.zeros((n, half), F32)
    zero = jnp.zeros((n, HEAD_DIM - ROT_DIM), F32)
    c = jnp.concatenate([cos, cos, one], axis=1)
    s1 = jnp.concatenate([zero8, sin, zero], axis=1)
    s2 = jnp.concatenate([-sin, zero8, zero], axis=1)
    return jnp.stack([jnp.concatenate([t, t], axis=1) for t in (c, s1, s2)])


def _head_expand_matrix():
    h = np.arange(128)[:, None]
    lane = np.arange(D_INNER)[None, :]
    return jnp.asarray((lane // SSM_HEADDIM == h).astype(np.float32), dtype=BF16)


def _group_padded(q):
    b = q.shape[0]
    qh = q.reshape(b, KV_GROUPS, HEADS_PER_GROUP, HEAD_DIM)
    z = jnp.zeros_like(qh[:, 0])
    lo = jnp.concatenate([qh[:, 0], z], axis=-1)
    hi = jnp.concatenate([z, qh[:, 1]], axis=-1)
    return jnp.concatenate([lo, hi], axis=1).astype(BF16)


def _layer_weights(l, p):
    w = {k: v[l] for k, v in p.items()}
    w['w_in_p'] = _pack_w_in(w['w_in'])
    for nm in ('w_cf_out', 'w_ssm_out', 'w_attn_out', 'w_o', 'w_ffn_in', 'w_ffn_out'):
        w[nm] = w[nm].astype(BF16)
    w['cmp_k'] = _expand_cmp_weights(w['cmp_pe_k'], w['cmp_w1_k'], w['cmp_w2_k'])
    w['cmp_v'] = _expand_cmp_weights(w['cmp_pe_v'], w['cmp_w1_v'], w['cmp_w2_v'])
    w['dskip_x'] = jnp.repeat(w['d_skip'], SSM_HEADDIM)[None, :]
    for nm in ('norm1_g', 'norm2_g', 'cf_dw_b', 'cf_ln_g', 'cf_ln_b', 'ssm_conv_b', 'dt_bias', 'a_log',
               'ssm_norm_g'):
        w[nm] = w[nm][None, :]
    return w


def _prompt_layer(x, mod, w, tab, rexp):
    l = x.shape[0]
    sh1, sc1, g1, sh2, sc2, g2 = [mod[:, k * D_MODEL:(k + 1) * D_MODEL] for k in range(6)]
    u = _in_proj(x, w['norm1_g'], sc1, sh1, w['w_in_p'], 1024)
    br_cf, conv_tail = _conformer_prompt(u, w['cf_dw_w'], w['cf_dw_b'], w['cf_ln_g'], w['cf_ln_b'], w['w_cf_out'])
    br_ssm, h_new, sconv_tail = _ssd_prompt(u, w['ssm_conv_w'], w['ssm_conv_b'], w['dt_bias'], w['a_log'],
                                            w['dskip_x'], w['ssm_norm_g'], rexp, w['w_ssm_out'])
    kc, vc, ks, vs, kw, vw, qpt, qrt, ksb, vst, kwb, vwt, gt = _prep_prompt(u, tab)
    kcmp = _compress(kc.reshape(l // BLOCK, BLOCK * KV_W), *w['cmp_k'][0])
    vcmp = _compress(vc.reshape(l // BLOCK, BLOCK * KV_W), *w['cmp_v'][0])
    att = _attention_prompt(qpt, qrt, gt, kcmp, vcmp, ksb, vst, kwb, vwt)
    x = _merge(x, br_cf, br_ssm, att, u, g1, w['w_attn_out'], w['w_o'], 512)
    x = _ffn(x, w['norm2_g'], sc2, sh2, g2, w['w_ffn_in'], w['w_ffn_out'], 512)
    keep = min(WINDOW, l)
    kv4 = lambda a: a.reshape(1, -1, KV_GROUPS, HEAD_DIM)
    state = (kv4(kc), kv4(vc), kv4(ks), kv4(vs), kv4(kw[l - keep:]), kv4(vw[l - keep:]),
             conv_tail[CF_HALO - (CONV_K - 1):][None],
             sconv_tail[8 - (SSM_CONV_K - 1):][None],
             h_new.reshape(1, SSM_HEADS, SSM_HEADDIM, D_STATE))
    return x, state


def _sample_layer(x, mod, w, tab, rexp, caches, layer, states, page_table, t_pos):
    b = x.shape[0]
    cache_ck, cache_cv, cache_sk, cache_sv = caches
    win_k, win_v, conv_st, sconv_st, ssm_st = states
    depth, n_pool = cache_ck.shape[0], cache_ck.shape[1]
    rows2d = lambda c: c.reshape(depth, n_pool * KV_GROUPS * HEAD_DIM, PAGE_SIZE)
    sh1, sc1, g1, sh2, sc2, g2 = [mod[:, k * D_MODEL:(k + 1) * D_MODEL] for k in range(6)]
    u = _in_proj(x, w['norm1_g'], sc1, sh1, w['w_in_p'], b)
    cv, yn, glu, h_new = _sample_mix(u.reshape(b, 1, N_PACK), conv_st, sconv_st,
                                     ssm_st.reshape(b, D_INNER, D_STATE),
                                     w['cf_dw_w'], w['cf_dw_b'], w['cf_ln_g'], w['cf_ln_b'],
                                     w['ssm_conv_w'], w['ssm_conv_b'], w['dt_bias'], w['a_log'],
                                     w['dskip_x'], w['ssm_norm_g'], rexp)
    br_cf = _matmul(cv.reshape(b, D_MODEL), w['w_cf_out'])
    br_ssm = _matmul(yn.reshape(b, D_INNER), w['w_ssm_out'])
    qp, qr, kvo = _prep_sample(u, tab)
    seg = lambda k: kvo[:, k * KV_W:(k + 1) * KV_W]
    kc, vc, ks, vs, kw, vw = [seg(k) for k in range(6)]
    gates = seg(6)[:, 32:32 + 3 * N_HEADS].reshape(b, 3, N_HEADS).transpose(0, 2, 1)
    gates = jnp.concatenate([gates, jnp.zeros((b, N_HEADS, KV_W - 3), F32)], axis=-1)
    kpool = _compress_pool(rows2d(cache_ck), layer, *w['cmp_k'][1])
    vpool = _compress_pool(rows2d(cache_cv), layer, *w['cmp_v'][1])
    o_c, idx = _sample_cmp(page_table, _group_padded(qp), kpool, vpool, t_pos)
    new_rows = jnp.stack([ks, vs, kw, vw] + [jnp.zeros_like(ks)] * 4, axis=1)
    att = _sample_attn(idx[:, :KV_GROUPS, :TOP_N], page_table, qr.reshape(b, N_HEADS, HEAD_DIM).astype(BF16),
                       _group_padded(qr), new_rows,
                       win_k.reshape(b, -1, KV_W), win_v.reshape(b, -1, KV_W), o_c, gates,
                       cache_sk, cache_sv, t_pos, layer)
    x = _merge(x, br_cf, br_ssm, att.reshape(b, D_MODEL), u, g1, w['w_attn_out'], w['w_o'], b)
    x = _ffn(x, w['norm2_g'], sc2, sh2, g2, w['w_ffn_in'], w['w_ffn_out'], b)
    keep = min(WINDOW, t_pos + 1)
    kv4 = lambda a: a.reshape(b, 1, KV_GROUPS, HEAD_DIM)
    new_kw = jnp.concatenate([win_k, kv4(kw)], axis=1)
    new_vw = jnp.concatenate([win_v, kv4(vw)], axis=1)
    state = (kv4(kc), kv4(vc), kv4(ks), kv4(vs),
             new_kw[:, new_kw.shape[1] - keep:], new_vw[:, new_vw.shape[1] - keep:],
             jnp.concatenate([conv_st[:, 1:], glu], axis=1),
             jnp.concatenate([sconv_st[:, 1:], u[:, None, XBC_OFF:XBC_OFF + SSM_CONV_DIM]], axis=1),
             h_new.reshape(b, SSM_HEADS, SSM_HEADDIM, D_STATE))
    return x, state


def kernel(x_prompt, x_sample, c_prompt, c_sample, cache_cmp_k, cache_cmp_v, cache_slc_k, cache_slc_v, state_win_k, state_win_v, state_conv, state_ssm_conv, state_ssm, page_table, norm1_g, w_ada, b_ada, w_in, cf_dw_w, cf_dw_b, cf_ln_g, cf_ln_b, w_cf_out, ssm_conv_w, ssm_conv_b, dt_bias, a_log, d_skip, ssm_norm_g, w_ssm_out, cmp_pe_k, cmp_pe_v, cmp_w1_k, cmp_w2_k, cmp_w1_v, cmp_w2_v, w_attn_out, w_o, norm2_g, w_ffn_in, w_ffn_out, final_g):
    depth = w_in.shape[0]
    bp, lp = x_prompt.shape[0], x_prompt.shape[1]
    bs = x_sample.shape[0]
    assert bp == 1 and x_sample.shape[1] == 1
    assert lp % TK == 0 and lp // BLOCK >= TOP_N
    t_pos = page_table.shape[1] * PAGE_SIZE
    params = dict(norm1_g=norm1_g, w_in=w_in, cf_dw_w=cf_dw_w, cf_dw_b=cf_dw_b, cf_ln_g=cf_ln_g, cf_ln_b=cf_ln_b,
                  w_cf_out=w_cf_out, ssm_conv_w=ssm_conv_w, ssm_conv_b=ssm_conv_b, dt_bias=dt_bias, a_log=a_log,
                  d_skip=d_skip, ssm_norm_g=ssm_norm_g, w_ssm_out=w_ssm_out, cmp_pe_k=cmp_pe_k, cmp_pe_v=cmp_pe_v,
                  cmp_w1_k=cmp_w1_k, cmp_w2_k=cmp_w2_k, cmp_w1_v=cmp_w1_v, cmp_w2_v=cmp_w2_v,
                  w_attn_out=w_attn_out, w_o=w_o, norm2_g=norm2_g, w_ffn_in=w_ffn_in, w_ffn_out=w_ffn_out)
    n_c = bp + bs
    c_pad = (-n_c) % 8
    c_all = jnp.concatenate([c_prompt, c_sample, jnp.zeros((c_pad, D_MODEL), F32)], axis=0)
    mod = _modulation(c_all, w_ada, b_ada)
    rexp = _head_expand_matrix()
    tab_p = _rope_tables(jnp.arange(lp, dtype=jnp.int32))
    tab_s = _rope_tables(jnp.full((bs,), t_pos, jnp.int32))
    caches = tuple(jnp.transpose(c, (0, 1, 3, 4, 2)) for c in (cache_cmp_k, cache_cmp_v, cache_slc_k, cache_slc_v))

    xp = x_prompt.reshape(lp, D_MODEL)
    xs = x_sample.reshape(bs, D_MODEL)
    outs_p = [[] for _ in range(9)]
    outs_s = [[] for _ in range(9)]
    for l in range(depth):
        w = _layer_weights(l, params)
        xp, st_p = _prompt_layer(xp, mod[l, 0:bp], w, tab_p, rexp)
        states = (state_win_k[l], state_win_v[l], state_conv[l], state_ssm_conv[l], state_ssm[l])
        xs, st_s = _sample_layer(xs, mod[l, bp:bp + bs], w, tab_s, rexp, caches, l, states, page_table, t_pos)
        for k in range(9):
            outs_p[k].append(st_p[k])
            outs_s[k].append(st_s[k])
    y_prompt = _final_norm(xp, final_g[None, :], 512).reshape(bp, lp, D_MODEL)
    y_sample = _final_norm(xs, final_g[None, :], bs).reshape(bs, 1, D_MODEL)
    np_ = [jnp.stack(o) for o in outs_p]
    ns_ = [jnp.stack(o) for o in outs_s]
    return (y_prompt, y_sample, *np_, *ns_)
```

```python
import functools

import numpy as np
import jax
import jax.numpy as jnp
from jax import lax
from jax.experimental import pallas as pl
from jax.experimental.pallas import tpu as pltpu

F32 = jnp.float32
BF16 = jnp.bfloat16

D_MODEL = 1024
PAGE_SIZE = 128
CONV_K = 31
D_INNER = 2 * D_MODEL
SSM_HEADDIM = 64
SSM_HEADS = D_INNER // SSM_HEADDIM
SSM_GROUPS = 4
D_STATE = 128
SSM_CONV_K = 4
SSM_CONV_DIM = D_INNER + 2 * SSM_GROUPS * D_STATE
SSM_CHUNK = 128
N_HEADS = 16
HEAD_DIM = 64
KV_GROUPS = 2
HEADS_PER_GROUP = N_HEADS // KV_GROUPS
BLOCK = 64
TOP_N = 16
WINDOW = 512
CMP_HIDDEN = 256
ROPE_THETA = 500000.0
ROT_DIM = HEAD_DIM // 4
BIG = 1e9
NEG = -1e30
D_FF = ((8 * D_MODEL + 3 * 256 - 1) // (3 * 256)) * 256
EPS = 1e-6
KV_W = KV_GROUPS * HEAD_DIM

XBC_OFF = 0
GMIX_OFF = 3072
UCF_OFF = 6144
Z_OFF = 8192
Q_OFF = 10240
KV_OFF = 11264
SMALL_OFF = KV_OFF + 6 * KV_W
N_PACK = 12288

TQ = 256
TK = 512
V_ROWS = HEAD_DIM + 16
Q_SCALE = HEAD_DIM ** -0.5 * 1.4426950408889634
LAZY_MAX_RISE = 64.0

VMEM_LIMIT = 56 * 1024 * 1024


def _cparams(*sem):
    return pltpu.CompilerParams(dimension_semantics=sem, vmem_limit_bytes=VMEM_LIMIT)


def _dot(a, b):
    return jnp.dot(a, b, preferred_element_type=F32)


def _dot_nt(a, b):
    return lax.dot_general(a, b, (((1,), (1,)), ((), ())), preferred_element_type=F32)


def _dot_tn(a, b):
    return lax.dot_general(a, b, (((0,), (0,)), ((), ())), preferred_element_type=F32)


def _sigmoid(x):
    return jax.nn.sigmoid(x)


def _silu(x):
    return x * jax.nn.sigmoid(x)


def _softplus(x):
    return jnp.maximum(x, 0.0) + jnp.log(1.0 + jnp.exp(-jnp.abs(x)))


def _split3(x):
    hi = x.astype(BF16)
    r1 = x - hi.astype(F32)
    mid = r1.astype(BF16)
    lo = (r1 - mid.astype(F32)).astype(BF16)
    return hi, mid, lo


def _dot_exact_rhs01(x, m01):
    hi, mid, lo = _split3(x)
    return _dot(hi, m01) + _dot(mid, m01) + _dot(lo, m01)


def _dot_exact_lhs01(m01, x):
    hi, mid, lo = _split3(x)
    return _dot(m01, hi) + _dot(m01, mid) + _dot(m01, lo)


def _mod_kernel(c_ref, w_ref, b_ref, o_ref):
    c = c_ref[...]
    o_ref[0] = _dot(_silu(c).astype(BF16), w_ref[0].astype(BF16)) + b_ref[0]


def _modulation(c_all, w_ada, b_ada):
    depth = w_ada.shape[0]
    bc = c_all.shape[0]
    n = w_ada.shape[2]
    tn = 1536
    return pl.pallas_call(
        _mod_kernel,
        grid=(depth, n // tn),
        in_specs=[pl.BlockSpec((bc, D_MODEL), lambda l, j: (0, 0)),
                  pl.BlockSpec((1, D_MODEL, tn), lambda l, j: (l, 0, j)),
                  pl.BlockSpec((1, 1, tn), lambda l, j: (l, 0, j))],
        out_specs=pl.BlockSpec((1, bc, tn), lambda l, j: (l, 0, j)),
        out_shape=jax.ShapeDtypeStruct((depth, bc, n), F32),
        compiler_params=_cparams("arbitrary", "arbitrary"),
        name="modulation",
    )(c_all, w_ada, b_ada.reshape(depth, 1, n))


def _in_proj_kernel(x_ref, g_ref, sc_ref, sh_ref, w_ref, o_ref, h_ref):
    @pl.when(pl.program_id(1) == 0)
    def _():
        x = x_ref[...]
        y = x * lax.rsqrt(jnp.mean(x * x, axis=-1, keepdims=True) + EPS) * g_ref[...]
        h_ref[...] = (y * (1.0 + sc_ref[...]) + sh_ref[...]).astype(BF16)

    o_ref[...] = _dot(h_ref[...], w_ref[...])


def _in_proj(x, g, sc, sh, w, tm):
    r = x.shape[0]
    n = w.shape[1]
    tn = 1024
    mrows = tm if sc.shape[0] == r else 1
    mmap = (lambda i, j: (i, 0)) if sc.shape[0] == r else (lambda i, j: (0, 0))
    return pl.pallas_call(
        _in_proj_kernel,
        grid=(r // tm, n // tn),
        in_specs=[pl.BlockSpec((tm, D_MODEL), lambda i, j: (i, 0)),
                  pl.BlockSpec((1, D_MODEL), lambda i, j: (0, 0)),
                  pl.BlockSpec((mrows, D_MODEL), mmap),
                  pl.BlockSpec((mrows, D_MODEL), mmap),
                  pl.BlockSpec((D_MODEL, tn), lambda i, j: (0, j))],
        out_specs=pl.BlockSpec((tm, tn), lambda i, j: (i, j)),
        out_shape=jax.ShapeDtypeStruct((r, n), F32),
        scratch_shapes=[pltpu.VMEM((tm, D_MODEL), BF16)],
        compiler_params=_cparams("arbitrary", "arbitrary"),
        name="in_proj",
    )(x, g, sc, sh, w)


def _matmul_kernel(x_ref, w_ref, o_ref):
    o_ref[...] = _dot(x_ref[...].astype(BF16), w_ref[...])


def _matmul(x, w):
    r, k = x.shape
    n = w.shape[1]
    return pl.pallas_call(
        _matmul_kernel,
        grid=(1,),
        in_specs=[pl.BlockSpec((r, k), lambda i: (0, 0)), pl.BlockSpec((k, n), lambda i: (0, 0))],
        out_specs=pl.BlockSpec((r, n), lambda i: (0, 0)),
        out_shape=jax.ShapeDtypeStruct((r, n), F32),
        compiler_params=_cparams("arbitrary"),
        name="row_matmul",
    )(x, w)


CF_TL = 256
CF_RC = 32
CF_HALO = 32


def _layernorm_silu(x, g, b):
    xc = x - jnp.mean(x, axis=-1, keepdims=True)
    y = xc * lax.rsqrt(jnp.mean(xc * xc, axis=-1, keepdims=True) + EPS)
    return _silu(y * g + b)


def _conformer_kernel(u_ref, w_ref, b_ref, lg_ref, lb_ref, wo_ref, o_ref, nc_ref, bufs, cvb):
    i = pl.program_id(0)
    tl = CF_TL

    @pl.when(i == 0)
    def _():
        bufs[0, 0:CF_HALO, :] = jnp.zeros((CF_HALO, D_MODEL), F32)

    u = u_ref[...]
    bufs[0, CF_HALO:CF_HALO + tl, :] = u[:, :D_MODEL] * _sigmoid(u[:, D_MODEL:])
    for s in range(1, 8):
        bufs[s, 0:tl + 24, :] = bufs[0, s:s + tl + 24, :]

    def chunk(c, carry):
        off = pl.multiple_of(c * CF_RC, CF_RC)
        acc = jnp.broadcast_to(b_ref[...], (CF_RC, D_MODEL))
        for k in range(CONV_K):
            a, s = divmod(CF_HALO - (CONV_K - 1) + k, 8)
            acc = acc + w_ref[k:k + 1, :] * bufs[s, pl.ds(off + 8 * a, CF_RC), :]
        cvb[pl.ds(off, CF_RC), :] = _layernorm_silu(acc, lg_ref[...], lb_ref[...]).astype(BF16)
        return carry

    lax.fori_loop(0, tl // CF_RC, chunk, 0)
    tail = bufs[0, tl:tl + CF_HALO, :]
    nc_ref[...] = tail
    bufs[0, 0:CF_HALO, :] = tail
    o_ref[...] = _dot(cvb[...], wo_ref[...])


def _conformer_prompt(u, dw_w, dw_b, ln_g, ln_b, w_out):
    l = u.shape[0]
    tl = CF_TL
    wpad = jnp.concatenate([dw_w, jnp.zeros((1, D_MODEL), F32)], axis=0)
    return pl.pallas_call(
        _conformer_kernel,
        grid=(l // tl,),
        in_specs=[pl.BlockSpec((tl, 2 * D_MODEL), lambda i: (i, UCF_OFF // (2 * D_MODEL))),
                  pl.BlockSpec((CONV_K + 1, D_MODEL), lambda i: (0, 0)),
                  pl.BlockSpec((1, D_MODEL), lambda i: (0, 0)),
                  pl.BlockSpec((1, D_MODEL), lambda i: (0, 0)),
                  pl.BlockSpec((1, D_MODEL), lambda i: (0, 0)),
                  pl.BlockSpec((D_MODEL, D_MODEL), lambda i: (0, 0))],
        out_specs=[pl.BlockSpec((tl, D_MODEL), lambda i: (i, 0)),
                   pl.BlockSpec((CF_HALO, D_MODEL), lambda i: (0, 0))],
        out_shape=[jax.ShapeDtypeStruct((l, D_MODEL), F32),
                   jax.ShapeDtypeStruct((CF_HALO, D_MODEL), F32)],
        scratch_shapes=[pltpu.VMEM((8, tl + CF_HALO, D_MODEL), F32),
                        pltpu.VMEM((tl, D_MODEL), BF16)],
        compiler_params=_cparams("arbitrary"),
        name="conformer_prompt",
    )(u, wpad, dw_b, ln_g, ln_b, w_out)


def _gated_group_norm(y, z, g):
    y = y * _silu(z)
    gw = D_INNER // SSM_GROUPS
    parts = []
    for k in range(SSM_GROUPS):
        yg = y[:, k * gw:(k + 1) * gw]
        parts.append(yg * lax.rsqrt(jnp.mean(yg * yg, axis=-1, keepdims=True) + EPS))
    return jnp.concatenate(parts, axis=1) * g


def _ssd_kernel(xbc_ref, z_ref, sm_ref, cw_ref, cb_ref, dtb_ref, alog_ref, dsk_ref, ng_ref, rexp_ref,
                wo_ref, o_ref, hout_ref, sc_ref, cbuf, hst):
    i = pl.program_id(0)
    q = SSM_CHUNK

    @pl.when(i == 0)
    def _():
        cbuf[0:8, :] = jnp.zeros((8, SSM_CONV_DIM), F32)
        hst[...] = jnp.zeros(hst.shape, F32)

    x = xbc_ref[...]
    cbuf[8:8 + q, :] = x
    conv = cb_ref[...] + cw_ref[3:4, :] * x
    for k in range(SSM_CONV_K - 1):
        conv = conv + cw_ref[k:k + 1, :] * cbuf[5 + k:5 + k + q, :]
    tail = cbuf[q:q + 8, :]
    sc_ref[...] = tail
    cbuf[0:8, :] = tail
    xa = _silu(conv)
    xs = xa[:, :D_INNER]
    bm = xa[:, D_INNER:D_INNER + SSM_GROUPS * D_STATE]
    cm = xa[:, D_INNER + SSM_GROUPS * D_STATE:]

    dt = _softplus(sm_ref[...] + dtb_ref[...])
    a = dt * (-jnp.exp(alog_ref[...]))
    row = lax.broadcasted_iota(jnp.int32, (q, q), 0)
    col = lax.broadcasted_iota(jnp.int32, (q, q), 1)
    tri = row >= col
    a_cum = _dot_exact_lhs01(tri.astype(BF16), a)
    a_cum_t = a_cum.T
    a_last = a_cum[q - 1:q, :]
    rexp = rexp_ref[...]
    dtx = _dot_exact_rhs01(dt, rexp)
    eax = _dot_exact_rhs01(jnp.exp(a_cum), rexp)
    decx = _dot_exact_rhs01(jnp.exp(a_last - a_cum), rexp)
    xin = xs * dtx
    xdec = (xin * decx).astype(BF16)
    xin_b = xin.astype(BF16)
    cd_col = jnp.broadcast_to(jnp.exp(a_last), (q, q)).T
    lane_lo = lax.broadcasted_iota(jnp.int32, (q, 2 * SSM_HEADDIM), 1) < SSM_HEADDIM

    hpg = SSM_HEADS // SSM_GROUPS
    gw = hpg * SSM_HEADDIM
    y_groups = []
    for g in range(SSM_GROUPS):
        cg = cm[:, g * D_STATE:(g + 1) * D_STATE].astype(BF16)
        bg = bm[:, g * D_STATE:(g + 1) * D_STATE].astype(BF16)
        cb = _dot_nt(cg, bg)
        hprev = hst[g * gw:(g + 1) * gw, :]
        y_off = _dot_nt(cg, hprev.astype(BF16)) * eax[:, g * gw:(g + 1) * gw]
        st = _dot_tn(xdec[:, g * gw:(g + 1) * gw], bg)
        pair_parts = []
        for pr in range(hpg // 2):
            xp = xin_b[:, g * gw + pr * 128:g * gw + (pr + 1) * 128]
            yp = None
            for sub in range(2):
                h = g * hpg + pr * 2 + sub
                seg = a_cum[:, h:h + 1] - a_cum_t[h:h + 1, :]
                lm = jnp.exp(jnp.where(tri, seg, -jnp.inf))
                mh = (cb * lm).astype(BF16)
                xm = jnp.where(lane_lo if sub == 0 else jnp.logical_not(lane_lo), xp, jnp.zeros_like(xp))
                d = _dot(mh, xm)
                yp = d if yp is None else yp + d
                r0 = h * SSM_HEADDIM
                lo = (pr * 2 + sub) * SSM_HEADDIM
                hst[r0:r0 + SSM_HEADDIM, :] = (hprev[lo:lo + SSM_HEADDIM, :] * cd_col[h:h + 1, :]
                                              + st[lo:lo + SSM_HEADDIM, :])
            pair_parts.append(yp)
        y_groups.append(jnp.concatenate(pair_parts, axis=1) + y_off)
    y = jnp.concatenate(y_groups, axis=1) + dsk_ref[...] * xs
    yn = _gated_group_norm(y, z_ref[...], ng_ref[...])
    o_ref[...] = _dot(yn.astype(BF16), wo_ref[...])

    @pl.when(i == pl.num_programs(0) - 1)
    def _():
        hout_ref[...] = hst[...]


def _pad_lanes(v, n=128):
    return jnp.concatenate([v, jnp.zeros((v.shape[0], n - v.shape[1]), v.dtype)], axis=1)


def _ssd_prompt(u, conv_w, conv_b, dt_bias, a_log, dskip_x, norm_g, rexp, w_out):
    l = u.shape[0]
    q = SSM_CHUNK
    cw = jnp.concatenate([conv_w, jnp.zeros((4, SSM_CONV_DIM), F32)], axis=0)
    const = lambda i: (0, 0)
    return pl.pallas_call(
        _ssd_kernel,
        grid=(l // q,),
        in_specs=[pl.BlockSpec((q, SSM_CONV_DIM), lambda i: (i, XBC_OFF // SSM_CONV_DIM)),
                  pl.BlockSpec((q, D_INNER), lambda i: (i, Z_OFF // D_INNER)),
                  pl.BlockSpec((q, 128), lambda i: (i, SMALL_OFF // 128)),
                  pl.BlockSpec((8, SSM_CONV_DIM), const),
                  pl.BlockSpec((1, SSM_CONV_DIM), const),
                  pl.BlockSpec((1, 128), const),
                  pl.BlockSpec((1, 128), const),
                  pl.BlockSpec((1, D_INNER), const),
                  pl.BlockSpec((1, D_INNER), const),
                  pl.BlockSpec((128, D_INNER), const),
                  pl.BlockSpec((D_INNER, D_MODEL), const)],
        out_specs=[pl.BlockSpec((q, D_MODEL), lambda i: (i, 0)),
                   pl.BlockSpec((D_INNER, D_STATE), const),
                   pl.BlockSpec((8, SSM_CONV_DIM), const)],
        out_shape=[jax.ShapeDtypeStruct((l, D_MODEL), F32),
                   jax.ShapeDtypeStruct((D_INNER, D_STATE), F32),
                   jax.ShapeDtypeStruct((8, SSM_CONV_DIM), F32)],
        scratch_shapes=[pltpu.VMEM((q + 8, SSM_CONV_DIM), F32),
                        pltpu.VMEM((D_INNER, D_STATE), F32)],
        compiler_params=_cparams("arbitrary"),
        name="ssd_prompt",
    )(u, u, u, cw, conv_b, _pad_lanes(dt_bias), _pad_lanes(a_log), dskip_x, norm_g, rexp, w_out)


def _rope(x, c, s1, s2):
    n = x.shape[1]
    return x * c + pltpu.roll(x, ROT_DIM // 2, 1) * s1 + pltpu.roll(x, n - ROT_DIM // 2, 1) * s2


def _aug_vt(v):
    vt = v.T.astype(BF16)
    ones = jnp.ones((V_ROWS - HEAD_DIM, v.shape[0]), BF16)
    return [jnp.concatenate([vt[g * HEAD_DIM:(g + 1) * HEAD_DIM, :], ones], axis=0) for g in range(KV_GROUPS)]


def _prep_kernel(q_ref, kv_ref, tab_ref, kc_ref, vc_ref, ks_ref, vs_ref, kw_ref, vw_ref,
                 qpt_ref, qrt_ref, ksb_ref, vst_ref, kwb_ref, vwt_ref, gt_ref):
    i = pl.program_id(0)
    tl = q_ref.shape[0]
    c = tab_ref[0]
    s1 = tab_ref[1]
    s2 = tab_ref[2]
    reps = D_MODEL // 128
    q = q_ref[...] * Q_SCALE
    qr = _rope(q, jnp.concatenate([c] * reps, axis=1), jnp.concatenate([s1] * reps, axis=1),
               jnp.concatenate([s2] * reps, axis=1))
    qpt_ref[...] = q.T.astype(BF16)
    qrt_ref[...] = qr.T.astype(BF16)
    kv = kv_ref[...]
    kc_ref[...] = kv[:, 0:KV_W]
    vc_ref[...] = kv[:, KV_W:2 * KV_W]
    ks = _rope(kv[:, 2 * KV_W:3 * KV_W], c, s1, s2)
    ks_ref[...] = ks
    row = lax.broadcasted_iota(jnp.int32, (tl, KV_W), 0)
    lane = lax.broadcasted_iota(jnp.int32, (tl, KV_W), 1)
    blk_in_tile = lax.shift_right_logical((i * tl) % TK + row, 6)
    ksb_ref[...] = jnp.concatenate([ks, jnp.where(lane == blk_in_tile, 1.0, 0.0)], axis=1).astype(BF16)
    vs = kv[:, 3 * KV_W:4 * KV_W]
    vs_ref[...] = vs
    kw = _rope(kv[:, 4 * KV_W:5 * KV_W], c, s1, s2)
    kw_ref[...] = kw
    kwb_ref[...] = kw.astype(BF16)
    vw = kv[:, 5 * KV_W:6 * KV_W]
    vw_ref[...] = vw
    for g, (a, b) in enumerate(zip(_aug_vt(vs), _aug_vt(vw))):
        vst_ref[0, g] = a
        vwt_ref[0, g] = b
    gt_ref[...] = _sigmoid(kv[:, 6 * KV_W:7 * KV_W]).T


def _prep_prompt(u, tab):
    l = u.shape[0]
    tl = TQ
    nt = l // tl
    row = lambda i: (i, 0)
    f32s = jax.ShapeDtypeStruct((l, KV_W), F32)
    vspec = pl.BlockSpec((1, KV_GROUPS, V_ROWS, tl), lambda i: (i, 0, 0, 0))
    vshape = jax.ShapeDtypeStruct((nt, KV_GROUPS, V_ROWS, tl), BF16)
    return pl.pallas_call(
        _prep_kernel,
        grid=(nt,),
        in_specs=[pl.BlockSpec((tl, D_MODEL), lambda i: (i, Q_OFF // D_MODEL)),
                  pl.BlockSpec((tl, D_MODEL), lambda i: (i, KV_OFF // D_MODEL)),
                  pl.BlockSpec((3, tl, 128), lambda i: (0, i, 0))],
        out_specs=[pl.BlockSpec((tl, KV_W), row)] * 6 + [
            pl.BlockSpec((D_MODEL, tl), lambda i: (0, i)),
            pl.BlockSpec((D_MODEL, tl), lambda i: (0, i)),
            pl.BlockSpec((tl, 2 * KV_W), row),
            vspec,
            pl.BlockSpec((tl, KV_W), row),
            vspec,
            pl.BlockSpec((128, tl), lambda i: (0, i))],
        out_shape=[f32s] * 6 + [
            jax.ShapeDtypeStruct((D_MODEL, l), BF16),
            jax.ShapeDtypeStruct((D_MODEL, l), BF16),
            jax.ShapeDtypeStruct((l, 2 * KV_W), BF16),
            vshape,
            jax.ShapeDtypeStruct((l, KV_W), BF16),
            vshape,
            jax.ShapeDtypeStruct((128, l), F32)],
        compiler_params=_cparams("arbitrary"),
        name="prep_prompt",
    )(u, u, tab)


def _prep_sample_kernel(q_ref, kv_ref, tab_ref, qp_ref, qr_ref, kvo_ref):
    c = tab_ref[0]
    s1 = tab_ref[1]
    s2 = tab_ref[2]
    reps = D_MODEL // 128
    q = q_ref[...] * Q_SCALE
    qp_ref[...] = q
    qr_ref[...] = _rope(q, jnp.concatenate([c] * reps, axis=1), jnp.concatenate([s1] * reps, axis=1),
                        jnp.concatenate([s2] * reps, axis=1))
    kv = kv_ref[...]
    one = jnp.ones_like(c)
    zero = jnp.zeros_like(c)
    ckv = jnp.concatenate([one, one, c, one, c, one, one, one], axis=1)
    s1kv = jnp.concatenate([zero, zero, s1, zero, s1, zero, zero, zero], axis=1)
    s2kv = jnp.concatenate([zero, zero, s2, zero, s2, zero, zero, zero], axis=1)
    kvr = _rope(kv, ckv, s1kv, s2kv)
    lane = lax.broadcasted_iota(jnp.int32, kv.shape, 1)
    kvo_ref[...] = jnp.where((lane >= 6 * KV_W) & (lane < 7 * KV_W), _sigmoid(kv), kvr)


def _prep_sample(u, tab):
    b = u.shape[0]
    full = lambda i: (0, 0)
    return pl.pallas_call(
        _prep_sample_kernel,
        grid=(1,),
        in_specs=[pl.BlockSpec((b, D_MODEL), lambda i: (0, Q_OFF // D_MODEL)),
                  pl.BlockSpec((b, D_MODEL), lambda i: (0, KV_OFF // D_MODEL)),
                  pl.BlockSpec((3, b, 128), lambda i: (0, 0, 0))],
        out_specs=[pl.BlockSpec((b, D_MODEL), full)] * 3,
        out_shape=[jax.ShapeDtypeStruct((b, D_MODEL), F32)] * 3,
        compiler_params=_cparams("arbitrary"),
        name="prep_sample",
    )(u, u, tab)


def _compress_kernel(x_ref, pe_ref, w1_ref, w2_ref, o_ref):
    x = (x_ref[...] + pe_ref[...]).astype(BF16)
    hid = _silu(_dot(x, w1_ref[...]))
    o_ref[...] = _dot(hid.astype(BF16), w2_ref[...])


def _compress(x2d, pe2, w1e, w2e):
    m = x2d.shape[0]
    tm = 256 if m % 256 == 0 else m
    kdim = BLOCK * KV_W
    return pl.pallas_call(
        _compress_kernel,
        grid=(m // tm,),
        in_specs=[pl.BlockSpec((tm, kdim), lambda i: (i, 0)),
                  pl.BlockSpec((1, kdim), lambda i: (0, 0)),
                  pl.BlockSpec((kdim, KV_GROUPS * CMP_HIDDEN), lambda i: (0, 0)),
                  pl.BlockSpec((KV_GROUPS * CMP_HIDDEN, KV_W), lambda i: (0, 0))],
        out_specs=pl.BlockSpec((tm, KV_W), lambda i: (i, 0)),
        out_shape=jax.ShapeDtypeStruct((m, KV_W), F32),
        compiler_params=_cparams("arbitrary"),
        name="compress",
    )(x2d, pe2, w1e, w2e)


def _compress_pool_kernel(x_ref, pe_ref, w1_ref, w2_ref, o_ref, xs, *, tp):
    stride = KV_GROUPS * HEAD_DIM
    for g in range(KV_GROUPS):
        for d in range(HEAD_DIM):
            rows = x_ref[pl.ds(g * HEAD_DIM + d, tp, stride=stride), :]
            xs[g * tp:(g + 1) * tp, d * PAGE_SIZE:(d + 1) * PAGE_SIZE] = (rows + pe_ref[d:d + 1, :]).astype(BF16)
    hid = _silu(_dot(xs[...], w1_ref[...]))
    o_ref[0] = _dot(hid.astype(BF16), w2_ref[...])


def _compress_pool(cache_t, layer, pe_t, w1t, w2e):
    n_pool = cache_t.shape[1] // (KV_GROUPS * HEAD_DIM)
    tp = 128 if n_pool % 128 == 0 else n_pool
    ns = n_pool // tp
    kdim = HEAD_DIM * PAGE_SIZE
    halves = PAGE_SIZE // BLOCK
    out = pl.pallas_call(
        functools.partial(_compress_pool_kernel, tp=tp),
        grid=(ns,),
        in_specs=[pl.BlockSpec((None, tp * KV_GROUPS * HEAD_DIM, PAGE_SIZE), lambda i: (layer, i, 0)),
                  pl.BlockSpec((HEAD_DIM, PAGE_SIZE), lambda i: (0, 0)),
                  pl.BlockSpec((kdim, halves * CMP_HIDDEN), lambda i: (0, 0)),
                  pl.BlockSpec((halves * CMP_HIDDEN, halves * HEAD_DIM), lambda i: (0, 0))],
        out_specs=pl.BlockSpec((1, KV_GROUPS * tp, halves * HEAD_DIM), lambda i: (i, 0, 0)),
        out_shape=jax.ShapeDtypeStruct((ns, KV_GROUPS * tp, halves * HEAD_DIM), F32),
        scratch_shapes=[pltpu.VMEM((KV_GROUPS * tp, kdim), BF16)],
        compiler_params=_cparams("arbitrary"),
        name="compress_pool",
    )(cache_t, pe_t, w1t, w2e)
    out = out.reshape(ns, KV_GROUPS, tp, halves, HEAD_DIM).transpose(0, 2, 3, 1, 4)
    return out.reshape(n_pool * halves, KV_W)


def _select_topn(score, index, axis, n_index):
    work = score
    for _ in range(TOP_N):
        m = jnp.max(work, axis=axis, keepdims=True)
        first = jnp.min(jnp.where(work == m, index, n_index), axis=axis, keepdims=True)
        work = jnp.where(index == first, -jnp.inf, work)
    return jnp.where(work == -jnp.inf, 1.0, 0.0)


def _attn_kernel(qpt_ref, qrt_ref, gt_ref, kcmp_ref, vcmp_ref, ks_ref, vst_ref,
                 kwa_ref, kwb_ref, kwc_ref, vwa_ref, vwb_ref, vwc_ref, o_ref,
                 rhs, bias16, m_sc, acc, outacc):
    i = pl.program_id(0)
    nb = kcmp_ref.shape[0]
    bpt = TK // BLOCK
    hw = HEADS_PER_GROUP * TQ
    t0 = i * TQ
    t_row = t0 + lax.broadcasted_iota(jnp.int32, (1, TQ), 1)
    zeros_half = jnp.zeros((HEAD_DIM, TQ), BF16)
    wide = lambda x: jnp.concatenate([x] * HEADS_PER_GROUP, axis=1)

    def group_queries(src_ref, g):
        cols = []
        for r in range(HEADS_PER_GROUP):
            h = g * HEADS_PER_GROUP + r
            qh = src_ref[h * HEAD_DIM:(h + 1) * HEAD_DIM, :]
            cols.append(jnp.concatenate([qh, zeros_half] if g == 0 else [zeros_half, qh], axis=0))
        return jnp.concatenate(cols, axis=1)

    def gate_row(branch, g):
        base = 32 + branch * N_HEADS + g * HEADS_PER_GROUP
        return jnp.concatenate([gt_ref[base + r:base + r + 1, :] for r in range(HEADS_PER_GROUP)], axis=1)

    @pl.when(i == 0)
    def _():
        rhs[...] = jnp.zeros(rhs.shape, BF16)

    kcmp = kcmp_ref[...].astype(BF16)
    vcmp_t = vcmp_ref[...].T.astype(BF16)
    jrow = lax.broadcasted_iota(jnp.int32, (nb, TQ), 0)
    mask_c1 = (jrow + 1) * BLOCK - 1 <= t_row
    mask_c = wide(mask_c1.astype(jnp.int32)) > 0
    cur = lax.shift_right_logical(t_row, 6)
    valid = jrow <= cur
    forced = valid & ((jrow == 0) | (jrow == cur) | (jrow == cur - 1))
    for g in range(KV_GROUPS):
        s = jnp.where(mask_c, _dot(kcmp, group_queries(qpt_ref, g)), NEG)
        m = jnp.max(s, axis=0, keepdims=True)
        e = jnp.where(mask_c, jnp.exp2(s - m), 0.0)
        p = e * (1.0 / jnp.maximum(jnp.sum(e, axis=0, keepdims=True), 1e-30))
        imp = p[:, 0:TQ]
        for r in range(1, HEADS_PER_GROUP):
            imp = imp + p[:, r * TQ:(r + 1) * TQ]
        o_c = _dot(vcmp_t, p.astype(BF16))[g * HEAD_DIM:(g + 1) * HEAD_DIM, :]
        outacc[g] = gate_row(0, g) * o_c
        rhs[g, 0:KV_W, :] = group_queries(qrt_ref, g)
        score = jnp.where(forced, BIG, jnp.where(valid, imp, -BIG))
        sel = _select_topn(score, jrow, 0, nb)
        bsel = jnp.where((sel > 0.5) & (score > -0.5 * BIG), 0.0, NEG)
        for c in range(nb // bpt):
            bias16[g, c] = jnp.concatenate([bsel[c * bpt:(c + 1) * bpt, :], jnp.zeros((16 - bpt, TQ), F32)],
                                           axis=0).astype(BF16)

    m_sc[...] = jnp.full(m_sc.shape, NEG, F32)
    acc[...] = jnp.zeros(acc.shape, F32)
    n_kt = (t0 + TQ + TK - 1) // TK

    def key_tile(kt, diagonal, lazy):
        k0 = pl.multiple_of(kt * TK, TK)
        k_aug = ks_ref[pl.ds(k0, TK), :]
        if diagonal:
            kpos = k0 + lax.broadcasted_iota(jnp.int32, (TK, TQ), 0)
            causal = wide(jnp.where(kpos > t_row, NEG, 0.0))

        def scores(g):
            s = _dot(k_aug, rhs[g])
            return s + causal if diagonal else s

        def values(g):
            return jnp.concatenate([vst_ref[kt * (TK // TQ) + c, g] for c in range(TK // TQ)], axis=1)

        def exact_update(g):
            s = scores(g)
            m_old8 = m_sc[g]
            m_new8 = jnp.maximum(m_old8, jnp.max(s, axis=0, keepdims=True))
            m_new = m_new8[0:1, :]
            alpha = jnp.exp2(m_old8[0:1, :] - m_new)
            p = jnp.exp2(s - m_new).astype(BF16)
            acc[g] = alpha * acc[g] + _dot(values(g), p)
            m_sc[g] = m_new8

        for g in range(KV_GROUPS):
            rhs[g, KV_W:KV_W + 16, :] = wide(bias16[g, kt])
        if not lazy:
            for g in range(KV_GROUPS):
                exact_update(g)
            return
        m_old8, m_tile, pv, rise = [], [], [], []
        for g in range(KV_GROUPS):
            s = scores(g)
            m_old8.append(m_sc[g])
            p = jnp.exp2(s - m_old8[g][0:1, :]).astype(BF16)
            m_tile.append(jnp.max(s, axis=0, keepdims=True))
            pv.append(_dot(values(g), p))
            rise.append(jnp.max(m_tile[g] - m_old8[g][0:1, :]))
        safe = functools.reduce(jnp.maximum, rise) <= LAZY_MAX_RISE

        @pl.when(safe)
        def _():
            for g in range(KV_GROUPS):
                m_new8 = jnp.maximum(m_old8[g], m_tile[g])
                alpha = jnp.exp2(m_old8[g][0:1, :] - m_new8[0:1, :])
                acc[g] = alpha * (acc[g] + pv[g])
                m_sc[g] = m_new8

        @pl.when(jnp.logical_not(safe))
        def _():
            for g in range(KV_GROUPS):
                exact_update(g)

    def lazy_tile(kt, carry):
        key_tile(kt, False, True)
        return carry

    key_tile(0, True, False)
    lax.fori_loop(1, n_kt - 1, lazy_tile, 0)

    @pl.when(n_kt > 1)
    def _():
        key_tile(n_kt - 1, True, True)

    kw = jnp.concatenate([kwa_ref[...], kwb_ref[...], kwc_ref[...]], axis=0)
    wpos = (i - 2) * TQ + lax.broadcasted_iota(jnp.int32, (3 * TQ, TQ), 0)
    bias_w = wide(jnp.where((wpos >= 0) & (wpos <= t_row) & (t_row - wpos < WINDOW), 0.0, NEG))
    for g in range(KV_GROUPS):
        s = _dot(kw, rhs[g, 0:KV_W, :]) + bias_w
        m = jnp.max(s, axis=0, keepdims=True)
        p = jnp.exp2(s - m).astype(BF16)
        pv = (_dot(vwa_ref[0, g], p[0:TQ, :]) + _dot(vwb_ref[0, g], p[TQ:2 * TQ, :])
              + _dot(vwc_ref[0, g], p[2 * TQ:, :]))
        o_w = pv[0:HEAD_DIM, :] * (1.0 / pv[HEAD_DIM:HEAD_DIM + 1, :])
        a = acc[g]
        o_s = a[0:HEAD_DIM, :] * (1.0 / a[HEAD_DIM:HEAD_DIM + 1, :])
        o_g = outacc[g] + gate_row(1, g) * o_s + gate_row(2, g) * o_w
        for r in range(0, HEADS_PER_GROUP, 2):
            h = g * HEADS_PER_GROUP + r
            pair = jnp.concatenate([o_g[:, r * TQ:(r + 1) * TQ], o_g[:, (r + 1) * TQ:(r + 2) * TQ]], axis=0)
            o_ref[:, h * HEAD_DIM:(h + 2) * HEAD_DIM] = pair.T


def _attention_prompt(qpt, qrt, gt, kcmp, vcmp, ksb, vst, kwb, vwt):
    l = qpt.shape[1]
    nb = kcmp.shape[0]
    nt = l // TQ
    const2 = lambda i: (0, 0)
    blk = lambda d: (lambda i: (jnp.maximum(i - d, 0), 0))
    blk4 = lambda d: (lambda i: (jnp.maximum(i - d, 0), 0, 0, 0))
    vblk = lambda d: pl.BlockSpec((1, KV_GROUPS, V_ROWS, TQ), blk4(d))
    return pl.pallas_call(
        _attn_kernel,
        grid=(nt,),
        in_specs=[pl.BlockSpec((D_MODEL, TQ), lambda i: (0, i)),
                  pl.BlockSpec((D_MODEL, TQ), lambda i: (0, i)),
                  pl.BlockSpec((128, TQ), lambda i: (0, i)),
                  pl.BlockSpec((nb, KV_W), const2),
                  pl.BlockSpec((nb, KV_W), const2),
                  pl.BlockSpec((l, 2 * KV_W), const2),
                  pl.BlockSpec((nt, KV_GROUPS, V_ROWS, TQ), lambda i: (0, 0, 0, 0)),
                  pl.BlockSpec((TQ, KV_W), blk(2)),
                  pl.BlockSpec((TQ, KV_W), blk(1)),
                  pl.BlockSpec((TQ, KV_W), blk(0)),
                  vblk(2), vblk(1), vblk(0)],
        out_specs=pl.BlockSpec((TQ, D_MODEL), lambda i: (i, 0)),
        out_shape=jax.ShapeDtypeStruct((l, D_MODEL), F32),
        scratch_shapes=[pltpu.VMEM((KV_GROUPS, 2 * KV_W, HEADS_PER_GROUP * TQ), BF16),
                        pltpu.VMEM((KV_GROUPS, nb // (TK // BLOCK), 16, TQ), BF16),
                        pltpu.VMEM((KV_GROUPS, 8, HEADS_PER_GROUP * TQ), F32),
                        pltpu.VMEM((KV_GROUPS, V_ROWS, HEADS_PER_GROUP * TQ), F32),
                        pltpu.VMEM((KV_GROUPS, HEAD_DIM, HEADS_PER_GROUP * TQ), F32)],
        compiler_params=_cparams("arbitrary"),
        name="attention_prompt",
    )(qpt, qrt, gt, kcmp, vcmp, ksb, vst, kwb, kwb, kwb, vwt, vwt, vwt)


def _merge_kernel(x_ref, cf_ref, ssm_ref, att_ref, gm_ref, g1_ref, wa_ref, wo_ref, o_ref):
    gm = _sigmoid(gm_ref[...])
    br_att = _dot(att_ref[...].astype(BF16), wa_ref[...])
    mixed = (gm[:, :D_MODEL] * cf_ref[...] + gm[:, D_MODEL:2 * D_MODEL] * ssm_ref[...]
             + gm[:, 2 * D_MODEL:] * br_att)
    o_ref[...] = x_ref[...] + g1_ref[...] * _dot(mixed.astype(BF16), wo_ref[...])


def _merge(x, br_cf, br_ssm, att, u, g1, w_attn, w_o, tm):
    r = x.shape[0]
    mrows = tm if g1.shape[0] == r else 1
    mmap = (lambda i: (i, 0)) if g1.shape[0] == r else (lambda i: (0, 0))
    row = lambda i: (i, 0)
    const = lambda i: (0, 0)
    return pl.pallas_call(
        _merge_kernel,
        grid=(r // tm,),
        in_specs=[pl.BlockSpec((tm, D_MODEL), row)] * 4 + [
            pl.BlockSpec((tm, 3 * D_MODEL), lambda i: (i, GMIX_OFF // (3 * D_MODEL))),
            pl.BlockSpec((mrows, D_MODEL), mmap),
            pl.BlockSpec((D_MODEL, D_MODEL), const),
            pl.BlockSpec((D_MODEL, D_MODEL), const)],
        out_specs=pl.BlockSpec((tm, D_MODEL), row),
        out_shape=jax.ShapeDtypeStruct((r, D_MODEL), F32),
        compiler_params=_cparams("arbitrary"),
        name="merge",
    )(x, br_cf, br_ssm, att, u, g1, w_attn, w_o)


FF_T = D_FF // 2


def _ffn_kernel(x_ref, g_ref, sc_ref, sh_ref, g2_ref, wg_ref, wu_ref, wo_ref, o_ref, h_ref, acc_ref):
    j = pl.program_id(1)

    @pl.when(j == 0)
    def _():
        x = x_ref[...]
        y = x * lax.rsqrt(jnp.mean(x * x, axis=-1, keepdims=True) + EPS) * g_ref[...]
        h_ref[...] = (y * (1.0 + sc_ref[...]) + sh_ref[...]).astype(BF16)
        acc_ref[...] = jnp.zeros(acc_ref.shape, F32)

    h = h_ref[...]
    act = _silu(_dot(h, wg_ref[...])) * _dot(h, wu_ref[...])
    acc_ref[...] += _dot(act.astype(BF16), wo_ref[...])

    @pl.when(j == pl.num_programs(1) - 1)
    def _():
        o_ref[...] = x_ref[...] + g2_ref[...] * acc_ref[...]


def _ffn(x, g, sc, sh, g2, w_in, w_out, tm):
    r = x.shape[0]
    nj = D_FF // FF_T
    mrows = tm if sc.shape[0] == r else 1
    mmap = (lambda i, j: (i, 0)) if sc.shape[0] == r else (lambda i, j: (0, 0))
    row = lambda i, j: (i, 0)
    return pl.pallas_call(
        _ffn_kernel,
        grid=(r // tm, nj),
        in_specs=[pl.BlockSpec((tm, D_MODEL), row),
                  pl.BlockSpec((1, D_MODEL), lambda i, j: (0, 0)),
                  pl.BlockSpec((mrows, D_MODEL), mmap),
                  pl.BlockSpec((mrows, D_MODEL), mmap),
                  pl.BlockSpec((mrows, D_MODEL), mmap),
                  pl.BlockSpec((D_MODEL, FF_T), lambda i, j: (0, j)),
                  pl.BlockSpec((D_MODEL, FF_T), lambda i, j: (0, nj + j)),
                  pl.BlockSpec((FF_T, D_MODEL), lambda i, j: (j, 0))],
        out_specs=pl.BlockSpec((tm, D_MODEL), row),
        out_shape=jax.ShapeDtypeStruct((r, D_MODEL), F32),
        scratch_shapes=[pltpu.VMEM((tm, D_MODEL), BF16), pltpu.VMEM((tm, D_MODEL), F32)],
        compiler_params=_cparams("arbitrary", "arbitrary"),
        name="ffn",
    )(x, g, sc, sh, g2, w_in, w_in, w_out)


def _final_norm_kernel(x_ref, g_ref, o_ref):
    x = x_ref[...]
    o_ref[...] = x * lax.rsqrt(jnp.mean(x * x, axis=-1, keepdims=True) + EPS) * g_ref[...]


def _final_norm(x, g, tm):
    r = x.shape[0]
    return pl.pallas_call(
        _final_norm_kernel,
        grid=(r // tm,),
        in_specs=[pl.BlockSpec((tm, D_MODEL), lambda i: (i, 0)), pl.BlockSpec((1, D_MODEL), lambda i: (0, 0))],
        out_specs=pl.BlockSpec((tm, D_MODEL), lambda i: (i, 0)),
        out_shape=jax.ShapeDtypeStruct((r, D_MODEL), F32),
        compiler_params=_cparams("arbitrary"),
        name="final_norm",
    )(x, g)


def _sample_mix_kernel(ucf_ref, z_ref, xbc_ref, sm_ref, cst_ref, sst_ref, h0_ref,
                       dww_ref, dwb_ref, lg_ref, lb_ref, cw_ref, cb_ref, dtb_ref, alog_ref,
                       dsk_ref, ng_ref, rexp_ref,
                       cv_ref, yn_ref, glu_ref, hnew_ref):
    u = ucf_ref[...]
    glu = u[:, :D_MODEL] * _sigmoid(u[:, D_MODEL:])
    glu_ref[...] = glu
    conv = (dwb_ref[...] + dww_ref[CONV_K - 1:CONV_K, :] * glu
            + jnp.sum(dww_ref[0:CONV_K - 1, :] * cst_ref[...], axis=0, keepdims=True))
    cv_ref[...] = _layernorm_silu(conv, lg_ref[...], lb_ref[...])

    xbc = xbc_ref[...]
    conv = (cb_ref[...] + cw_ref[SSM_CONV_K - 1:SSM_CONV_K, :] * xbc
            + jnp.sum(cw_ref[0:SSM_CONV_K - 1, :] * sst_ref[...], axis=0, keepdims=True))
    xa = _silu(conv)
    xs = xa[:, :D_INNER]
    bm = xa[:, D_INNER:D_INNER + SSM_GROUPS * D_STATE]
    cm = xa[:, D_INNER + SSM_GROUPS * D_STATE:]
    dt = _softplus(sm_ref[...] + dtb_ref[...])
    dec = jnp.exp(dt * (-jnp.exp(alog_ref[...])))
    dtx = _dot_exact_rhs01(jnp.broadcast_to(dt, (8, 128)), rexp_ref[...])[0:1, :]
    xin = xs * dtx
    dec_col = jnp.broadcast_to(dec, (128, 128)).T
    nblk = D_INNER // 128
    xrows = jnp.concatenate([xin[:, a * 128:(a + 1) * 128] for a in range(nblk)]
                            + [jnp.zeros((128 - nblk, 128), F32)], axis=0)
    x_col = xrows.T
    blocks_per_group = (D_INNER // SSM_GROUPS) // 128
    for a in range(nblk):
        g = a // blocks_per_group
        bg = bm[:, g * D_STATE:(g + 1) * D_STATE]
        dcol = jnp.concatenate([jnp.broadcast_to(dec_col[2 * a:2 * a + 1, :], (SSM_HEADDIM, 128)),
                                jnp.broadcast_to(dec_col[2 * a + 1:2 * a + 2, :], (SSM_HEADDIM, 128))], axis=0)
        hnew_ref[a * 128:(a + 1) * 128, :] = h0_ref[a * 128:(a + 1) * 128, :] * dcol + x_col[:, a:a + 1] * bg
    c8 = jnp.concatenate([cm[:, g * D_STATE:(g + 1) * D_STATE] for g in range(SSM_GROUPS)]
                         + [jnp.zeros((8 - SSM_GROUPS, D_STATE), F32)], axis=0)
    yall = _dot_nt(c8.astype(BF16), hnew_ref[...].astype(BF16))
    lane = lax.broadcasted_iota(jnp.int32, (1, D_INNER), 1)
    gw = D_INNER // SSM_GROUPS
    y = jnp.zeros((1, D_INNER), F32)
    for g in range(SSM_GROUPS):
        y = y + jnp.where((lane >= g * gw) & (lane < (g + 1) * gw), yall[g:g + 1, :], 0.0)
    y = y + dsk_ref[...] * xs
    yn_ref[...] = _gated_group_norm(y, z_ref[...], ng_ref[...])


def _sample_mix(u3, conv_st, sconv_st, h0, dw_w, dw_b, ln_g, ln_b, conv_w, conv_b, dt_bias, a_log,
                dskip_x, norm_g, rexp):
    b = u3.shape[0]
    const = lambda i: (0, 0)
    ublk = lambda w, off: pl.BlockSpec((None, 1, w), lambda i: (i, 0, off // w))
    out1 = lambda w: pl.BlockSpec((None, 1, w), lambda i: (i, 0, 0))
    return pl.pallas_call(
        _sample_mix_kernel,
        grid=(b,),
        in_specs=[ublk(2 * D_MODEL, UCF_OFF), ublk(D_INNER, Z_OFF), ublk(SSM_CONV_DIM, XBC_OFF),
                  ublk(128, SMALL_OFF),
                  pl.BlockSpec((None, CONV_K - 1, D_MODEL), lambda i: (i, 0, 0)),
                  pl.BlockSpec((None, SSM_CONV_K - 1, SSM_CONV_DIM), lambda i: (i, 0, 0)),
                  pl.BlockSpec((None, D_INNER, D_STATE), lambda i: (i, 0, 0)),
                  pl.BlockSpec((CONV_K, D_MODEL), const),
                  pl.BlockSpec((1, D_MODEL), const),
                  pl.BlockSpec((1, D_MODEL), const),
                  pl.BlockSpec((1, D_MODEL), const),
                  pl.BlockSpec((SSM_CONV_K, SSM_CONV_DIM), const),
                  pl.BlockSpec((1, SSM_CONV_DIM), const),
                  pl.BlockSpec((1, 128), const),
                  pl.BlockSpec((1, 128), const),
                  pl.BlockSpec((1, D_INNER), const),
                  pl.BlockSpec((1, D_INNER), const),
                  pl.BlockSpec((128, D_INNER), const)],
        out_specs=[out1(D_MODEL), out1(D_INNER), out1(D_MODEL),
                   pl.BlockSpec((None, D_INNER, D_STATE), lambda i: (i, 0, 0))],
        out_shape=[jax.ShapeDtypeStruct((b, 1, D_MODEL), F32),
                   jax.ShapeDtypeStruct((b, 1, D_INNER), F32),
                   jax.ShapeDtypeStruct((b, 1, D_MODEL), F32),
                   jax.ShapeDtypeStruct((b, D_INNER, D_STATE), F32)],
        compiler_params=_cparams("arbitrary"),
        name="sample_mix",
    )(u3, u3, u3, u3, conv_st, sconv_st, h0, dw_w, dw_b, ln_g, ln_b, conv_w, conv_b,
      _pad_lanes(dt_bias), _pad_lanes(a_log), dskip_x, norm_g, rexp)


SEL_LANES = 128


def _sample_cmp_kernel(pt_ref, qp_ref, kpool_ref, vpool_ref, oc_ref, idx_ref, kbuf, vbuf, *, n_pages, t_pos):
    b = pl.program_id(0)
    nb = 2 * n_pages
    for pg in range(n_pages):
        src = pt_ref[b, pg] * 2
        kbuf[2 * pg:2 * pg + 2, :] = kpool_ref[pl.ds(src, 2), :]
        vbuf[2 * pg:2 * pg + 2, :] = vpool_ref[pl.ds(src, 2), :]
    qp = qp_ref[...]
    s = _dot_nt(qp, kbuf[...].astype(BF16))
    jl = lax.broadcasted_iota(jnp.int32, (N_HEADS, nb), 1)
    mask_c = (jl + 1) * BLOCK - 1 <= t_pos
    s = jnp.where(mask_c, s, NEG)
    m = jnp.max(s, axis=1, keepdims=True)
    e = jnp.where(mask_c, jnp.exp2(s - m), 0.0)
    p = e * (1.0 / jnp.maximum(jnp.sum(e, axis=1, keepdims=True), 1e-30))
    oc_ref[...] = _dot(p.astype(BF16), vbuf[...].astype(BF16))

    nbs = t_pos // BLOCK + 1
    width = ((nbs + 127) // 128) * 128
    imp = jnp.concatenate(
        [jnp.sum(p[g * HEADS_PER_GROUP:(g + 1) * HEADS_PER_GROUP, :], axis=0, keepdims=True)
         for g in range(KV_GROUPS)] + [jnp.zeros((8 - KV_GROUPS, nb), F32)], axis=0)
    imp = jnp.concatenate([imp, jnp.zeros((8, width - nb), F32)], axis=1)
    jw = lax.broadcasted_iota(jnp.int32, (8, width), 1)
    cur = t_pos // BLOCK
    valid = jw <= cur
    forced = valid & ((jw == 0) | (jw == cur) | (jw == cur - 1))
    score = jnp.where(forced, BIG, jnp.where(valid, imp, -BIG))
    work = jnp.where(jw < nbs, score, -jnp.inf)
    lane = lax.broadcasted_iota(jnp.int32, (8, SEL_LANES), 1)
    out = jnp.full((8, SEL_LANES), -1, jnp.int32)
    for it in range(TOP_N):
        mx = jnp.max(work, axis=1, keepdims=True)
        first = jnp.min(jnp.where(work == mx, jw, width), axis=1, keepdims=True)
        out = jnp.where(lane == it, jnp.where(mx > -0.5 * BIG, first, -1), out)
        work = jnp.where(jw == first, -jnp.inf, work)
    idx_ref[...] = out


def _sample_cmp(page_table, qp_pad, kpool, vpool, t_pos):
    b, n_pages = page_table.shape
    npool = kpool.shape[0]
    nb = 2 * n_pages
    kern = functools.partial(_sample_cmp_kernel, n_pages=n_pages, t_pos=t_pos)
    return pl.pallas_call(
        kern,
        grid_spec=pltpu.PrefetchScalarGridSpec(
            num_scalar_prefetch=1,
            grid=(b,),
            in_specs=[pl.BlockSpec((None, N_HEADS, KV_W), lambda i, pt: (i, 0, 0)),
                      pl.BlockSpec((npool, KV_W), lambda i, pt: (0, 0)),
                      pl.BlockSpec((npool, KV_W), lambda i, pt: (0, 0))],
            out_specs=[pl.BlockSpec((None, N_HEADS, KV_W), lambda i, pt: (i, 0, 0)),
                       pl.BlockSpec((None, 8, SEL_LANES), lambda i, pt: (i, 0, 0))],
            scratch_shapes=[pltpu.VMEM((nb, KV_W), F32), pltpu.VMEM((nb, KV_W), F32)]),
        out_shape=[jax.ShapeDtypeStruct((b, N_HEADS, KV_W), F32),
                   jax.ShapeDtypeStruct((b, 8, SEL_LANES), jnp.int32)],
        compiler_params=_cparams("arbitrary"),
        name="sample_cmp",
    )(page_table, qp_pad, kpool, vpool)


def _sample_attn_kernel(idx_ref, pt_ref, q16_ref, qr_ref, new_ref, wk_ref, wv_ref, oc_ref, gate_ref,
                        kcache_ref, vcache_ref, o_ref, kbuf, vbuf, sem, *, n_pages, t_pos, layer):
    b = pl.program_id(0)
    nb_past = 2 * n_pages

    def page_copy(cache_ref, buf, g, k, which):
        j = idx_ref[b, g, k]
        jj = jnp.where((j >= 0) & (j < nb_past), j, 0)
        page = pt_ref[b, jj // 2]
        return pltpu.make_async_copy(cache_ref.at[layer, page, g],
                                     buf.at[g, :, pl.ds(k * PAGE_SIZE, PAGE_SIZE)], sem.at[which, g, k])

    for g in range(KV_GROUPS):
        for k in range(TOP_N):
            page_copy(kcache_ref, kbuf, g, k, 0).start()
            page_copy(vcache_ref, vbuf, g, k, 1).start()

    qr = qr_ref[...]
    qf = qr.astype(F32)
    new = new_ref[...]
    grp_lo = lax.broadcasted_iota(jnp.int32, (N_HEADS, KV_W), 0) < HEADS_PER_GROUP
    lane_lo = lax.broadcasted_iota(jnp.int32, (N_HEADS, KV_W), 1) < HEAD_DIM
    own = grp_lo == lane_lo

    def own_half(x):
        x = jnp.where(own, x, 0.0)
        return x[:, :HEAD_DIM] + x[:, HEAD_DIM:]

    win = wk_ref.shape[0]
    s = _dot_nt(qr, wk_ref[...].astype(BF16))
    wpos = t_pos - win + lax.broadcasted_iota(jnp.int32, (N_HEADS, win), 1)
    mask_w = t_pos - wpos < WINDOW
    s = jnp.where(mask_w, s, NEG)
    s_new = jnp.sum(qf * new[2:3, :], axis=1, keepdims=True)
    m = jnp.maximum(jnp.max(s, axis=1, keepdims=True), s_new)
    p = jnp.where(mask_w, jnp.exp2(s - m), 0.0)
    p_new = jnp.exp2(s_new - m)
    lsum = jnp.sum(p, axis=1, keepdims=True) + p_new
    o_w = own_half((_dot(p.astype(BF16), wv_ref[...].astype(BF16)) + p_new * new[3:4, :]) * (1.0 / lsum))

    for g in range(KV_GROUPS):
        for k in range(TOP_N):
            page_copy(kcache_ref, kbuf, g, k, 0).wait()
            page_copy(vcache_ref, vbuf, g, k, 1).wait()
    nsel = TOP_N * PAGE_SIZE
    lane = lax.broadcasted_iota(jnp.int32, (1, nsel), 1)
    lane_slot = lane // PAGE_SIZE
    lane_half = (lane // BLOCK) % 2
    o_s_parts = []
    for g in range(KV_GROUPS):
        ok = jnp.zeros((1, nsel), jnp.int32)
        has_new = jnp.zeros((1, 1), jnp.int32)
        for k in range(TOP_N):
            j = idx_ref[b, g, k]
            in_cache = (j >= 0) & (j < nb_past)
            hit = jnp.where(in_cache, (lane_half == j % 2).astype(jnp.int32), 0)
            ok = jnp.where(lane_slot == k, hit, ok)
            has_new = jnp.maximum(has_new, (j == nb_past).astype(jnp.int32))
        mask_s = ok > 0
        mask_n = has_new > 0
        qg = q16_ref[g * HEADS_PER_GROUP:(g + 1) * HEADS_PER_GROUP, :]
        k_new = new[0:1, g * HEAD_DIM:(g + 1) * HEAD_DIM]
        v_new = new[1:2, g * HEAD_DIM:(g + 1) * HEAD_DIM]
        s = jnp.where(mask_s, _dot(qg, kbuf[g].astype(BF16)), NEG)
        s_new = jnp.where(mask_n, jnp.sum(qg.astype(F32) * k_new, axis=1, keepdims=True), NEG)
        m = jnp.maximum(jnp.max(s, axis=1, keepdims=True), s_new)
        p = jnp.where(mask_s, jnp.exp2(s - m), 0.0)
        p_new = jnp.where(mask_n, jnp.exp2(s_new - m), 0.0)
        lsum = jnp.maximum(jnp.sum(p, axis=1, keepdims=True) + p_new, 1e-30)
        o_s_parts.append((_dot_nt(p.astype(BF16), vbuf[g].astype(BF16)) + p_new * v_new) * (1.0 / lsum))
    o_s = jnp.concatenate(o_s_parts, axis=0)

    gate = gate_ref[...]
    o_ref[...] = gate[:, 0:1] * own_half(oc_ref[...]) + gate[:, 1:2] * o_s + gate[:, 2:3] * o_w


def _sample_attn(idx, page_table, q16, qr_pad, new_rows, win_k, win_v, o_c, gates, kcache, vcache, t_pos, layer):
    b, n_pages = page_table.shape
    win = win_k.shape[1]
    kern = functools.partial(_sample_attn_kernel, n_pages=n_pages, t_pos=t_pos, layer=layer)
    per_b = lambda r, c: pl.BlockSpec((None, r, c), lambda i, ix, pt: (i, 0, 0))
    return pl.pallas_call(
        kern,
        grid_spec=pltpu.PrefetchScalarGridSpec(
            num_scalar_prefetch=2,
            grid=(b,),
            in_specs=[per_b(N_HEADS, HEAD_DIM), per_b(N_HEADS, KV_W), per_b(8, KV_W), per_b(win, KV_W),
                      per_b(win, KV_W), per_b(N_HEADS, KV_W), per_b(N_HEADS, KV_W),
                      pl.BlockSpec(memory_space=pl.ANY), pl.BlockSpec(memory_space=pl.ANY)],
            out_specs=per_b(N_HEADS, HEAD_DIM),
            scratch_shapes=[pltpu.VMEM((KV_GROUPS, HEAD_DIM, TOP_N * PAGE_SIZE), F32),
                            pltpu.VMEM((KV_GROUPS, HEAD_DIM, TOP_N * PAGE_SIZE), F32),
                            pltpu.SemaphoreType.DMA((2, KV_GROUPS, TOP_N))]),
        out_shape=jax.ShapeDtypeStruct((b, N_HEADS, HEAD_DIM), F32),
        compiler_params=_cparams("arbitrary"),
        name="sample_attn",
    )(idx, page_table, q16, qr_pad, new_rows, win_k, win_v, o_c, gates, kcache, vcache)


def _pack_w_in(w):
    segs = [w[:, 4096:7168], w[:, 9040:12112], w[:, 0:2048], w[:, 2048:4096], w[:, 7200:8224],
            w[:, 8224:8992], w[:, 7168:7200], w[:, 8992:9040],
            jnp.zeros((D_MODEL, N_PACK - 12112), w.dtype)]
    return jnp.concatenate(segs, axis=1).astype(BF16)


def _expand_cmp_weights(pe, w1, w2):
    eye = jnp.eye(KV_GROUPS, dtype=F32)
    w1r = w1.reshape(BLOCK, HEAD_DIM, CMP_HIDDEN)
    w1e = jnp.einsum('ldh,gk->lgdkh', w1r, eye).reshape(BLOCK * KV_W, KV_GROUPS * CMP_HIDDEN).astype(BF16)
    w2e = jnp.einsum('hd,gk->ghkd', w2, eye).reshape(KV_GROUPS * CMP_HIDDEN, KV_W).astype(BF16)
    pe2 = jnp.broadcast_to(pe[:, None, :], (BLOCK, KV_GROUPS, HEAD_DIM)).reshape(1, BLOCK * KV_W)
    halves = PAGE_SIZE // BLOCK
    w1t = jnp.einsum('ldh,ab->dalbh', w1r, jnp.eye(halves, dtype=F32))
    w1t = w1t.reshape(HEAD_DIM * PAGE_SIZE, halves * CMP_HIDDEN).astype(BF16)
    pe_t = jnp.tile(pe.T, (1, halves))
    return (pe2, w1e, w2e), (pe_t, w1t, w2e)


def _rope_tables(pos):
    half = ROT_DIM // 2
    inv = ROPE_THETA ** (-(jnp.arange(half, dtype=F32) * 2.0 / ROT_DIM))
    ang = pos.astype(F32)[:, None] * inv[None, :]
    cos = jnp.cos(ang)
    sin = jnp.sin(ang)
    n = pos.shape[0]
    one = jnp.ones((n, HEAD_DIM - ROT_DIM), F32)
    zero8 = jnp.zeros((n, half), F32)
    zero = jnp.zeros((n, HEAD_DIM - ROT_DIM), F32)
    c = jnp.concatenate([cos, cos, one], axis=1)
    s1 = jnp.concatenate([zero8, sin, zero], axis=1)
    s2 = jnp.concatenate([-sin, zero8, zero], axis=1)
    return jnp.stack([jnp.concatenate([t, t], axis=1) for t in (c, s1, s2)])


def _head_expand_matrix():
    h = np.arange(128)[:, None]
    lane = np.arange(D_INNER)[None, :]
    return jnp.asarray((lane // SSM_HEADDIM == h).astype(np.float32), dtype=BF16)


def _group_padded(q):
    b = q.shape[0]
    qh = q.reshape(b, KV_GROUPS, HEADS_PER_GROUP, HEAD_DIM)
    z = jnp.zeros_like(qh[:, 0])
    lo = jnp.concatenate([qh[:, 0], z], axis=-1)
    hi = jnp.concatenate([z, qh[:, 1]], axis=-1)
    return jnp.concatenate([lo, hi], axis=1).astype(BF16)


def _layer_weights(l, p):
    w = {k: v[l] for k, v in p.items()}
    w['w_in_p'] = _pack_w_in(w['w_in'])
    for nm in ('w_cf_out', 'w_ssm_out', 'w_attn_out', 'w_o', 'w_ffn_in', 'w_ffn_out'):
        w[nm] = w[nm].astype(BF16)
    w['cmp_k'] = _expand_cmp_weights(w['cmp_pe_k'], w['cmp_w1_k'], w['cmp_w2_k'])
    w['cmp_v'] = _expand_cmp_weights(w['cmp_pe_v'], w['cmp_w1_v'], w['cmp_w2_v'])
    w['dskip_x'] = jnp.repeat(w['d_skip'], SSM_HEADDIM)[None, :]
    for nm in ('norm1_g', 'norm2_g', 'cf_dw_b', 'cf_ln_g', 'cf_ln_b', 'ssm_conv_b', 'dt_bias', 'a_log',
               'ssm_norm_g'):
        w[nm] = w[nm][None, :]
    return w


def _prompt_layer(x, mod, w, tab, rexp):
    l = x.shape[0]
    sh1, sc1, g1, sh2, sc2, g2 = [mod[:, k * D_MODEL:(k + 1) * D_MODEL] for k in range(6)]
    u = _in_proj(x, w['norm1_g'], sc1, sh1, w['w_in_p'], 1024)
    br_cf, conv_tail = _conformer_prompt(u, w['cf_dw_w'], w['cf_dw_b'], w['cf_ln_g'], w['cf_ln_b'], w['w_cf_out'])
    br_ssm, h_new, sconv_tail = _ssd_prompt(u, w['ssm_conv_w'], w['ssm_conv_b'], w['dt_bias'], w['a_log'],
                                            w['dskip_x'], w['ssm_norm_g'], rexp, w['w_ssm_out'])
    kc, vc, ks, vs, kw, vw, qpt, qrt, ksb, vst, kwb, vwt, gt = _prep_prompt(u, tab)
    kcmp = _compress(kc.reshape(l // BLOCK, BLOCK * KV_W), *w['cmp_k'][0])
    vcmp = _compress(vc.reshape(l // BLOCK, BLOCK * KV_W), *w['cmp_v'][0])
    att = _attention_prompt(qpt, qrt, gt, kcmp, vcmp, ksb, vst, kwb, vwt)
    x = _merge(x, br_cf, br_ssm, att, u, g1, w['w_attn_out'], w['w_o'], 512)
    x = _ffn(x, w['norm2_g'], sc2, sh2, g2, w['w_ffn_in'], w['w_ffn_out'], 512)
    keep = min(WINDOW, l)
    kv4 = lambda a: a.reshape(1, -1, KV_GROUPS, HEAD_DIM)
    state = (kv4(kc), kv4(vc), kv4(ks), kv4(vs), kv4(kw[l - keep:]), kv4(vw[l - keep:]),
             conv_tail[CF_HALO - (CONV_K - 1):][None],
             sconv_tail[8 - (SSM_CONV_K - 1):][None],
             h_new.reshape(1, SSM_HEADS, SSM_HEADDIM, D_STATE))
    return x, state


def _sample_layer(x, mod, w, tab, rexp, caches, layer, states, page_table, t_pos):
    b = x.shape[0]
    cache_ck, cache_cv, cache_sk, cache_sv = caches
    win_k, win_v, conv_st, sconv_st, ssm_st = states
    depth, n_pool = cache_ck.shape[0], cache_ck.shape[1]
    rows2d = lambda c: c.reshape(depth, n_pool * KV_GROUPS * HEAD_DIM, PAGE_SIZE)
    sh1, sc1, g1, sh2, sc2, g2 = [mod[:, k * D_MODEL:(k + 1) * D_MODEL] for k in range(6)]
    u = _in_proj(x, w['norm1_g'], sc1, sh1, w['w_in_p'], b)
    cv, yn, glu, h_new = _sample_mix(u.reshape(b, 1, N_PACK), conv_st, sconv_st,
                                     ssm_st.reshape(b, D_INNER, D_STATE),
                                     w['cf_dw_w'], w['cf_dw_b'], w['cf_ln_g'], w['cf_ln_b'],
                                     w['ssm_conv_w'], w['ssm_conv_b'], w['dt_bias'], w['a_log'],
                                     w['dskip_x'], w['ssm_norm_g'], rexp)
    br_cf = _matmul(cv.reshape(b, D_MODEL), w['w_cf_out'])
    br_ssm = _matmul(yn.reshape(b, D_INNER), w['w_ssm_out'])
    qp, qr, kvo = _prep_sample(u, tab)
    seg = lambda k: kvo[:, k * KV_W:(k + 1) * KV_W]
    kc, vc, ks, vs, kw, vw = [seg(k) for k in range(6)]
    gates = seg(6)[:, 32:32 + 3 * N_HEADS].reshape(b, 3, N_HEADS).transpose(0, 2, 1)
    gates = jnp.concatenate([gates, jnp.zeros((b, N_HEADS, KV_W - 3), F32)], axis=-1)
    kpool = _compress_pool(rows2d(cache_ck), layer, *w['cmp_k'][1])
    vpool = _compress_pool(rows2d(cache_cv), layer, *w['cmp_v'][1])
    o_c, idx = _sample_cmp(page_table, _group_padded(qp), kpool, vpool, t_pos)
    new_rows = jnp.stack([ks, vs, kw, vw] + [jnp.zeros_like(ks)] * 4, axis=1)
    att = _sample_attn(idx[:, :KV_GROUPS, :TOP_N], page_table, qr.reshape(b, N_HEADS, HEAD_DIM).astype(BF16),
                       _group_padded(qr), new_rows,
                       win_k.reshape(b, -1, KV_W), win_v.reshape(b, -1, KV_W), o_c, gates,
                       cache_sk, cache_sv, t_pos, layer)
    x = _merge(x, br_cf, br_ssm, att.reshape(b, D_MODEL), u, g1, w['w_attn_out'], w['w_o'], b)
    x = _ffn(x, w['norm2_g'], sc2, sh2, g2, w['w_ffn_in'], w['w_ffn_out'], b)
    keep = min(WINDOW, t_pos + 1)
    kv4 = lambda a: a.reshape(b, 1, KV_GROUPS, HEAD_DIM)
    new_kw = jnp.concatenate([win_k, kv4(kw)], axis=1)
    new_vw = jnp.concatenate([win_v, kv4(vw)], axis=1)
    state = (kv4(kc), kv4(vc), kv4(ks), kv4(vs),
             new_kw[:, new_kw.shape[1] - keep:], new_vw[:, new_vw.shape[1] - keep:],
             jnp.concatenate([conv_st[:, 1:], glu], axis=1),
             jnp.concatenate([sconv_st[:, 1:], u[:, None, XBC_OFF:XBC_OFF + SSM_CONV_DIM]], axis=1),
             h_new.reshape(b, SSM_HEADS, SSM_HEADDIM, D_STATE))
    return x, state


def kernel(x_prompt, x_sample, c_prompt, c_sample, cache_cmp_k, cache_cmp_v, cache_slc_k, cache_slc_v, state_win_k, state_win_v, state_conv, state_ssm_conv, state_ssm, page_table, norm1_g, w_ada, b_ada, w_in, cf_dw_w, cf_dw_b, cf_ln_g, cf_ln_b, w_cf_out, ssm_conv_w, ssm_conv_b, dt_bias, a_log, d_skip, ssm_norm_g, w_ssm_out, cmp_pe_k, cmp_pe_v, cmp_w1_k, cmp_w2_k, cmp_w1_v, cmp_w2_v, w_attn_out, w_o, norm2_g, w_ffn_in, w_ffn_out, final_g):
    depth = w_in.shape[0]
    bp, lp = x_prompt.shape[0], x_prompt.shape[1]
    bs = x_sample.shape[0]
    assert bp == 1 and x_sample.shape[1] == 1
    assert lp % TK == 0 and lp // BLOCK >= TOP_N
    t_pos = page_table.shape[1] * PAGE_SIZE
    params = dict(norm1_g=norm1_g, w_in=w_in, cf_dw_w=cf_dw_w, cf_dw_b=cf_dw_b, cf_ln_g=cf_ln_g, cf_ln_b=cf_ln_b,
                  w_cf_out=w_cf_out, ssm_conv_w=ssm_conv_w, ssm_conv_b=ssm_conv_b, dt_bias=dt_bias, a_log=a_log,
                  d_skip=d_skip, ssm_norm_g=ssm_norm_g, w_ssm_out=w_ssm_out, cmp_pe_k=cmp_pe_k, cmp_pe_v=cmp_pe_v,
                  cmp_w1_k=cmp_w1_k, cmp_w2_k=cmp_w2_k, cmp_w1_v=cmp_w1_v, cmp_w2_v=cmp_w2_v,
                  w_attn_out=w_attn_out, w_o=w_o, norm2_g=norm2_g, w_ffn_in=w_ffn_in, w_ffn_out=w_ffn_out)
    n_c = bp + bs
    c_pad = (-n_c) % 8
    c_all = jnp.concatenate([c_prompt, c_sample, jnp.zeros((c_pad, D_MODEL), F32)], axis=0)
    mod = _modulation(c_all, w_ada, b_ada)
    rexp = _head_expand_matrix()
    tab_p = _rope_tables(jnp.arange(lp, dtype=jnp.int32))
    tab_s = _rope_tables(jnp.full((bs,), t_pos, jnp.int32))
    caches = tuple(jnp.transpose(c, (0, 1, 3, 4, 2)) for c in (cache_cmp_k, cache_cmp_v, cache_slc_k, cache_slc_v))

    xp = x_prompt.reshape(lp, D_MODEL)
    xs = x_sample.reshape(bs, D_MODEL)
    outs_p = [[] for _ in range(9)]
    outs_s = [[] for _ in range(9)]
    for l in range(depth):
        w = _layer_weights(l, params)
        xp, st_p = _prompt_layer(xp, mod[l, 0:bp], w, tab_p, rexp)
        states = (state_win_k[l], state_win_v[l], state_conv[l], state_ssm_conv[l], state_ssm[l])
        xs, st_s = _sample_layer(xs, mod[l, bp:bp + bs], w, tab_s, rexp, caches, l, states, page_table, t_pos)
        for k in range(9):
            outs_p[k].append(st_p[k])
            outs_s[k].append(st_s[k])
    y_prompt = _final_norm(xp, final_g[None, :], 512).reshape(bp, lp, D_MODEL)
    y_sample = _final_norm(xs, final_g[None, :], bs).reshape(bs, 1, D_MODEL)
    np_ = [jnp.stack(o) for o in outs_p]
    ns_ = [jnp.stack(o) for o in outs_s]
    return (y_prompt, y_sample, *np_, *ns_)
```

```python
import functools

import numpy as np
import jax
import jax.numpy as jnp
from jax import lax
from jax.experimental import pallas as pl
from jax.experimental.pallas import tpu as pltpu

F32 = jnp.float32
BF16 = jnp.bfloat16

D_MODEL = 1024
PAGE_SIZE = 128
CONV_K = 31
D_INNER = 2 * D_MODEL
SSM_HEADDIM = 64
SSM_HEADS = D_INNER // SSM_HEADDIM
SSM_GROUPS = 4
D_STATE = 128
SSM_CONV_K = 4
SSM_CONV_DIM = D_INNER + 2 * SSM_GROUPS * D_STATE
SSM_CHUNK = 128
N_HEADS = 16
HEAD_DIM = 64
KV_GROUPS = 2
HEADS_PER_GROUP = N_HEADS // KV_GROUPS
BLOCK = 64
TOP_N = 16
WINDOW = 512
CMP_HIDDEN = 256
ROPE_THETA = 500000.0
ROT_DIM = HEAD_DIM // 4
BIG = 1e9
NEG = -1e30
D_FF = ((8 * D_MODEL + 3 * 256 - 1) // (3 * 256)) * 256
EPS = 1e-6
KV_W = KV_GROUPS * HEAD_DIM

XBC_OFF = 0
GMIX_OFF = 3072
UCF_OFF = 6144
Z_OFF = 8192
Q_OFF = 10240
KV_OFF = 11264
SMALL_OFF = KV_OFF + 6 * KV_W
N_PACK = 12288

TQ = 256
TK = 512
V_ROWS = HEAD_DIM + 16
Q_SCALE = HEAD_DIM ** -0.5 * 1.4426950408889634
LAZY_MAX_RISE = 64.0

VMEM_LIMIT = 56 * 1024 * 1024


def _cparams(*sem):
    return pltpu.CompilerParams(dimension_semantics=sem, vmem_limit_bytes=VMEM_LIMIT)


def _dot(a, b):
    return jnp.dot(a, b, preferred_element_type=F32)


def _dot_nt(a, b):
    return lax.dot_general(a, b, (((1,), (1,)), ((), ())), preferred_element_type=F32)


def _dot_tn(a, b):
    return lax.dot_general(a, b, (((0,), (0,)), ((), ())), preferred_element_type=F32)


def _sigmoid(x):
    return jax.nn.sigmoid(x)


def _silu(x):
    return x * jax.nn.sigmoid(x)


def _softplus(x):
    return jnp.maximum(x, 0.0) + jnp.log(1.0 + jnp.exp(-jnp.abs(x)))


def _split3(x):
    hi = x.astype(BF16)
    r1 = x - hi.astype(F32)
    mid = r1.astype(BF16)
    lo = (r1 - mid.astype(F32)).astype(BF16)
    return hi, mid, lo


def _dot_exact_rhs01(x, m01):
    hi, mid, lo = _split3(x)
    return _dot(hi, m01) + _dot(mid, m01) + _dot(lo, m01)


def _dot_exact_lhs01(m01, x):
    hi, mid, lo = _split3(x)
    return _dot(m01, hi) + _dot(m01, mid) + _dot(m01, lo)


def _mod_kernel(c_ref, w_ref, b_ref, o_ref):
    c = c_ref[...]
    o_ref[0] = _dot(_silu(c).astype(BF16), w_ref[0].astype(BF16)) + b_ref[0]


def _modulation(c_all, w_ada, b_ada):
    depth = w_ada.shape[0]
    bc = c_all.shape[0]
    n = w_ada.shape[2]
    tn = 1536
    return pl.pallas_call(
        _mod_kernel,
        grid=(depth, n // tn),
        in_specs=[pl.BlockSpec((bc, D_MODEL), lambda l, j: (0, 0)),
                  pl.BlockSpec((1, D_MODEL, tn), lambda l, j: (l, 0, j)),
                  pl.BlockSpec((1, 1, tn), lambda l, j: (l, 0, j))],
        out_specs=pl.BlockSpec((1, bc, tn), lambda l, j: (l, 0, j)),
        out_shape=jax.ShapeDtypeStruct((depth, bc, n), F32),
        compiler_params=_cparams("arbitrary", "arbitrary"),
        name="modulation",
    )(c_all, w_ada, b_ada.reshape(depth, 1, n))


def _in_proj_kernel(x_ref, g_ref, sc_ref, sh_ref, w_ref, o_ref, h_ref):
    @pl.when(pl.program_id(1) == 0)
    def _():
        x = x_ref[...]
        y = x * lax.rsqrt(jnp.mean(x * x, axis=-1, keepdims=True) + EPS) * g_ref[...]
        h_ref[...] = (y * (1.0 + sc_ref[...]) + sh_ref[...]).astype(BF16)

    o_ref[...] = _dot(h_ref[...], w_ref[...])


def _in_proj(x, g, sc, sh, w, tm):
    r = x.shape[0]
    n = w.shape[1]
    tn = 1024
    mrows = tm if sc.shape[0] == r else 1
    mmap = (lambda i, j: (i, 0)) if sc.shape[0] == r else (lambda i, j: (0, 0))
    return pl.pallas_call(
        _in_proj_kernel,
        grid=(r // tm, n // tn),
        in_specs=[pl.BlockSpec((tm, D_MODEL), lambda i, j: (i, 0)),
                  pl.BlockSpec((1, D_MODEL), lambda i, j: (0, 0)),
                  pl.BlockSpec((mrows, D_MODEL), mmap),
                  pl.BlockSpec((mrows, D_MODEL), mmap),
                  pl.BlockSpec((D_MODEL, tn), lambda i, j: (0, j))],
        out_specs=pl.BlockSpec((tm, tn), lambda i, j: (i, j)),
        out_shape=jax.ShapeDtypeStruct((r, n), F32),
        scratch_shapes=[pltpu.VMEM((tm, D_MODEL), BF16)],
        compiler_params=_cparams("arbitrary", "arbitrary"),
        name="in_proj",
    )(x, g, sc, sh, w)


def _matmul_kernel(x_ref, w_ref, o_ref):
    o_ref[...] = _dot(x_ref[...].astype(BF16), w_ref[...])


def _matmul(x, w):
    r, k = x.shape
    n = w.shape[1]
    return pl.pallas_call(
        _matmul_kernel,
        grid=(1,),
        in_specs=[pl.BlockSpec((r, k), lambda i: (0, 0)), pl.BlockSpec((k, n), lambda i: (0, 0))],
        out_specs=pl.BlockSpec((r, n), lambda i: (0, 0)),
        out_shape=jax.ShapeDtypeStruct((r, n), F32),
        compiler_params=_cparams("arbitrary"),
        name="row_matmul",
    )(x, w)


CF_TL = 256
CF_RC = 32
CF_HALO = 32


def _layernorm_silu(x, g, b):
    xc = x - jnp.mean(x, axis=-1, keepdims=True)
    y = xc * lax.rsqrt(jnp.mean(xc * xc, axis=-1, keepdims=True) + EPS)
    return _silu(y * g + b)


def _conformer_kernel(u_ref, w_ref, b_ref, lg_ref, lb_ref, wo_ref, o_ref, nc_ref, bufs, cvb):
    i = pl.program_id(0)
    tl = CF_TL

    @pl.when(i == 0)
    def _():
        bufs[0, 0:CF_HALO, :] = jnp.zeros((CF_HALO, D_MODEL), F32)

    u = u_ref[...]
    bufs[0, CF_HALO:CF_HALO + tl, :] = u[:, :D_MODEL] * _sigmoid(u[:, D_MODEL:])
    for s in range(1, 8):
        bufs[s, 0:tl + 24, :] = bufs[0, s:s + tl + 24, :]

    def chunk(c, carry):
        off = pl.multiple_of(c * CF_RC, CF_RC)
        acc = jnp.broadcast_to(b_ref[...], (CF_RC, D_MODEL))
        for k in range(CONV_K):
            a, s = divmod(CF_HALO - (CONV_K - 1) + k, 8)
            acc = acc + w_ref[k:k + 1, :] * bufs[s, pl.ds(off + 8 * a, CF_RC), :]
        cvb[pl.ds(off, CF_RC), :] = _layernorm_silu(acc, lg_ref[...], lb_ref[...]).astype(BF16)
        return carry

    lax.fori_loop(0, tl // CF_RC, chunk, 0)
    tail = bufs[0, tl:tl + CF_HALO, :]
    nc_ref[...] = tail
    bufs[0, 0:CF_HALO, :] = tail
    o_ref[...] = _dot(cvb[...], wo_ref[...])


def _conformer_prompt(u, dw_w, dw_b, ln_g, ln_b, w_out):
    l = u.shape[0]
    tl = CF_TL
    wpad = jnp.concatenate([dw_w, jnp.zeros((1, D_MODEL), F32)], axis=0)
    return pl.pallas_call(
        _conformer_kernel,
        grid=(l // tl,),
        in_specs=[pl.BlockSpec((tl, 2 * D_MODEL), lambda i: (i, UCF_OFF // (2 * D_MODEL))),
                  pl.BlockSpec((CONV_K + 1, D_MODEL), lambda i: (0, 0)),
                  pl.BlockSpec((1, D_MODEL), lambda i: (0, 0)),
                  pl.BlockSpec((1, D_MODEL), lambda i: (0, 0)),
                  pl.BlockSpec((1, D_MODEL), lambda i: (0, 0)),
                  pl.BlockSpec((D_MODEL, D_MODEL), lambda i: (0, 0))],
        out_specs=[pl.BlockSpec((tl, D_MODEL), lambda i: (i, 0)),
                   pl.BlockSpec((CF_HALO, D_MODEL), lambda i: (0, 0))],
        out_shape=[jax.ShapeDtypeStruct((l, D_MODEL), F32),
                   jax.ShapeDtypeStruct((CF_HALO, D_MODEL), F32)],
        scratch_shapes=[pltpu.VMEM((8, tl + CF_HALO, D_MODEL), F32),
                        pltpu.VMEM((tl, D_MODEL), BF16)],
        compiler_params=_cparams("arbitrary"),
        name="conformer_prompt",
    )(u, wpad, dw_b, ln_g, ln_b, w_out)


def _gated_group_norm(y, z, g):
    y = y * _silu(z)
    gw = D_INNER // SSM_GROUPS
    parts = []
    for k in range(SSM_GROUPS):
        yg = y[:, k * gw:(k + 1) * gw]
        parts.append(yg * lax.rsqrt(jnp.mean(yg * yg, axis=-1, keepdims=True) + EPS))
    return jnp.concatenate(parts, axis=1) * g


def _ssd_kernel(xbc_ref, z_ref, sm_ref, cw_ref, cb_ref, dtb_ref, alog_ref, dsk_ref, ng_ref, rexp_ref,
                wo_ref, o_ref, hout_ref, sc_ref, cbuf, hst):
    i = pl.program_id(0)
    q = SSM_CHUNK

    @pl.when(i == 0)
    def _():
        cbuf[0:8, :] = jnp.zeros((8, SSM_CONV_DIM), F32)
        hst[...] = jnp.zeros(hst.shape, F32)

    x = xbc_ref[...]
    cbuf[8:8 + q, :] = x
    conv = cb_ref[...] + cw_ref[3:4, :] * x
    for k in range(SSM_CONV_K - 1):
        conv = conv + cw_ref[k:k + 1, :] * cbuf[5 + k:5 + k + q, :]
    tail = cbuf[q:q + 8, :]
    sc_ref[...] = tail
    cbuf[0:8, :] = tail
    xa = _silu(conv)
    xs = xa[:, :D_INNER]
    bm = xa[:, D_INNER:D_INNER + SSM_GROUPS * D_STATE]
    cm = xa[:, D_INNER + SSM_GROUPS * D_STATE:]

    dt = _softplus(sm_ref[...] + dtb_ref[...])
    a = dt * (-jnp.exp(alog_ref[...]))
    row = lax.broadcasted_iota(jnp.int32, (q, q), 0)
    col = lax.broadcasted_iota(jnp.int32, (q, q), 1)
    tri = row >= col
    a_cum = _dot_exact_lhs01(tri.astype(BF16), a)
    a_cum_t = a_cum.T
    a_last = a_cum[q - 1:q, :]
    rexp = rexp_ref[...]
    dtx = _dot_exact_rhs01(dt, rexp)
    eax = _dot_exact_rhs01(jnp.exp(a_cum), rexp)
    decx = _dot_exact_rhs01(jnp.exp(a_last - a_cum), rexp)
    xin = xs * dtx
    xdec = (xin * decx).astype(BF16)
    xin_b = xin.astype(BF16)
    cd_col = jnp.broadcast_to(jnp.exp(a_last), (q, q)).T
    lane_lo = lax.broadcasted_iota(jnp.int32, (q, 2 * SSM_HEADDIM), 1) < SSM_HEADDIM

    hpg = SSM_HEADS // SSM_GROUPS
    gw = hpg * SSM_HEADDIM
    y_groups = []
    for g in range(SSM_GROUPS):
        cg = cm[:, g * D_STATE:(g + 1) * D_STATE].astype(BF16)
        bg = bm[:, g * D_STATE:(g + 1) * D_STATE].astype(BF16)
        cb = _dot_nt(cg, bg)
        hprev = hst[g * gw:(g + 1) * gw, :]
        y_off = _dot_nt(cg, hprev.astype(BF16)) * eax[:, g * gw:(g + 1) * gw]
        st = _dot_tn(xdec[:, g * gw:(g + 1) * gw], bg)
        pair_parts = []
        for pr in range(hpg // 2):
            xp = xin_b[:, g * gw + pr * 128:g * gw + (pr + 1) * 128]
            yp = None
            for sub in range(2):
                h = g * hpg + pr * 2 + sub
                seg = a_cum[:, h:h + 1] - a_cum_t[h:h + 1, :]
                lm = jnp.exp(jnp.where(tri, seg, -jnp.inf))
                mh = (cb * lm).astype(BF16)
                xm = jnp.where(lane_lo if sub == 0 else jnp.logical_not(lane_lo), xp, jnp.zeros_like(xp))
                d = _dot(mh, xm)
                yp = d if yp is None else yp + d
                r0 = h * SSM_HEADDIM
                lo = (pr * 2 + sub) * SSM_HEADDIM
                hst[r0:r0 + SSM_HEADDIM, :] = (hprev[lo:lo + SSM_HEADDIM, :] * cd_col[h:h + 1, :]
                                              + st[lo:lo + SSM_HEADDIM, :])
            pair_parts.append(yp)
        y_groups.append(jnp.concatenate(pair_parts, axis=1) + y_off)
    y = jnp.concatenate(y_groups, axis=1) + dsk_ref[...] * xs
    yn = _gated_group_norm(y, z_ref[...], ng_ref[...])
    o_ref[...] = _dot(yn.astype(BF16), wo_ref[...])

    @pl.when(i == pl.num_programs(0) - 1)
    def _():
        hout_ref[...] = hst[...]


def _pad_lanes(v, n=128):
    return jnp.concatenate([v, jnp.zeros((v.shape[0], n - v.shape[1]), v.dtype)], axis=1)


def _ssd_prompt(u, conv_w, conv_b, dt_bias, a_log, dskip_x, norm_g, rexp, w_out):
    l = u.shape[0]
    q = SSM_CHUNK
    cw = jnp.concatenate([conv_w, jnp.zeros((4, SSM_CONV_DIM), F32)], axis=0)
    const = lambda i: (0, 0)
    return pl.pallas_call(
        _ssd_kernel,
        grid=(l // q,),
        in_specs=[pl.BlockSpec((q, SSM_CONV_DIM), lambda i: (i, XBC_OFF // SSM_CONV_DIM)),
                  pl.BlockSpec((q, D_INNER), lambda i: (i, Z_OFF // D_INNER)),
                  pl.BlockSpec((q, 128), lambda i: (i, SMALL_OFF // 128)),
                  pl.BlockSpec((8, SSM_CONV_DIM), const),
                  pl.BlockSpec((1, SSM_CONV_DIM), const),
                  pl.BlockSpec((1, 128), const),
                  pl.BlockSpec((1, 128), const),
                  pl.BlockSpec((1, D_INNER), const),
                  pl.BlockSpec((1, D_INNER), const),
                  pl.BlockSpec((128, D_INNER), const),
                  pl.BlockSpec((D_INNER, D_MODEL), const)],
        out_specs=[pl.BlockSpec((q, D_MODEL), lambda i: (i, 0)),
                   pl.BlockSpec((D_INNER, D_STATE), const),
                   pl.BlockSpec((8, SSM_CONV_DIM), const)],
        out_shape=[jax.ShapeDtypeStruct((l, D_MODEL), F32),
                   jax.ShapeDtypeStruct((D_INNER, D_STATE), F32),
                   jax.ShapeDtypeStruct((8, SSM_CONV_DIM), F32)],
        scratch_shapes=[pltpu.VMEM((q + 8, SSM_CONV_DIM), F32),
                        pltpu.VMEM((D_INNER, D_STATE), F32)],
        compiler_params=_cparams("arbitrary"),
        name="ssd_prompt",
    )(u, u, u, cw, conv_b, _pad_lanes(dt_bias), _pad_lanes(a_log), dskip_x, norm_g, rexp, w_out)


def _rope(x, c, s1, s2):
    n = x.shape[1]
    return x * c + pltpu.roll(x, ROT_DIM // 2, 1) * s1 + pltpu.roll(x, n - ROT_DIM // 2, 1) * s2


def _aug_vt(v):
    vt = v.T.astype(BF16)
    ones = jnp.ones((V_ROWS - HEAD_DIM, v.shape[0]), BF16)
    return [jnp.concatenate([vt[g * HEAD_DIM:(g + 1) * HEAD_DIM, :], ones], axis=0) for g in range(KV_GROUPS)]


def _prep_kernel(q_ref, kv_ref, tab_ref, kc_ref, vc_ref, ks_ref, vs_ref, kw_ref, vw_ref,
                 qpt_ref, qrt_ref, ksb_ref, vst_ref, kwb_ref, vwt_ref, gt_ref):
    i = pl.program_id(0)
    tl = q_ref.shape[0]
    c = tab_ref[0]
    s1 = tab_ref[1]
    s2 = tab_ref[2]
    reps = D_MODEL // 128
    q = q_ref[...] * Q_SCALE
    qr = _rope(q, jnp.concatenate([c] * reps, axis=1), jnp.concatenate([s1] * reps, axis=1),
               jnp.concatenate([s2] * reps, axis=1))
    qpt_ref[...] = q.T.astype(BF16)
    qrt_ref[...] = qr.T.astype(BF16)
    kv = kv_ref[...]
    kc_ref[...] = kv[:, 0:KV_W]
    vc_ref[...] = kv[:, KV_W:2 * KV_W]
    ks = _rope(kv[:, 2 * KV_W:3 * KV_W], c, s1, s2)
    ks_ref[...] = ks
    row = lax.broadcasted_iota(jnp.int32, (tl, KV_W), 0)
    lane = lax.broadcasted_iota(jnp.int32, (tl, KV_W), 1)
    blk_in_tile = lax.shift_right_logical((i * tl) % TK + row, 6)
    ksb_ref[...] = jnp.concatenate([ks, jnp.where(lane == blk_in_tile, 1.0, 0.0)], axis=1).astype(BF16)
    vs = kv[:, 3 * KV_W:4 * KV_W]
    vs_ref[...] = vs
    kw = _rope(kv[:, 4 * KV_W:5 * KV_W], c, s1, s2)
    kw_ref[...] = kw
    kwb_ref[...] = kw.astype(BF16)
    vw = kv[:, 5 * KV_W:6 * KV_W]
    vw_ref[...] = vw
    for g, (a, b) in enumerate(zip(_aug_vt(vs), _aug_vt(vw))):
        vst_ref[0, g] = a
        vwt_ref[0, g] = b
    gt_ref[...] = _sigmoid(kv[:, 6 * KV_W:7 * KV_W]).T


def _prep_prompt(u, tab):
    l = u.shape[0]
    tl = TQ
    nt = l // tl
    row = lambda i: (i, 0)
    f32s = jax.ShapeDtypeStruct((l, KV_W), F32)
    vspec = pl.BlockSpec((1, KV_GROUPS, V_ROWS, tl), lambda i: (i, 0, 0, 0))
    vshape = jax.ShapeDtypeStruct((nt, KV_GROUPS, V_ROWS, tl), BF16)
    return pl.pallas_call(
        _prep_kernel,
        grid=(nt,),
        in_specs=[pl.BlockSpec((tl, D_MODEL), lambda i: (i, Q_OFF // D_MODEL)),
                  pl.BlockSpec((tl, D_MODEL), lambda i: (i, KV_OFF // D_MODEL)),
                  pl.BlockSpec((3, tl, 128), lambda i: (0, i, 0))],
        out_specs=[pl.BlockSpec((tl, KV_W), row)] * 6 + [
            pl.BlockSpec((D_MODEL, tl), lambda i: (0, i)),
            pl.BlockSpec((D_MODEL, tl), lambda i: (0, i)),
            pl.BlockSpec((tl, 2 * KV_W), row),
            vspec,
            pl.BlockSpec((tl, KV_W), row),
            vspec,
            pl.BlockSpec((128, tl), lambda i: (0, i))],
        out_shape=[f32s] * 6 + [
            jax.ShapeDtypeStruct((D_MODEL, l), BF16),
            jax.ShapeDtypeStruct((D_MODEL, l), BF16),
            jax.ShapeDtypeStruct((l, 2 * KV_W), BF16),
            vshape,
            jax.ShapeDtypeStruct((l, KV_W), BF16),
            vshape,
            jax.ShapeDtypeStruct((128, l), F32)],
        compiler_params=_cparams("arbitrary"),
        name="prep_prompt",
    )(u, u, tab)


def _prep_sample_kernel(q_ref, kv_ref, tab_ref, qp_ref, qr_ref, kvo_ref):
    c = tab_ref[0]
    s1 = tab_ref[1]
    s2 = tab_ref[2]
    reps = D_MODEL // 128
    q = q_ref[...] * Q_SCALE
    qp_ref[...] = q
    qr_ref[...] = _rope(q, jnp.concatenate([c] * reps, axis=1), jnp.concatenate([s1] * reps, axis=1),
                        jnp.concatenate([s2] * reps, axis=1))
    kv = kv_ref[...]
    one = jnp.ones_like(c)
    zero = jnp.zeros_like(c)
    ckv = jnp.concatenate([one, one, c, one, c, one, one, one], axis=1)
    s1kv = jnp.concatenate([zero, zero, s1, zero, s1, zero, zero, zero], axis=1)
    s2kv = jnp.concatenate([zero, zero, s2, zero, s2, zero, zero, zero], axis=1)
    kvr = _rope(kv, ckv, s1kv, s2kv)
    lane = lax.broadcasted_iota(jnp.int32, kv.shape, 1)
    kvo_ref[...] = jnp.where((lane >= 6 * KV_W) & (lane < 7 * KV_W), _sigmoid(kv), kvr)


def _prep_sample(u, tab):
    b = u.shape[0]
    full = lambda i: (0, 0)
    return pl.pallas_call(
        _prep_sample_kernel,
        grid=(1,),
        in_specs=[pl.BlockSpec((b, D_MODEL), lambda i: (0, Q_OFF // D_MODEL)),
                  pl.BlockSpec((b, D_MODEL), lambda i: (0, KV_OFF // D_MODEL)),
                  pl.BlockSpec((3, b, 128), lambda i: (0, 0, 0))],
        out_specs=[pl.BlockSpec((b, D_MODEL), full)] * 3,
        out_shape=[jax.ShapeDtypeStruct((b, D_MODEL), F32)] * 3,
        compiler_params=_cparams("arbitrary"),
        name="prep_sample",
    )(u, u, tab)


def _compress_kernel(x_ref, pe_ref, w1_ref, w2_ref, o_ref):
    x = (x_ref[...] + pe_ref[...]).astype(BF16)
    hid = _silu(_dot(x, w1_ref[...]))
    o_ref[...] = _dot(hid.astype(BF16), w2_ref[...])


def _compress(x2d, pe2, w1e, w2e):
    m = x2d.shape[0]
    tm = 256 if m % 256 == 0 else m
    kdim = BLOCK * KV_W
    return pl.pallas_call(
        _compress_kernel,
        grid=(m // tm,),
        in_specs=[pl.BlockSpec((tm, kdim), lambda i: (i, 0)),
                  pl.BlockSpec((1, kdim), lambda i: (0, 0)),
                  pl.BlockSpec((kdim, KV_GROUPS * CMP_HIDDEN), lambda i: (0, 0)),
                  pl.BlockSpec((KV_GROUPS * CMP_HIDDEN, KV_W), lambda i: (0, 0))],
        out_specs=pl.BlockSpec((tm, KV_W), lambda i: (i, 0)),
        out_shape=jax.ShapeDtypeStruct((m, KV_W), F32),
        compiler_params=_cparams("arbitrary"),
        name="compress",
    )(x2d, pe2, w1e, w2e)


def _compress_pool_kernel(x_ref, pe_ref, w1_ref, w2_ref, o_ref, xs, *, tp):
    stride = KV_GROUPS * HEAD_DIM
    for g in range(KV_GROUPS):
        for d in range(HEAD_DIM):
            rows = x_ref[pl.ds(g * HEAD_DIM + d, tp, stride=stride), :]
            xs[g * tp:(g + 1) * tp, d * PAGE_SIZE:(d + 1) * PAGE_SIZE] = (rows + pe_ref[d:d + 1, :]).astype(BF16)
    hid = _silu(_dot(xs[...], w1_ref[...]))
    o_ref[0] = _dot(hid.astype(BF16), w2_ref[...])


def _compress_pool(cache_t, layer, pe_t, w1t, w2e):
    n_pool = cache_t.shape[1] // (KV_GROUPS * HEAD_DIM)
    tp = 128 if n_pool % 128 == 0 else n_pool
    ns = n_pool // tp
    kdim = HEAD_DIM * PAGE_SIZE
    halves = PAGE_SIZE // BLOCK
    out = pl.pallas_call(
        functools.partial(_compress_pool_kernel, tp=tp),
        grid=(ns,),
        in_specs=[pl.BlockSpec((None, tp * KV_GROUPS * HEAD_DIM, PAGE_SIZE), lambda i: (layer, i, 0)),
                  pl.BlockSpec((HEAD_DIM, PAGE_SIZE), lambda i: (0, 0)),
                  pl.BlockSpec((kdim, halves * CMP_HIDDEN), lambda i: (0, 0)),
                  pl.BlockSpec((halves * CMP_HIDDEN, halves * HEAD_DIM), lambda i: (0, 0))],
        out_specs=pl.BlockSpec((1, KV_GROUPS * tp, halves * HEAD_DIM), lambda i: (i, 0, 0)),
        out_shape=jax.ShapeDtypeStruct((ns, KV_GROUPS * tp, halves * HEAD_DIM), F32),
        scratch_shapes=[pltpu.VMEM((KV_GROUPS * tp, kdim), BF16)],
        compiler_params=_cparams("arbitrary"),
        name="compress_pool",
    )(cache_t, pe_t, w1t, w2e)
    return out.reshape(ns * KV_GROUPS * tp, halves * HEAD_DIM), tp


def _select_topn(score, index, axis, n_index):
    work = score
    for _ in range(TOP_N):
        m = jnp.max(work, axis=axis, keepdims=True)
        first = jnp.min(jnp.where(work == m, index, n_index), axis=axis, keepdims=True)
        work = jnp.where(index == first, -jnp.inf, work)
    return jnp.where(work == -jnp.inf, 1.0, 0.0)


def _attn_kernel(qpt_ref, qrt_ref, gt_ref, kcmp_ref, vcmp_ref, ks_ref, vst_ref,
                 kwa_ref, kwb_ref, kwc_ref, vwa_ref, vwb_ref, vwc_ref, o_ref,
                 rhs, bias16, m_sc, acc, outacc):
    i = pl.program_id(0)
    nb = kcmp_ref.shape[0]
    bpt = TK // BLOCK
    hw = HEADS_PER_GROUP * TQ
    t0 = i * TQ
    t_row = t0 + lax.broadcasted_iota(jnp.int32, (1, TQ), 1)
    zeros_half = jnp.zeros((HEAD_DIM, TQ), BF16)
    wide = lambda x: jnp.concatenate([x] * HEADS_PER_GROUP, axis=1)

    def group_queries(src_ref, g):
        cols = []
        for r in range(HEADS_PER_GROUP):
            h = g * HEADS_PER_GROUP + r
            qh = src_ref[h * HEAD_DIM:(h + 1) * HEAD_DIM, :]
            cols.append(jnp.concatenate([qh, zeros_half] if g == 0 else [zeros_half, qh], axis=0))
        return jnp.concatenate(cols, axis=1)

    def gate_row(branch, g):
        base = 32 + branch * N_HEADS + g * HEADS_PER_GROUP
        return jnp.concatenate([gt_ref[base + r:base + r + 1, :] for r in range(HEADS_PER_GROUP)], axis=1)

    @pl.when(i == 0)
    def _():
        rhs[...] = jnp.zeros(rhs.shape, BF16)

    def compress_and_select(rows):
        kcmp = kcmp_ref[0:rows, :].astype(BF16)
        vcmp_t = vcmp_ref[...].T.astype(BF16)[:, 0:rows]
        jrow = lax.broadcasted_iota(jnp.int32, (rows, TQ), 0)
        mask_c1 = (jrow + 1) * BLOCK - 1 <= t_row
        mask_c = wide(mask_c1.astype(jnp.int32)) > 0
        cur = lax.shift_right_logical(t_row, 6)
        valid = jrow <= cur
        forced = valid & ((jrow == 0) | (jrow == cur) | (jrow == cur - 1))
        unused = jnp.concatenate([jnp.full((bpt, TQ), NEG, F32), jnp.zeros((16 - bpt, TQ), F32)], axis=0).astype(BF16)
        for g in range(KV_GROUPS):
            s = jnp.where(mask_c, _dot(kcmp, group_queries(qpt_ref, g)), NEG)
            m = jnp.max(s, axis=0, keepdims=True)
            e = jnp.where(mask_c, jnp.exp2(s - m), 0.0)
            p = e * (1.0 / jnp.maximum(jnp.sum(e, axis=0, keepdims=True), 1e-30))
            imp = p[:, 0:TQ]
            for r in range(1, HEADS_PER_GROUP):
                imp = imp + p[:, r * TQ:(r + 1) * TQ]
            o_c = _dot(vcmp_t, p.astype(BF16))[g * HEAD_DIM:(g + 1) * HEAD_DIM, :]
            outacc[g] = gate_row(0, g) * o_c
            score = jnp.where(forced, BIG, jnp.where(valid, imp, -BIG))
            sel = _select_topn(score, jrow, 0, rows)
            bsel = jnp.where((sel > 0.5) & (score > -0.5 * BIG), 0.0, NEG)
            for c in range(rows // bpt):
                bias16[g, c] = jnp.concatenate([bsel[c * bpt:(c + 1) * bpt, :], jnp.zeros((16 - bpt, TQ), F32)],
                                               axis=0).astype(BF16)
            for c in range(rows // bpt, nb // bpt):
                bias16[g, c] = unused

    half = nb // 2
    if half % 128 == 0 and half >= TOP_N:
        few_blocks = (t0 + TQ) // BLOCK <= half

        @pl.when(few_blocks)
        def _():
            compress_and_select(half)

        @pl.when(jnp.logical_not(few_blocks))
        def _():
            compress_and_select(nb)
    else:
        compress_and_select(nb)
    for g in range(KV_GROUPS):
        rhs[g, 0:KV_W, :] = group_queries(qrt_ref, g)

    m_sc[...] = jnp.full(m_sc.shape, NEG, F32)
    acc[...] = jnp.zeros(acc.shape, F32)
    n_kt = (t0 + TQ + TK - 1) // TK

    def key_tile(kt, diagonal, lazy):
        k0 = pl.multiple_of(kt * TK, TK)
        k_aug = ks_ref[pl.ds(k0, TK), :]
        if diagonal:
            kpos = k0 + lax.broadcasted_iota(jnp.int32, (TK, TQ), 0)
            causal = wide(jnp.where(kpos > t_row, NEG, 0.0))

        def scores(g):
            s = _dot(k_aug, rhs[g])
            return s + causal if diagonal else s

        def values(g):
            return jnp.concatenate([vst_ref[kt * (TK // TQ) + c, g] for c in range(TK // TQ)], axis=1)

        def exact_update(g):
            s = scores(g)
            m_old8 = m_sc[g]
            m_new8 = jnp.maximum(m_old8, jnp.max(s, axis=0, keepdims=True))
            m_new = m_new8[0:1, :]
            alpha = jnp.exp2(m_old8[0:1, :] - m_new)
            p = jnp.exp2(s - m_new).astype(BF16)
            acc[g] = alpha * acc[g] + _dot(values(g), p)
            m_sc[g] = m_new8

        for g in range(KV_GROUPS):
            rhs[g, KV_W:KV_W + 16, :] = wide(bias16[g, kt])
        if not lazy:
            for g in range(KV_GROUPS):
                exact_update(g)
            return
        m_old8, m_tile, pv, rise = [], [], [], []
        for g in range(KV_GROUPS):
            s = scores(g)
            m_old8.append(m_sc[g])
            p = jnp.exp2(s - m_old8[g][0:1, :]).astype(BF16)
            m_tile.append(jnp.max(s, axis=0, keepdims=True))
            pv.append(_dot(values(g), p))
            rise.append(jnp.max(m_tile[g] - m_old8[g][0:1, :]))
        safe = functools.reduce(jnp.maximum, rise) <= LAZY_MAX_RISE

        @pl.when(safe)
        def _():
            for g in range(KV_GROUPS):
                m_new8 = jnp.maximum(m_old8[g], m_tile[g])
                alpha = jnp.exp2(m_old8[g][0:1, :] - m_new8[0:1, :])
                acc[g] = alpha * (acc[g] + pv[g])
                m_sc[g] = m_new8

        @pl.when(jnp.logical_not(safe))
        def _():
            for g in range(KV_GROUPS):
                exact_update(g)

    def lazy_tile(kt, carry):
        key_tile(kt, False, True)
        return carry

    key_tile(0, True, False)
    lax.fori_loop(1, n_kt - 1, lazy_tile, 0)

    @pl.when(n_kt > 1)
    def _():
        key_tile(n_kt - 1, True, True)

    kw = jnp.concatenate([kwa_ref[...], kwb_ref[...], kwc_ref[...]], axis=0)
    wpos = (i - 2) * TQ + lax.broadcasted_iota(jnp.int32, (3 * TQ, TQ), 0)
    bias_w = wide(jnp.where((wpos >= 0) & (wpos <= t_row) & (t_row - wpos < WINDOW), 0.0, NEG))
    for g in range(KV_GROUPS):
        s = _dot(kw, rhs[g, 0:KV_W, :]) + bias_w
        m = jnp.max(s, axis=0, keepdims=True)
        p = jnp.exp2(s - m).astype(BF16)
        pv = (_dot(vwa_ref[0, g], p[0:TQ, :]) + _dot(vwb_ref[0, g], p[TQ:2 * TQ, :])
              + _dot(vwc_ref[0, g], p[2 * TQ:, :]))
        o_w = pv[0:HEAD_DIM, :] * (1.0 / pv[HEAD_DIM:HEAD_DIM + 1, :])
        a = acc[g]
        o_s = a[0:HEAD_DIM, :] * (1.0 / a[HEAD_DIM:HEAD_DIM + 1, :])
        o_g = outacc[g] + gate_row(1, g) * o_s + gate_row(2, g) * o_w
        for r in range(0, HEADS_PER_GROUP, 2):
            h = g * HEADS_PER_GROUP + r
            pair = jnp.concatenate([o_g[:, r * TQ:(r + 1) * TQ], o_g[:, (r + 1) * TQ:(r + 2) * TQ]], axis=0)
            o_ref[:, h * HEAD_DIM:(h + 2) * HEAD_DIM] = pair.T


def _attention_prompt(qpt, qrt, gt, kcmp, vcmp, ksb, vst, kwb, vwt):
    l = qpt.shape[1]
    nb = kcmp.shape[0]
    nt = l // TQ
    const2 = lambda i: (0, 0)
    blk = lambda d: (lambda i: (jnp.maximum(i - d, 0), 0))
    blk4 = lambda d: (lambda i: (jnp.maximum(i - d, 0), 0, 0, 0))
    vblk = lambda d: pl.BlockSpec((1, KV_GROUPS, V_ROWS, TQ), blk4(d))
    return pl.pallas_call(
        _attn_kernel,
        grid=(nt,),
        in_specs=[pl.BlockSpec((D_MODEL, TQ), lambda i: (0, i)),
                  pl.BlockSpec((D_MODEL, TQ), lambda i: (0, i)),
                  pl.BlockSpec((128, TQ), lambda i: (0, i)),
                  pl.BlockSpec((nb, KV_W), const2),
                  pl.BlockSpec((nb, KV_W), const2),
                  pl.BlockSpec((l, 2 * KV_W), const2),
                  pl.BlockSpec((nt, KV_GROUPS, V_ROWS, TQ), lambda i: (0, 0, 0, 0)),
                  pl.BlockSpec((TQ, KV_W), blk(2)),
                  pl.BlockSpec((TQ, KV_W), blk(1)),
                  pl.BlockSpec((TQ, KV_W), blk(0)),
                  vblk(2), vblk(1), vblk(0)],
        out_specs=pl.BlockSpec((TQ, D_MODEL), lambda i: (i, 0)),
        out_shape=jax.ShapeDtypeStruct((l, D_MODEL), F32),
        scratch_shapes=[pltpu.VMEM((KV_GROUPS, 2 * KV_W, HEADS_PER_GROUP * TQ), BF16),
                        pltpu.VMEM((KV_GROUPS, nb // (TK // BLOCK), 16, TQ), BF16),
                        pltpu.VMEM((KV_GROUPS, 8, HEADS_PER_GROUP * TQ), F32),
                        pltpu.VMEM((KV_GROUPS, V_ROWS, HEADS_PER_GROUP * TQ), F32),
                        pltpu.VMEM((KV_GROUPS, HEAD_DIM, HEADS_PER_GROUP * TQ), F32)],
        compiler_params=_cparams("arbitrary"),
        name="attention_prompt",
    )(qpt, qrt, gt, kcmp, vcmp, ksb, vst, kwb, kwb, kwb, vwt, vwt, vwt)


def _merge_kernel(x_ref, cf_ref, ssm_ref, att_ref, gm_ref, g1_ref, wa_ref, wo_ref, o_ref):
    gm = _sigmoid(gm_ref[...])
    br_att = _dot(att_ref[...].astype(BF16), wa_ref[...])
    mixed = (gm[:, :D_MODEL] * cf_ref[...] + gm[:, D_MODEL:2 * D_MODEL] * ssm_ref[...]
             + gm[:, 2 * D_MODEL:] * br_att)
    o_ref[...] = x_ref[...] + g1_ref[...] * _dot(mixed.astype(BF16), wo_ref[...])


def _merge(x, br_cf, br_ssm, att, u, g1, w_attn, w_o, tm):
    r = x.shape[0]
    mrows = tm if g1.shape[0] == r else 1
    mmap = (lambda i: (i, 0)) if g1.shape[0] == r else (lambda i: (0, 0))
    row = lambda i: (i, 0)
    const = lambda i: (0, 0)
    return pl.pallas_call(
        _merge_kernel,
        grid=(r // tm,),
        in_specs=[pl.BlockSpec((tm, D_MODEL), row)] * 4 + [
            pl.BlockSpec((tm, 3 * D_MODEL), lambda i: (i, GMIX_OFF // (3 * D_MODEL))),
            pl.BlockSpec((mrows, D_MODEL), mmap),
            pl.BlockSpec((D_MODEL, D_MODEL), const),
            pl.BlockSpec((D_MODEL, D_MODEL), const)],
        out_specs=pl.BlockSpec((tm, D_MODEL), row),
        out_shape=jax.ShapeDtypeStruct((r, D_MODEL), F32),
        compiler_params=_cparams("arbitrary"),
        name="merge",
    )(x, br_cf, br_ssm, att, u, g1, w_attn, w_o)


FF_T = D_FF // 2


def _ffn_kernel(x_ref, g_ref, sc_ref, sh_ref, g2_ref, wg_ref, wu_ref, wo_ref, o_ref, h_ref, acc_ref):
    j = pl.program_id(1)

    @pl.when(j == 0)
    def _():
        x = x_ref[...]
        y = x * lax.rsqrt(jnp.mean(x * x, axis=-1, keepdims=True) + EPS) * g_ref[...]
        h_ref[...] = (y * (1.0 + sc_ref[...]) + sh_ref[...]).astype(BF16)
        acc_ref[...] = jnp.zeros(acc_ref.shape, F32)

    h = h_ref[...]
    act = _silu(_dot(h, wg_ref[...])) * _dot(h, wu_ref[...])
    acc_ref[...] += _dot(act.astype(BF16), wo_ref[...])

    @pl.when(j == pl.num_programs(1) - 1)
    def _():
        o_ref[...] = x_ref[...] + g2_ref[...] * acc_ref[...]


def _ffn(x, g, sc, sh, g2, w_in, w_out, tm):
    r = x.shape[0]
    nj = D_FF // FF_T
    mrows = tm if sc.shape[0] == r else 1
    mmap = (lambda i, j: (i, 0)) if sc.shape[0] == r else (lambda i, j: (0, 0))
    row = lambda i, j: (i, 0)
    return pl.pallas_call(
        _ffn_kernel,
        grid=(r // tm, nj),
        in_specs=[pl.BlockSpec((tm, D_MODEL), row),
                  pl.BlockSpec((1, D_MODEL), lambda i, j: (0, 0)),
                  pl.BlockSpec((mrows, D_MODEL), mmap),
                  pl.BlockSpec((mrows, D_MODEL), mmap),
                  pl.BlockSpec((mrows, D_MODEL), mmap),
                  pl.BlockSpec((D_MODEL, FF_T), lambda i, j: (0, j)),
                  pl.BlockSpec((D_MODEL, FF_T), lambda i, j: (0, nj + j)),
                  pl.BlockSpec((FF_T, D_MODEL), lambda i, j: (j, 0))],
        out_specs=pl.BlockSpec((tm, D_MODEL), row),
        out_shape=jax.ShapeDtypeStruct((r, D_MODEL), F32),
        scratch_shapes=[pltpu.VMEM((tm, D_MODEL), BF16), pltpu.VMEM((tm, D_MODEL), F32)],
        compiler_params=_cparams("arbitrary", "arbitrary"),
        name="ffn",
    )(x, g, sc, sh, g2, w_in, w_in, w_out)


def _final_norm_kernel(x_ref, g_ref, o_ref):
    x = x_ref[...]
    o_ref[...] = x * lax.rsqrt(jnp.mean(x * x, axis=-1, keepdims=True) + EPS) * g_ref[...]


def _final_norm(x, g, tm):
    r = x.shape[0]
    return pl.pallas_call(
        _final_norm_kernel,
        grid=(r // tm,),
        in_specs=[pl.BlockSpec((tm, D_MODEL), lambda i: (i, 0)), pl.BlockSpec((1, D_MODEL), lambda i: (0, 0))],
        out_specs=pl.BlockSpec((tm, D_MODEL), lambda i: (i, 0)),
        out_shape=jax.ShapeDtypeStruct((r, D_MODEL), F32),
        compiler_params=_cparams("arbitrary"),
        name="final_norm",
    )(x, g)


def _sample_mix_kernel(ucf_ref, z_ref, xbc_ref, sm_ref, cst_ref, sst_ref, h0_ref,
                       dww_ref, dwb_ref, lg_ref, lb_ref, cw_ref, cb_ref, dtb_ref, alog_ref,
                       dsk_ref, ng_ref, rexp_ref,
                       cv_ref, yn_ref, glu_ref, hnew_ref):
    u = ucf_ref[...]
    glu = u[:, :D_MODEL] * _sigmoid(u[:, D_MODEL:])
    glu_ref[...] = glu
    conv = (dwb_ref[...] + dww_ref[CONV_K - 1:CONV_K, :] * glu
            + jnp.sum(dww_ref[0:CONV_K - 1, :] * cst_ref[...], axis=0, keepdims=True))
    cv_ref[...] = _layernorm_silu(conv, lg_ref[...], lb_ref[...])

    xbc = xbc_ref[...]
    conv = (cb_ref[...] + cw_ref[SSM_CONV_K - 1:SSM_CONV_K, :] * xbc
            + jnp.sum(cw_ref[0:SSM_CONV_K - 1, :] * sst_ref[...], axis=0, keepdims=True))
    xa = _silu(conv)
    xs = xa[:, :D_INNER]
    bm = xa[:, D_INNER:D_INNER + SSM_GROUPS * D_STATE]
    cm = xa[:, D_INNER + SSM_GROUPS * D_STATE:]
    dt = _softplus(sm_ref[...] + dtb_ref[...])
    dec = jnp.exp(dt * (-jnp.exp(alog_ref[...])))
    dtx = _dot_exact_rhs01(jnp.broadcast_to(dt, (8, 128)), rexp_ref[...])[0:1, :]
    xin = xs * dtx
    dec_col = jnp.broadcast_to(dec, (128, 128)).T
    nblk = D_INNER // 128
    xrows = jnp.concatenate([xin[:, a * 128:(a + 1) * 128] for a in range(nblk)]
                            + [jnp.zeros((128 - nblk, 128), F32)], axis=0)
    x_col = xrows.T
    blocks_per_group = (D_INNER // SSM_GROUPS) // 128
    for a in range(nblk):
        g = a // blocks_per_group
        bg = bm[:, g * D_STATE:(g + 1) * D_STATE]
        dcol = jnp.concatenate([jnp.broadcast_to(dec_col[2 * a:2 * a + 1, :], (SSM_HEADDIM, 128)),
                                jnp.broadcast_to(dec_col[2 * a + 1:2 * a + 2, :], (SSM_HEADDIM, 128))], axis=0)
        hnew_ref[a * 128:(a + 1) * 128, :] = h0_ref[a * 128:(a + 1) * 128, :] * dcol + x_col[:, a:a + 1] * bg
    c8 = jnp.concatenate([cm[:, g * D_STATE:(g + 1) * D_STATE] for g in range(SSM_GROUPS)]
                         + [jnp.zeros((8 - SSM_GROUPS, D_STATE), F32)], axis=0)
    yall = _dot_nt(c8.astype(BF16), hnew_ref[...].astype(BF16))
    lane = lax.broadcasted_iota(jnp.int32, (1, D_INNER), 1)
    gw = D_INNER // SSM_GROUPS
    y = jnp.zeros((1, D_INNER), F32)
    for g in range(SSM_GROUPS):
        y = y + jnp.where((lane >= g * gw) & (lane < (g + 1) * gw), yall[g:g + 1, :], 0.0)
    y = y + dsk_ref[...] * xs
    yn_ref[...] = _gated_group_norm(y, z_ref[...], ng_ref[...])


def _sample_mix(u3, conv_st, sconv_st, h0, dw_w, dw_b, ln_g, ln_b, conv_w, conv_b, dt_bias, a_log,
                dskip_x, norm_g, rexp):
    b = u3.shape[0]
    const = lambda i: (0, 0)
    ublk = lambda w, off: pl.BlockSpec((None, 1, w), lambda i: (i, 0, off // w))
    out1 = lambda w: pl.BlockSpec((None, 1, w), lambda i: (i, 0, 0))
    return pl.pallas_call(
        _sample_mix_kernel,
        grid=(b,),
        in_specs=[ublk(2 * D_MODEL, UCF_OFF), ublk(D_INNER, Z_OFF), ublk(SSM_CONV_DIM, XBC_OFF),
                  ublk(128, SMALL_OFF),
                  pl.BlockSpec((None, CONV_K - 1, D_MODEL), lambda i: (i, 0, 0)),
                  pl.BlockSpec((None, SSM_CONV_K - 1, SSM_CONV_DIM), lambda i: (i, 0, 0)),
                  pl.BlockSpec((None, D_INNER, D_STATE), lambda i: (i, 0, 0)),
                  pl.BlockSpec((CONV_K, D_MODEL), const),
                  pl.BlockSpec((1, D_MODEL), const),
                  pl.BlockSpec((1, D_MODEL), const),
                  pl.BlockSpec((1, D_MODEL), const),
                  pl.BlockSpec((SSM_CONV_K, SSM_CONV_DIM), const),
                  pl.BlockSpec((1, SSM_CONV_DIM), const),
                  pl.BlockSpec((1, 128), const),
                  pl.BlockSpec((1, 128), const),
                  pl.BlockSpec((1, D_INNER), const),
                  pl.BlockSpec((1, D_INNER), const),
                  pl.BlockSpec((128, D_INNER), const)],
        out_specs=[out1(D_MODEL), out1(D_INNER), out1(D_MODEL),
                   pl.BlockSpec((None, D_INNER, D_STATE), lambda i: (i, 0, 0))],
        out_shape=[jax.ShapeDtypeStruct((b, 1, D_MODEL), F32),
                   jax.ShapeDtypeStruct((b, 1, D_INNER), F32),
                   jax.ShapeDtypeStruct((b, 1, D_MODEL), F32),
                   jax.ShapeDtypeStruct((b, D_INNER, D_STATE), F32)],
        compiler_params=_cparams("arbitrary"),
        name="sample_mix",
    )(u3, u3, u3, u3, conv_st, sconv_st, h0, dw_w, dw_b, ln_g, ln_b, conv_w, conv_b,
      _pad_lanes(dt_bias), _pad_lanes(a_log), dskip_x, norm_g, rexp)


SEL_LANES = 128


def _sample_cmp_kernel(pt_ref, qlo_ref, qhi_ref, kpool_ref, vpool_ref, oc_ref, idx_ref, kbuf, vbuf,
                       *, n_pages, t_pos, tp):
    b = pl.program_id(0)
    nb = 2 * n_pages
    for pg in range(n_pages):
        page = pt_ref[b, pg]
        base = (page // tp) * (KV_GROUPS * tp) + page % tp
        for g in range(KV_GROUPS):
            kbuf[g, pg:pg + 1, :] = kpool_ref[pl.ds(base + g * tp, 1), :]
            vbuf[g, pg:pg + 1, :] = vpool_ref[pl.ds(base + g * tp, 1), :]
    lane_pg = lax.broadcasted_iota(jnp.int32, (1, n_pages), 1)
    jl = jnp.concatenate([2 * lane_pg, 2 * lane_pg + 1], axis=1)
    mask_c = (jl + 1) * BLOCK - 1 <= t_pos
    imp_rows = []
    for g in range(KV_GROUPS):
        hs = slice(g * HEADS_PER_GROUP, (g + 1) * HEADS_PER_GROUP)
        kg = kbuf[g].astype(BF16)
        vg = vbuf[g].astype(BF16)
        s = jnp.concatenate([_dot_nt(qlo_ref[hs, :], kg), _dot_nt(qhi_ref[hs, :], kg)], axis=1)
        s = jnp.where(mask_c, s, NEG)
        m = jnp.max(s, axis=1, keepdims=True)
        e = jnp.where(mask_c, jnp.exp2(s - m), 0.0)
        p = e * (1.0 / jnp.maximum(jnp.sum(e, axis=1, keepdims=True), 1e-30))
        pb = p.astype(BF16)
        oc_ref[hs, :] = (_dot(pb[:, :n_pages], vg)[:, :HEAD_DIM] + _dot(pb[:, n_pages:], vg)[:, HEAD_DIM:])
        imp_rows.append(jnp.sum(p, axis=0, keepdims=True))

    nbs = t_pos // BLOCK + 1
    extra = ((nbs - nb + 127) // 128) * 128
    width = nb + extra
    imp = jnp.concatenate(imp_rows + [jnp.zeros((8 - KV_GROUPS, nb), F32)], axis=0)
    imp = jnp.concatenate([imp, jnp.zeros((8, extra), F32)], axis=1)
    jw = jnp.broadcast_to(jnp.concatenate([jl, nb + lax.broadcasted_iota(jnp.int32, (1, extra), 1)], axis=1),
                          (8, width))
    cur = t_pos // BLOCK
    valid = jw <= cur
    forced = valid & ((jw == 0) | (jw == cur) | (jw == cur - 1))
    score = jnp.where(forced, BIG, jnp.where(valid, imp, -BIG))
    work = jnp.where(jw < nbs, score, -jnp.inf)
    lane = lax.broadcasted_iota(jnp.int32, (8, SEL_LANES), 1)
    out = jnp.full((8, SEL_LANES), -1, jnp.int32)
    for it in range(TOP_N):
        mx = jnp.max(work, axis=1, keepdims=True)
        first = jnp.min(jnp.where(work == mx, jw, width), axis=1, keepdims=True)
        out = jnp.where(lane == it, jnp.where(mx > -0.5 * BIG, first, -1), out)
        work = jnp.where(jw == first, -jnp.inf, work)
    idx_ref[...] = out


def _sample_cmp(page_table, q_lo, q_hi, kpool, vpool, t_pos, tp):
    b, n_pages = page_table.shape
    npool = kpool.shape[0]
    kern = functools.partial(_sample_cmp_kernel, n_pages=n_pages, t_pos=t_pos, tp=tp)
    return pl.pallas_call(
        kern,
        grid_spec=pltpu.PrefetchScalarGridSpec(
            num_scalar_prefetch=1,
            grid=(b,),
            in_specs=[pl.BlockSpec((None, N_HEADS, KV_W), lambda i, pt: (i, 0, 0)),
                      pl.BlockSpec((None, N_HEADS, KV_W), lambda i, pt: (i, 0, 0)),
                      pl.BlockSpec((npool, KV_W), lambda i, pt: (0, 0)),
                      pl.BlockSpec((npool, KV_W), lambda i, pt: (0, 0))],
            out_specs=[pl.BlockSpec((None, N_HEADS, HEAD_DIM), lambda i, pt: (i, 0, 0)),
                       pl.BlockSpec((None, 8, SEL_LANES), lambda i, pt: (i, 0, 0))],
            scratch_shapes=[pltpu.VMEM((KV_GROUPS, n_pages, KV_W), F32),
                            pltpu.VMEM((KV_GROUPS, n_pages, KV_W), F32)]),
        out_shape=[jax.ShapeDtypeStruct((b, N_HEADS, HEAD_DIM), F32),
                   jax.ShapeDtypeStruct((b, 8, SEL_LANES), jnp.int32)],
        compiler_params=_cparams("arbitrary"),
        name="sample_cmp",
    )(page_table, q_lo, q_hi, kpool, vpool)


def _sample_attn_kernel(idx_ref, pt_ref, q16_ref, qr_ref, new_ref, wk_ref, wv_ref, oc_ref, gate_ref,
                        kcache_ref, vcache_ref, o_ref, kbuf, vbuf, sem, *, n_pages, t_pos, layer):
    b = pl.program_id(0)
    nb_past = 2 * n_pages

    def page_copy(cache_ref, buf, g, k, which):
        j = idx_ref[b, g, k]
        jj = jnp.where((j >= 0) & (j < nb_past), j, 0)
        page = pt_ref[b, jj // 2]
        return pltpu.make_async_copy(cache_ref.at[layer, page, g],
                                     buf.at[g, :, pl.ds(k * PAGE_SIZE, PAGE_SIZE)], sem.at[which, g, k])

    for g in range(KV_GROUPS):
        for k in range(TOP_N):
            page_copy(kcache_ref, kbuf, g, k, 0).start()
            page_copy(vcache_ref, vbuf, g, k, 1).start()

    qr = qr_ref[...]
    qf = qr.astype(F32)
    new = new_ref[...]
    grp_lo = lax.broadcasted_iota(jnp.int32, (N_HEADS, KV_W), 0) < HEADS_PER_GROUP
    lane_lo = lax.broadcasted_iota(jnp.int32, (N_HEADS, KV_W), 1) < HEAD_DIM
    own = grp_lo == lane_lo

    def own_half(x):
        x = jnp.where(own, x, 0.0)
        return x[:, :HEAD_DIM] + x[:, HEAD_DIM:]

    win = wk_ref.shape[0]
    s = _dot_nt(qr, wk_ref[...].astype(BF16))
    wpos = t_pos - win + lax.broadcasted_iota(jnp.int32, (N_HEADS, win), 1)
    mask_w = t_pos - wpos < WINDOW
    s = jnp.where(mask_w, s, NEG)
    s_new = jnp.sum(qf * new[2:3, :], axis=1, keepdims=True)
    m = jnp.maximum(jnp.max(s, axis=1, keepdims=True), s_new)
    p = jnp.where(mask_w, jnp.exp2(s - m), 0.0)
    p_new = jnp.exp2(s_new - m)
    lsum = jnp.sum(p, axis=1, keepdims=True) + p_new
    o_w = own_half((_dot(p.astype(BF16), wv_ref[...].astype(BF16)) + p_new * new[3:4, :]) * (1.0 / lsum))

    for g in range(KV_GROUPS):
        for k in range(TOP_N):
            page_copy(kcache_ref, kbuf, g, k, 0).wait()
            page_copy(vcache_ref, vbuf, g, k, 1).wait()
    nsel = TOP_N * PAGE_SIZE
    lane = lax.broadcasted_iota(jnp.int32, (1, nsel), 1)
    lane_slot = lane // PAGE_SIZE
    lane_half = (lane // BLOCK) % 2
    o_s_parts = []
    for g in range(KV_GROUPS):
        ok = jnp.zeros((1, nsel), jnp.int32)
        has_new = jnp.zeros((1, 1), jnp.int32)
        for k in range(TOP_N):
            j = idx_ref[b, g, k]
            in_cache = (j >= 0) & (j < nb_past)
            hit = jnp.where(in_cache, (lane_half == j % 2).astype(jnp.int32), 0)
            ok = jnp.where(lane_slot == k, hit, ok)
            has_new = jnp.maximum(has_new, (j == nb_past).astype(jnp.int32))
        mask_s = ok > 0
        mask_n = has_new > 0
        qg = q16_ref[g * HEADS_PER_GROUP:(g + 1) * HEADS_PER_GROUP, :]
        k_new = new[0:1, g * HEAD_DIM:(g + 1) * HEAD_DIM]
        v_new = new[1:2, g * HEAD_DIM:(g + 1) * HEAD_DIM]
        s = jnp.where(mask_s, _dot(qg, kbuf[g].astype(BF16)), NEG)
        s_new = jnp.where(mask_n, jnp.sum(qg.astype(F32) * k_new, axis=1, keepdims=True), NEG)
        m = jnp.maximum(jnp.max(s, axis=1, keepdims=True), s_new)
        p = jnp.where(mask_s, jnp.exp2(s - m), 0.0)
        p_new = jnp.where(mask_n, jnp.exp2(s_new - m), 0.0)
        lsum = jnp.maximum(jnp.sum(p, axis=1, keepdims=True) + p_new, 1e-30)
        o_s_parts.append((_dot_nt(p.astype(BF16), vbuf[g].astype(BF16)) + p_new * v_new) * (1.0 / lsum))
    o_s = jnp.concatenate(o_s_parts, axis=0)

    gate = gate_ref[...]
    o_ref[...] = gate[:, 0:1] * oc_ref[...] + gate[:, 1:2] * o_s + gate[:, 2:3] * o_w


def _sample_attn(idx, page_table, q16, qr_pad, new_rows, win_k, win_v, o_c, gates, kcache, vcache, t_pos, layer):
    b, n_pages = page_table.shape
    win = win_k.shape[1]
    kern = functools.partial(_sample_attn_kernel, n_pages=n_pages, t_pos=t_pos, layer=layer)
    per_b = lambda r, c: pl.BlockSpec((None, r, c), lambda i, ix, pt: (i, 0, 0))
    return pl.pallas_call(
        kern,
        grid_spec=pltpu.PrefetchScalarGridSpec(
            num_scalar_prefetch=2,
            grid=(b,),
            in_specs=[per_b(N_HEADS, HEAD_DIM), per_b(N_HEADS, KV_W), per_b(8, KV_W), per_b(win, KV_W),
                      per_b(win, KV_W), per_b(N_HEADS, HEAD_DIM), per_b(N_HEADS, KV_W),
                      pl.BlockSpec(memory_space=pl.ANY), pl.BlockSpec(memory_space=pl.ANY)],
            out_specs=per_b(N_HEADS, HEAD_DIM),
            scratch_shapes=[pltpu.VMEM((KV_GROUPS, HEAD_DIM, TOP_N * PAGE_SIZE), F32),
                            pltpu.VMEM((KV_GROUPS, HEAD_DIM, TOP_N * PAGE_SIZE), F32),
                            pltpu.SemaphoreType.DMA((2, KV_GROUPS, TOP_N))]),
        out_shape=jax.ShapeDtypeStruct((b, N_HEADS, HEAD_DIM), F32),
        compiler_params=_cparams("arbitrary"),
        name="sample_attn",
    )(idx, page_table, q16, qr_pad, new_rows, win_k, win_v, o_c, gates, kcache, vcache)


def _pack_w_in(w):
    segs = [w[:, 4096:7168], w[:, 9040:12112], w[:, 0:2048], w[:, 2048:4096], w[:, 7200:8224],
            w[:, 8224:8992], w[:, 7168:7200], w[:, 8992:9040],
            jnp.zeros((D_MODEL, N_PACK - 12112), w.dtype)]
    return jnp.concatenate(segs, axis=1).astype(BF16)


def _expand_cmp_weights(pe, w1, w2):
    eye = jnp.eye(KV_GROUPS, dtype=F32)
    w1r = w1.reshape(BLOCK, HEAD_DIM, CMP_HIDDEN)
    w1e = jnp.einsum('ldh,gk->lgdkh', w1r, eye).reshape(BLOCK * KV_W, KV_GROUPS * CMP_HIDDEN).astype(BF16)
    w2e = jnp.einsum('hd,gk->ghkd', w2, eye).reshape(KV_GROUPS * CMP_HIDDEN, KV_W).astype(BF16)
    pe2 = jnp.broadcast_to(pe[:, None, :], (BLOCK, KV_GROUPS, HEAD_DIM)).reshape(1, BLOCK * KV_W)
    halves = PAGE_SIZE // BLOCK
    w1t = jnp.einsum('ldh,ab->dalbh', w1r, jnp.eye(halves, dtype=F32))
    w1t = w1t.reshape(HEAD_DIM * PAGE_SIZE, halves * CMP_HIDDEN).astype(BF16)
    pe_t = jnp.tile(pe.T, (1, halves))
    return (pe2, w1e, w2e), (pe_t, w1t, w2e)


def _rope_tables(pos):
    half = ROT_DIM // 2
    inv = ROPE_THETA ** (-(jnp.arange(half, dtype=F32) * 2.0 / ROT_DIM))
    e = jnp.arange(KV_W, dtype=jnp.int32) % HEAD_DIM
    ang = pos.astype(F32)[:, None] * inv[e % half][None, :]
    cos = jnp.cos(ang)
    sin = jnp.sin(ang)
    c = jnp.where(e < ROT_DIM, cos, 1.0)
    s1 = jnp.where((e >= half) & (e < ROT_DIM), sin, 0.0)
    s2 = jnp.where(e < half, -sin, 0.0)
    return jnp.stack([c, s1, s2])


def _head_expand_matrix():
    h = np.arange(128)[:, None]
    lane = np.arange(D_INNER)[None, :]
    return jnp.asarray((lane // SSM_HEADDIM == h).astype(np.float32), dtype=BF16)


def _group_padded(q):
    b = q.shape[0]
    qh = q.reshape(b, KV_GROUPS, HEADS_PER_GROUP, HEAD_DIM)
    z = jnp.zeros_like(qh[:, 0])
    lo = jnp.concatenate([qh[:, 0], z], axis=-1)
    hi = jnp.concatenate([z, qh[:, 1]], axis=-1)
    return jnp.concatenate([lo, hi], axis=1).astype(BF16)


def _layer_weights(l, p):
    w = {k: v[l] for k, v in p.items()}
    w['w_in_p'] = _pack_w_in(w['w_in'])
    for nm in ('w_cf_out', 'w_ssm_out', 'w_attn_out', 'w_o', 'w_ffn_in', 'w_ffn_out'):
        w[nm] = w[nm].astype(BF16)
    w['cmp_k'] = _expand_cmp_weights(w['cmp_pe_k'], w['cmp_w1_k'], w['cmp_w2_k'])
    w['cmp_v'] = _expand_cmp_weights(w['cmp_pe_v'], w['cmp_w1_v'], w['cmp_w2_v'])
    w['dskip_x'] = jnp.repeat(w['d_skip'], SSM_HEADDIM)[None, :]
    for nm in ('norm1_g', 'norm2_g', 'cf_dw_b', 'cf_ln_g', 'cf_ln_b', 'ssm_conv_b', 'dt_bias', 'a_log',
               'ssm_norm_g'):
        w[nm] = w[nm][None, :]
    return w


def _prompt_layer(x, mod, w, tab, rexp):
    l = x.shape[0]
    sh1, sc1, g1, sh2, sc2, g2 = [mod[:, k * D_MODEL:(k + 1) * D_MODEL] for k in range(6)]
    u = _in_proj(x, w['norm1_g'], sc1, sh1, w['w_in_p'], 1024)
    br_cf, conv_tail = _conformer_prompt(u, w['cf_dw_w'], w['cf_dw_b'], w['cf_ln_g'], w['cf_ln_b'], w['w_cf_out'])
    br_ssm, h_new, sconv_tail = _ssd_prompt(u, w['ssm_conv_w'], w['ssm_conv_b'], w['dt_bias'], w['a_log'],
                                            w['dskip_x'], w['ssm_norm_g'], rexp, w['w_ssm_out'])
    kc, vc, ks, vs, kw, vw, qpt, qrt, ksb, vst, kwb, vwt, gt = _prep_prompt(u, tab)
    kcmp = _compress(kc.reshape(l // BLOCK, BLOCK * KV_W), *w['cmp_k'][0])
    vcmp = _compress(vc.reshape(l // BLOCK, BLOCK * KV_W), *w['cmp_v'][0])
    att = _attention_prompt(qpt, qrt, gt, kcmp, vcmp, ksb, vst, kwb, vwt)
    x = _merge(x, br_cf, br_ssm, att, u, g1, w['w_attn_out'], w['w_o'], 512)
    x = _ffn(x, w['norm2_g'], sc2, sh2, g2, w['w_ffn_in'], w['w_ffn_out'], 512)
    keep = min(WINDOW, l)
    kv4 = lambda a: a.reshape(1, -1, KV_GROUPS, HEAD_DIM)
    state = (kv4(kc), kv4(vc), kv4(ks), kv4(vs), kv4(kw[l - keep:]), kv4(vw[l - keep:]),
             conv_tail[CF_HALO - (CONV_K - 1):][None],
             sconv_tail[8 - (SSM_CONV_K - 1):][None],
             h_new.reshape(1, SSM_HEADS, SSM_HEADDIM, D_STATE))
    return x, state


def _sample_layer(x, mod, w, tab, rexp, caches, layer, states, page_table, t_pos):
    b = x.shape[0]
    cache_ck, cache_cv, cache_sk, cache_sv = caches
    win_k, win_v, conv_st, sconv_st, ssm_st = states
    depth, n_pool = cache_ck.shape[0], cache_ck.shape[1]
    rows2d = lambda c: c.reshape(depth, n_pool * KV_GROUPS * HEAD_DIM, PAGE_SIZE)
    sh1, sc1, g1, sh2, sc2, g2 = [mod[:, k * D_MODEL:(k + 1) * D_MODEL] for k in range(6)]
    u = _in_proj(x, w['norm1_g'], sc1, sh1, w['w_in_p'], b)
    cv, yn, glu, h_new = _sample_mix(u.reshape(b, 1, N_PACK), conv_st, sconv_st,
                                     ssm_st.reshape(b, D_INNER, D_STATE),
                                     w['cf_dw_w'], w['cf_dw_b'], w['cf_ln_g'], w['cf_ln_b'],
                                     w['ssm_conv_w'], w['ssm_conv_b'], w['dt_bias'], w['a_log'],
                                     w['dskip_x'], w['ssm_norm_g'], rexp)
    br_cf = _matmul(cv.reshape(b, D_MODEL), w['w_cf_out'])
    br_ssm = _matmul(yn.reshape(b, D_INNER), w['w_ssm_out'])
    qp, qr, kvo = _prep_sample(u, tab)
    seg = lambda k: kvo[:, k * KV_W:(k + 1) * KV_W]
    kc, vc, ks, vs, kw, vw = [seg(k) for k in range(6)]
    gates = seg(6)[:, 32:32 + 3 * N_HEADS].reshape(b, 3, N_HEADS).transpose(0, 2, 1)
    gates = jnp.concatenate([gates, jnp.zeros((b, N_HEADS, KV_W - 3), F32)], axis=-1)
    kpool, tp = _compress_pool(rows2d(cache_ck), layer, *w['cmp_k'][1])
    vpool, _ = _compress_pool(rows2d(cache_cv), layer, *w['cmp_v'][1])
    qp16 = qp.reshape(b, N_HEADS, HEAD_DIM).astype(BF16)
    zq = jnp.zeros_like(qp16)
    o_c, idx = _sample_cmp(page_table, jnp.concatenate([qp16, zq], axis=-1), jnp.concatenate([zq, qp16], axis=-1),
                           kpool, vpool, t_pos, tp)
    new_rows = jnp.stack([ks, vs, kw, vw] + [jnp.zeros_like(ks)] * 4, axis=1)
    att = _sample_attn(idx[:, :KV_GROUPS, :TOP_N], page_table, qr.reshape(b, N_HEADS, HEAD_DIM).astype(BF16),
                       _group_padded(qr), new_rows,
                       win_k.reshape(b, -1, KV_W), win_v.reshape(b, -1, KV_W), o_c, gates,
                       cache_sk, cache_sv, t_pos, layer)
    x = _merge(x, br_cf, br_ssm, att.reshape(b, D_MODEL), u, g1, w['w_attn_out'], w['w_o'], b)
    x = _ffn(x, w['norm2_g'], sc2, sh2, g2, w['w_ffn_in'], w['w_ffn_out'], b)
    keep = min(WINDOW, t_pos + 1)
    kv4 = lambda a: a.reshape(b, 1, KV_GROUPS, HEAD_DIM)
    new_kw = jnp.concatenate([win_k, kv4(kw)], axis=1)
    new_vw = jnp.concatenate([win_v, kv4(vw)], axis=1)
    state = (kv4(kc), kv4(vc), kv4(ks), kv4(vs),
             new_kw[:, new_kw.shape[1] - keep:], new_vw[:, new_vw.shape[1] - keep:],
             jnp.concatenate([conv_st[:, 1:], glu], axis=1),
             jnp.concatenate([sconv_st[:, 1:], u[:, None, XBC_OFF:XBC_OFF + SSM_CONV_DIM]], axis=1),
             h_new.reshape(b, SSM_HEADS, SSM_HEADDIM, D_STATE))
    return x, state


def kernel(x_prompt, x_sample, c_prompt, c_sample, cache_cmp_k, cache_cmp_v, cache_slc_k, cache_slc_v, state_win_k, state_win_v, state_conv, state_ssm_conv, state_ssm, page_table, norm1_g, w_ada, b_ada, w_in, cf_dw_w, cf_dw_b, cf_ln_g, cf_ln_b, w_cf_out, ssm_conv_w, ssm_conv_b, dt_bias, a_log, d_skip, ssm_norm_g, w_ssm_out, cmp_pe_k, cmp_pe_v, cmp_w1_k, cmp_w2_k, cmp_w1_v, cmp_w2_v, w_attn_out, w_o, norm2_g, w_ffn_in, w_ffn_out, final_g):
    depth = w_in.shape[0]
    bp, lp = x_prompt.shape[0], x_prompt.shape[1]
    bs = x_sample.shape[0]
    assert bp == 1 and x_sample.shape[1] == 1
    assert lp % TK == 0 and lp // BLOCK >= TOP_N
    t_pos = page_table.shape[1] * PAGE_SIZE
    params = dict(norm1_g=norm1_g, w_in=w_in, cf_dw_w=cf_dw_w, cf_dw_b=cf_dw_b, cf_ln_g=cf_ln_g, cf_ln_b=cf_ln_b,
                  w_cf_out=w_cf_out, ssm_conv_w=ssm_conv_w, ssm_conv_b=ssm_conv_b, dt_bias=dt_bias, a_log=a_log,
                  d_skip=d_skip, ssm_norm_g=ssm_norm_g, w_ssm_out=w_ssm_out, cmp_pe_k=cmp_pe_k, cmp_pe_v=cmp_pe_v,
                  cmp_w1_k=cmp_w1_k, cmp_w2_k=cmp_w2_k, cmp_w1_v=cmp_w1_v, cmp_w2_v=cmp_w2_v,
                  w_attn_out=w_attn_out, w_o=w_o, norm2_g=norm2_g, w_ffn_in=w_ffn_in, w_ffn_out=w_ffn_out)
    n_c = bp + bs
    c_pad = (-n_c) % 8
    c_all = jnp.concatenate([c_prompt, c_sample, jnp.zeros((c_pad, D_MODEL), F32)], axis=0)
    mod = _modulation(c_all, w_ada, b_ada)
    rexp = _head_expand_matrix()
    tab_p = _rope_tables(jnp.arange(lp, dtype=jnp.int32))
    tab_s = _rope_tables(jnp.full((bs,), t_pos, jnp.int32))
    caches = tuple(jnp.transpose(c, (0, 1, 3, 4, 2)) for c in (cache_cmp_k, cache_cmp_v, cache_slc_k, cache_slc_v))

    xp = x_prompt.reshape(lp, D_MODEL)
    xs = x_sample.reshape(bs, D_MODEL)
    outs_p = [[] for _ in range(9)]
    outs_s = [[] for _ in range(9)]
    for l in range(depth):
        w = _layer_weights(l, params)
        xp, st_p = _prompt_layer(xp, mod[l, 0:bp], w, tab_p, rexp)
        states = (state_win_k[l], state_win_v[l], state_conv[l], state_ssm_conv[l], state_ssm[l])
        xs, st_s = _sample_layer(xs, mod[l, bp:bp + bs], w, tab_s, rexp, caches, l, states, page_table, t_pos)
        for k in range(9):
            outs_p[k].append(st_p[k])
            outs_s[k].append(st_s[k])
    y_prompt = _final_norm(xp, final_g[None, :], 512).reshape(bp, lp, D_MODEL)
    y_sample = _final_norm(xs, final_g[None, :], bs).reshape(bs, 1, D_MODEL)
    np_ = [jnp.stack(o) for o in outs_p]
    ns_ = [jnp.stack(o) for o in outs_s]
    return (y_prompt, y_sample, *np_, *ns_)
```

```python
import functools

import numpy as np
import jax
import jax.numpy as jnp
from jax import lax
from jax.experimental import pallas as pl
from jax.experimental.pallas import tpu as pltpu

F32 = jnp.float32
BF16 = jnp.bfloat16

D_MODEL = 1024
PAGE_SIZE = 128
CONV_K = 31
D_INNER = 2 * D_MODEL
SSM_HEADDIM = 64
SSM_HEADS = D_INNER // SSM_HEADDIM
SSM_GROUPS = 4
D_STATE = 128
SSM_CONV_K = 4
SSM_CONV_DIM = D_INNER + 2 * SSM_GROUPS * D_STATE
SSM_CHUNK = 128
N_HEADS = 16
HEAD_DIM = 64
KV_GROUPS = 2
HEADS_PER_GROUP = N_HEADS // KV_GROUPS
BLOCK = 64
TOP_N = 16
WINDOW = 512
CMP_HIDDEN = 256
ROPE_THETA = 500000.0
ROT_DIM = HEAD_DIM // 4
BIG = 1e9
NEG = -1e30
D_FF = ((8 * D_MODEL + 3 * 256 - 1) // (3 * 256)) * 256
EPS = 1e-6
KV_W = KV_GROUPS * HEAD_DIM

XBC_OFF = 0
GMIX_OFF = 3072
UCF_OFF = 6144
Z_OFF = 8192
Q_OFF = 10240
KV_OFF = 11264
SMALL_OFF = KV_OFF + 6 * KV_W
N_PACK = 12288

TQ = 256
TK = 512
V_ROWS = HEAD_DIM + 16
Q_SCALE = HEAD_DIM ** -0.5 * 1.4426950408889634
LAZY_MAX_RISE = 64.0

VMEM_LIMIT = 56 * 1024 * 1024


def _cparams(*sem):
    return pltpu.CompilerParams(dimension_semantics=sem, vmem_limit_bytes=VMEM_LIMIT)


def _dot(a, b):
    return jnp.dot(a, b, preferred_element_type=F32)


def _dot_nt(a, b):
    return lax.dot_general(a, b, (((1,), (1,)), ((), ())), preferred_element_type=F32)


def _dot_tn(a, b):
    return lax.dot_general(a, b, (((0,), (0,)), ((), ())), preferred_element_type=F32)


def _sigmoid(x):
    return jax.nn.sigmoid(x)


def _silu(x):
    return x * jax.nn.sigmoid(x)


def _softplus(x):
    return jnp.maximum(x, 0.0) + jnp.log(1.0 + jnp.exp(-jnp.abs(x)))


def _split3(x):
    hi = x.astype(BF16)
    r1 = x - hi.astype(F32)
    mid = r1.astype(BF16)
    lo = (r1 - mid.astype(F32)).astype(BF16)
    return hi, mid, lo


def _dot_exact_rhs01(x, m01):
    hi, mid, lo = _split3(x)
    return _dot(hi, m01) + _dot(mid, m01) + _dot(lo, m01)


def _dot_exact_lhs01(m01, x):
    hi, mid, lo = _split3(x)
    return _dot(m01, hi) + _dot(m01, mid) + _dot(m01, lo)


def _mod_kernel(c_ref, w_ref, b_ref, o_ref):
    c = c_ref[...]
    o_ref[0] = _dot(_silu(c).astype(BF16), w_ref[0].astype(BF16)) + b_ref[0]


def _modulation(c_all, w_ada, b_ada):
    depth = w_ada.shape[0]
    bc = c_all.shape[0]
    n = w_ada.shape[2]
    tn = 1536
    return pl.pallas_call(
        _mod_kernel,
        grid=(depth, n // tn),
        in_specs=[pl.BlockSpec((bc, D_MODEL), lambda l, j: (0, 0)),
                  pl.BlockSpec((1, D_MODEL, tn), lambda l, j: (l, 0, j)),
                  pl.BlockSpec((1, 1, tn), lambda l, j: (l, 0, j))],
        out_specs=pl.BlockSpec((1, bc, tn), lambda l, j: (l, 0, j)),
        out_shape=jax.ShapeDtypeStruct((depth, bc, n), F32),
        compiler_params=_cparams("arbitrary", "arbitrary"),
        name="modulation",
    )(c_all, w_ada, b_ada.reshape(depth, 1, n))


def _in_proj_kernel(x_ref, g_ref, sc_ref, sh_ref, w_ref, o_ref, h_ref):
    @pl.when(pl.program_id(1) == 0)
    def _():
        x = x_ref[...]
        y = x * lax.rsqrt(jnp.mean(x * x, axis=-1, keepdims=True) + EPS) * g_ref[...]
        h_ref[...] = (y * (1.0 + sc_ref[...]) + sh_ref[...]).astype(BF16)

    o_ref[...] = _dot(h_ref[...], w_ref[...])


def _in_proj(x, g, sc, sh, w, tm):
    r = x.shape[0]
    n = w.shape[1]
    tn = 1024
    mrows = tm if sc.shape[0] == r else 1
    mmap = (lambda i, j: (i, 0)) if sc.shape[0] == r else (lambda i, j: (0, 0))
    return pl.pallas_call(
        _in_proj_kernel,
        grid=(r // tm, n // tn),
        in_specs=[pl.BlockSpec((tm, D_MODEL), lambda i, j: (i, 0)),
                  pl.BlockSpec((1, D_MODEL), lambda i, j: (0, 0)),
                  pl.BlockSpec((mrows, D_MODEL), mmap),
                  pl.BlockSpec((mrows, D_MODEL), mmap),
                  pl.BlockSpec((D_MODEL, tn), lambda i, j: (0, j))],
        out_specs=pl.BlockSpec((tm, tn), lambda i, j: (i, j)),
        out_shape=jax.ShapeDtypeStruct((r, n), F32),
        scratch_shapes=[pltpu.VMEM((tm, D_MODEL), BF16)],
        compiler_params=_cparams("arbitrary", "arbitrary"),
        name="in_proj",
    )(x, g, sc, sh, w)


def _matmul_kernel(x_ref, w_ref, o_ref):
    o_ref[...] = _dot(x_ref[...].astype(BF16), w_ref[...])


def _matmul(x, w):
    r, k = x.shape
    n = w.shape[1]
    return pl.pallas_call(
        _matmul_kernel,
        grid=(1,),
        in_specs=[pl.BlockSpec((r, k), lambda i: (0, 0)), pl.BlockSpec((k, n), lambda i: (0, 0))],
        out_specs=pl.BlockSpec((r, n), lambda i: (0, 0)),
        out_shape=jax.ShapeDtypeStruct((r, n), F32),
        compiler_params=_cparams("arbitrary"),
        name="row_matmul",
    )(x, w)


CF_TL = 256
CF_RC = 32
CF_HALO = 32


def _layernorm_silu(x, g, b):
    xc = x - jnp.mean(x, axis=-1, keepdims=True)
    y = xc * lax.rsqrt(jnp.mean(xc * xc, axis=-1, keepdims=True) + EPS)
    return _silu(y * g + b)


def _conformer_kernel(u_ref, w_ref, b_ref, lg_ref, lb_ref, wo_ref, o_ref, nc_ref, bufs, cvb):
    i = pl.program_id(0)
    tl = CF_TL

    @pl.when(i == 0)
    def _():
        bufs[0, 0:CF_HALO, :] = jnp.zeros((CF_HALO, D_MODEL), F32)

    u = u_ref[...]
    bufs[0, CF_HALO:CF_HALO + tl, :] = u[:, :D_MODEL] * _sigmoid(u[:, D_MODEL:])
    for s in range(1, 8):
        bufs[s, 0:tl + 24, :] = bufs[0, s:s + tl + 24, :]

    def chunk(c, carry):
        off = pl.multiple_of(c * CF_RC, CF_RC)
        acc = jnp.broadcast_to(b_ref[...], (CF_RC, D_MODEL))
        for k in range(CONV_K):
            a, s = divmod(CF_HALO - (CONV_K - 1) + k, 8)
            acc = acc + w_ref[k:k + 1, :] * bufs[s, pl.ds(off + 8 * a, CF_RC), :]
        cvb[pl.ds(off, CF_RC), :] = _layernorm_silu(acc, lg_ref[...], lb_ref[...]).astype(BF16)
        return carry

    lax.fori_loop(0, tl // CF_RC, chunk, 0)
    tail = bufs[0, tl:tl + CF_HALO, :]
    nc_ref[...] = tail
    bufs[0, 0:CF_HALO, :] = tail
    o_ref[...] = _dot(cvb[...], wo_ref[...])


def _conformer_prompt(u, dw_w, dw_b, ln_g, ln_b, w_out):
    l = u.shape[0]
    tl = CF_TL
    wpad = jnp.concatenate([dw_w, jnp.zeros((1, D_MODEL), F32)], axis=0)
    return pl.pallas_call(
        _conformer_kernel,
        grid=(l // tl,),
        in_specs=[pl.BlockSpec((tl, 2 * D_MODEL), lambda i: (i, UCF_OFF // (2 * D_MODEL))),
                  pl.BlockSpec((CONV_K + 1, D_MODEL), lambda i: (0, 0)),
                  pl.BlockSpec((1, D_MODEL), lambda i: (0, 0)),
                  pl.BlockSpec((1, D_MODEL), lambda i: (0, 0)),
                  pl.BlockSpec((1, D_MODEL), lambda i: (0, 0)),
                  pl.BlockSpec((D_MODEL, D_MODEL), lambda i: (0, 0))],
        out_specs=[pl.BlockSpec((tl, D_MODEL), lambda i: (i, 0)),
                   pl.BlockSpec((CF_HALO, D_MODEL), lambda i: (0, 0))],
        out_shape=[jax.ShapeDtypeStruct((l, D_MODEL), F32),
                   jax.ShapeDtypeStruct((CF_HALO, D_MODEL), F32)],
        scratch_shapes=[pltpu.VMEM((8, tl + CF_HALO, D_MODEL), F32),
                        pltpu.VMEM((tl, D_MODEL), BF16)],
        compiler_params=_cparams("arbitrary"),
        name="conformer_prompt",
    )(u, wpad, dw_b, ln_g, ln_b, w_out)


def _gated_group_norm(y, z, g):
    y = y * _silu(z)
    gw = D_INNER // SSM_GROUPS
    parts = []
    for k in range(SSM_GROUPS):
        yg = y[:, k * gw:(k + 1) * gw]
        parts.append(yg * lax.rsqrt(jnp.mean(yg * yg, axis=-1, keepdims=True) + EPS))
    return jnp.concatenate(parts, axis=1) * g


def _ssd_kernel(xbc_ref, z_ref, sm_ref, cw_ref, cb_ref, dtb_ref, alog_ref, dsk_ref, ng_ref, rexp_ref,
                wo_ref, o_ref, hout_ref, sc_ref, cbuf, hst):
    i = pl.program_id(0)
    q = SSM_CHUNK

    @pl.when(i == 0)
    def _():
        cbuf[0:8, :] = jnp.zeros((8, SSM_CONV_DIM), F32)
        hst[...] = jnp.zeros(hst.shape, F32)

    x = xbc_ref[...]
    cbuf[8:8 + q, :] = x
    conv = cb_ref[...] + cw_ref[3:4, :] * x
    for k in range(SSM_CONV_K - 1):
        conv = conv + cw_ref[k:k + 1, :] * cbuf[5 + k:5 + k + q, :]
    tail = cbuf[q:q + 8, :]
    sc_ref[...] = tail
    cbuf[0:8, :] = tail
    xa = _silu(conv)
    xs = xa[:, :D_INNER]
    bm = xa[:, D_INNER:D_INNER + SSM_GROUPS * D_STATE]
    cm = xa[:, D_INNER + SSM_GROUPS * D_STATE:]

    dt = _softplus(sm_ref[...] + dtb_ref[...])
    a = dt * (-jnp.exp(alog_ref[...]))
    row = lax.broadcasted_iota(jnp.int32, (q, q), 0)
    col = lax.broadcasted_iota(jnp.int32, (q, q), 1)
    tri = row >= col
    a_cum = _dot_exact_lhs01(tri.astype(BF16), a)
    a_cum_t = a_cum.T
    a_last = a_cum[q - 1:q, :]
    rexp = rexp_ref[...]
    dtx = _dot_exact_rhs01(dt, rexp)
    eax = _dot_exact_rhs01(jnp.exp(a_cum), rexp)
    decx = _dot_exact_rhs01(jnp.exp(a_last - a_cum), rexp)
    xin = xs * dtx
    xdec = (xin * decx).astype(BF16)
    xin_b = xin.astype(BF16)
    cd_col = jnp.broadcast_to(jnp.exp(a_last), (q, q)).T
    lane_lo = lax.broadcasted_iota(jnp.int32, (q, 2 * SSM_HEADDIM), 1) < SSM_HEADDIM

    hpg = SSM_HEADS // SSM_GROUPS
    gw = hpg * SSM_HEADDIM
    y_groups = []
    for g in range(SSM_GROUPS):
        cg = cm[:, g * D_STATE:(g + 1) * D_STATE].astype(BF16)
        bg = bm[:, g * D_STATE:(g + 1) * D_STATE].astype(BF16)
        cb = _dot_nt(cg, bg)
        hprev = hst[g * gw:(g + 1) * gw, :]
        y_off = _dot_nt(cg, hprev.astype(BF16)) * eax[:, g * gw:(g + 1) * gw]
        st = _dot_tn(xdec[:, g * gw:(g + 1) * gw], bg)
        pair_parts = []
        for pr in range(hpg // 2):
            xp = xin_b[:, g * gw + pr * 128:g * gw + (pr + 1) * 128]
            yp = None
            for sub in range(2):
                h = g * hpg + pr * 2 + sub
                seg = a_cum[:, h:h + 1] - a_cum_t[h:h + 1, :]
                lm = jnp.exp(jnp.where(tri, seg, -jnp.inf))
                mh = (cb * lm).astype(BF16)
                xm = jnp.where(lane_lo if sub == 0 else jnp.logical_not(lane_lo), xp, jnp.zeros_like(xp))
                d = _dot(mh, xm)
                yp = d if yp is None else yp + d
                r0 = h * SSM_HEADDIM
                lo = (pr * 2 + sub) * SSM_HEADDIM
                hst[r0:r0 + SSM_HEADDIM, :] = (hprev[lo:lo + SSM_HEADDIM, :] * cd_col[h:h + 1, :]
                                              + st[lo:lo + SSM_HEADDIM, :])
            pair_parts.append(yp)
        y_groups.append(jnp.concatenate(pair_parts, axis=1) + y_off)
    y = jnp.concatenate(y_groups, axis=1) + dsk_ref[...] * xs
    yn = _gated_group_norm(y, z_ref[...], ng_ref[...])
    o_ref[...] = _dot(yn.astype(BF16), wo_ref[...])

    @pl.when(i == pl.num_programs(0) - 1)
    def _():
        hout_ref[...] = hst[...]


def _pad_lanes(v, n=128):
    return jnp.concatenate([v, jnp.zeros((v.shape[0], n - v.shape[1]), v.dtype)], axis=1)


def _ssd_prompt(u, conv_w, conv_b, dt_bias, a_log, dskip_x, norm_g, rexp, w_out):
    l = u.shape[0]
    q = SSM_CHUNK
    cw = jnp.concatenate([conv_w, jnp.zeros((4, SSM_CONV_DIM), F32)], axis=0)
    const = lambda i: (0, 0)
    return pl.pallas_call(
        _ssd_kernel,
        grid=(l // q,),
        in_specs=[pl.BlockSpec((q, SSM_CONV_DIM), lambda i: (i, XBC_OFF // SSM_CONV_DIM)),
                  pl.BlockSpec((q, D_INNER), lambda i: (i, Z_OFF // D_INNER)),
                  pl.BlockSpec((q, 128), lambda i: (i, SMALL_OFF // 128)),
                  pl.BlockSpec((8, SSM_CONV_DIM), const),
                  pl.BlockSpec((1, SSM_CONV_DIM), const),
                  pl.BlockSpec((1, 128), const),
                  pl.BlockSpec((1, 128), const),
                  pl.BlockSpec((1, D_INNER), const),
                  pl.BlockSpec((1, D_INNER), const),
                  pl.BlockSpec((128, D_INNER), const),
                  pl.BlockSpec((D_INNER, D_MODEL), const)],
        out_specs=[pl.BlockSpec((q, D_MODEL), lambda i: (i, 0)),
                   pl.BlockSpec((D_INNER, D_STATE), const),
                   pl.BlockSpec((8, SSM_CONV_DIM), const)],
        out_shape=[jax.ShapeDtypeStruct((l, D_MODEL), F32),
                   jax.ShapeDtypeStruct((D_INNER, D_STATE), F32),
                   jax.ShapeDtypeStruct((8, SSM_CONV_DIM), F32)],
        scratch_shapes=[pltpu.VMEM((q + 8, SSM_CONV_DIM), F32),
                        pltpu.VMEM((D_INNER, D_STATE), F32)],
        compiler_params=_cparams("arbitrary"),
        name="ssd_prompt",
    )(u, u, u, cw, conv_b, _pad_lanes(dt_bias), _pad_lanes(a_log), dskip_x, norm_g, rexp, w_out)


def _rope(x, c, s1, s2):
    n = x.shape[1]
    return x * c + pltpu.roll(x, ROT_DIM // 2, 1) * s1 + pltpu.roll(x, n - ROT_DIM // 2, 1) * s2


def _aug_vt(v):
    vt = v.T.astype(BF16)
    ones = jnp.ones((V_ROWS - HEAD_DIM, v.shape[0]), BF16)
    return [jnp.concatenate([vt[g * HEAD_DIM:(g + 1) * HEAD_DIM, :], ones], axis=0) for g in range(KV_GROUPS)]


def _prep_kernel(q_ref, kv_ref, tab_ref, kc_ref, vc_ref, ks_ref, vs_ref, kw_ref, vw_ref,
                 qpt_ref, qrt_ref, ksb_ref, vst_ref, kwb_ref, vwt_ref, gt_ref):
    i = pl.program_id(0)
    tl = q_ref.shape[0]
    c = tab_ref[0]
    s1 = tab_ref[1]
    s2 = tab_ref[2]
    reps = D_MODEL // 128
    q = q_ref[...] * Q_SCALE
    qr = _rope(q, jnp.concatenate([c] * reps, axis=1), jnp.concatenate([s1] * reps, axis=1),
               jnp.concatenate([s2] * reps, axis=1))
    qpt_ref[...] = q.T.astype(BF16)
    qrt_ref[...] = qr.T.astype(BF16)
    kv = kv_ref[...]
    kc_ref[...] = kv[:, 0:KV_W]
    vc_ref[...] = kv[:, KV_W:2 * KV_W]
    ks = _rope(kv[:, 2 * KV_W:3 * KV_W], c, s1, s2)
    ks_ref[...] = ks
    row = lax.broadcasted_iota(jnp.int32, (tl, KV_W), 0)
    lane = lax.broadcasted_iota(jnp.int32, (tl, KV_W), 1)
    blk_in_tile = lax.shift_right_logical((i * tl) % TK + row, 6)
    ksb_ref[...] = jnp.concatenate([ks, jnp.where(lane == blk_in_tile, 1.0, 0.0)], axis=1).astype(BF16)
    vs = kv[:, 3 * KV_W:4 * KV_W]
    vs_ref[...] = vs
    kw = _rope(kv[:, 4 * KV_W:5 * KV_W], c, s1, s2)
    kw_ref[...] = kw
    kwb_ref[...] = kw.astype(BF16)
    vw = kv[:, 5 * KV_W:6 * KV_W]
    vw_ref[...] = vw
    for g, (a, b) in enumerate(zip(_aug_vt(vs), _aug_vt(vw))):
        vst_ref[0, g] = a
        vwt_ref[0, g] = b
    gt_ref[...] = _sigmoid(kv[:, 6 * KV_W:7 * KV_W]).T


def _prep_prompt(u, tab):
    l = u.shape[0]
    tl = TQ
    nt = l // tl
    row = lambda i: (i, 0)
    f32s = jax.ShapeDtypeStruct((l, KV_W), F32)
    vspec = pl.BlockSpec((1, KV_GROUPS, V_ROWS, tl), lambda i: (i, 0, 0, 0))
    vshape = jax.ShapeDtypeStruct((nt, KV_GROUPS, V_ROWS, tl), BF16)
    return pl.pallas_call(
        _prep_kernel,
        grid=(nt,),
        in_specs=[pl.BlockSpec((tl, D_MODEL), lambda i: (i, Q_OFF // D_MODEL)),
                  pl.BlockSpec((tl, D_MODEL), lambda i: (i, KV_OFF // D_MODEL)),
                  pl.BlockSpec((3, tl, 128), lambda i: (0, i, 0))],
        out_specs=[pl.BlockSpec((tl, KV_W), row)] * 6 + [
            pl.BlockSpec((D_MODEL, tl), lambda i: (0, i)),
            pl.BlockSpec((D_MODEL, tl), lambda i: (0, i)),
            pl.BlockSpec((tl, 2 * KV_W), row),
            vspec,
            pl.BlockSpec((tl, KV_W), row),
            vspec,
            pl.BlockSpec((128, tl), lambda i: (0, i))],
        out_shape=[f32s] * 6 + [
            jax.ShapeDtypeStruct((D_MODEL, l), BF16),
            jax.ShapeDtypeStruct((D_MODEL, l), BF16),
            jax.ShapeDtypeStruct((l, 2 * KV_W), BF16),
            vshape,
            jax.ShapeDtypeStruct((l, KV_W), BF16),
            vshape,
            jax.ShapeDtypeStruct((128, l), F32)],
        compiler_params=_cparams("arbitrary"),
        name="prep_prompt",
    )(u, u, tab)


def _prep_sample_kernel(q_ref, kv_ref, tab_ref, qp_ref, qr_ref, kvo_ref):
    c = tab_ref[0]
    s1 = tab_ref[1]
    s2 = tab_ref[2]
    reps = D_MODEL // 128
    q = q_ref[...] * Q_SCALE
    qp_ref[...] = q
    qr_ref[...] = _rope(q, jnp.concatenate([c] * reps, axis=1), jnp.concatenate([s1] * reps, axis=1),
                        jnp.concatenate([s2] * reps, axis=1))
    kv = kv_ref[...]
    one = jnp.ones_like(c)
    zero = jnp.zeros_like(c)
    ckv = jnp.concatenate([one, one, c, one, c, one, one, one], axis=1)
    s1kv = jnp.concatenate([zero, zero, s1, zero, s1, zero, zero, zero], axis=1)
    s2kv = jnp.concatenate([zero, zero, s2, zero, s2, zero, zero, zero], axis=1)
    kvr = _rope(kv, ckv, s1kv, s2kv)
    lane = lax.broadcasted_iota(jnp.int32, kv.shape, 1)
    kvo_ref[...] = jnp.where((lane >= 6 * KV_W) & (lane < 7 * KV_W), _sigmoid(kv), kvr)


def _prep_sample(u, tab):
    b = u.shape[0]
    full = lambda i: (0, 0)
    return pl.pallas_call(
        _prep_sample_kernel,
        grid=(1,),
        in_specs=[pl.BlockSpec((b, D_MODEL), lambda i: (0, Q_OFF // D_MODEL)),
                  pl.BlockSpec((b, D_MODEL), lambda i: (0, KV_OFF // D_MODEL)),
                  pl.BlockSpec((3, b, 128), lambda i: (0, 0, 0))],
        out_specs=[pl.BlockSpec((b, D_MODEL), full)] * 3,
        out_shape=[jax.ShapeDtypeStruct((b, D_MODEL), F32)] * 3,
        compiler_params=_cparams("arbitrary"),
        name="prep_sample",
    )(u, u, tab)


def _compress_kernel(x_ref, pe_ref, w1_ref, w2_ref, o_ref):
    x = (x_ref[...] + pe_ref[...]).astype(BF16)
    hid = _silu(_dot(x, w1_ref[...]))
    o_ref[...] = _dot(hid.astype(BF16), w2_ref[...])


def _compress(x2d, pe2, w1e, w2e):
    m = x2d.shape[0]
    tm = 256 if m % 256 == 0 else m
    kdim = BLOCK * KV_W
    return pl.pallas_call(
        _compress_kernel,
        grid=(m // tm,),
        in_specs=[pl.BlockSpec((tm, kdim), lambda i: (i, 0)),
                  pl.BlockSpec((1, kdim), lambda i: (0, 0)),
                  pl.BlockSpec((kdim, KV_GROUPS * CMP_HIDDEN), lambda i: (0, 0)),
                  pl.BlockSpec((KV_GROUPS * CMP_HIDDEN, KV_W), lambda i: (0, 0))],
        out_specs=pl.BlockSpec((tm, KV_W), lambda i: (i, 0)),
        out_shape=jax.ShapeDtypeStruct((m, KV_W), F32),
        compiler_params=_cparams("arbitrary"),
        name="compress",
    )(x2d, pe2, w1e, w2e)


def _compress_pool_kernel(x_ref, pe_ref, w1_ref, w2_ref, o_ref, xs, *, tp):
    stride = KV_GROUPS * HEAD_DIM
    for g in range(KV_GROUPS):
        for d in range(HEAD_DIM):
            rows = x_ref[pl.ds(g * HEAD_DIM + d, tp, stride=stride), :]
            xs[g * tp:(g + 1) * tp, d * PAGE_SIZE:(d + 1) * PAGE_SIZE] = (rows + pe_ref[d:d + 1, :]).astype(BF16)
    hid = _silu(_dot(xs[...], w1_ref[...]))
    o_ref[0] = _dot(hid.astype(BF16), w2_ref[...])


def _compress_pool(cache_t, layer, pe_t, w1t, w2e):
    n_pool = cache_t.shape[1] // (KV_GROUPS * HEAD_DIM)
    tp = 128 if n_pool % 128 == 0 else n_pool
    ns = n_pool // tp
    kdim = HEAD_DIM * PAGE_SIZE
    halves = PAGE_SIZE // BLOCK
    out = pl.pallas_call(
        functools.partial(_compress_pool_kernel, tp=tp),
        grid=(ns,),
        in_specs=[pl.BlockSpec((None, tp * KV_GROUPS * HEAD_DIM, PAGE_SIZE), lambda i: (layer, i, 0)),
                  pl.BlockSpec((HEAD_DIM, PAGE_SIZE), lambda i: (0, 0)),
                  pl.BlockSpec((kdim, halves * CMP_HIDDEN), lambda i: (0, 0)),
                  pl.BlockSpec((halves * CMP_HIDDEN, halves * HEAD_DIM), lambda i: (0, 0))],
        out_specs=pl.BlockSpec((1, KV_GROUPS * tp, halves * HEAD_DIM), lambda i: (i, 0, 0)),
        out_shape=jax.ShapeDtypeStruct((ns, KV_GROUPS * tp, halves * HEAD_DIM), F32),
        scratch_shapes=[pltpu.VMEM((KV_GROUPS * tp, kdim), BF16)],
        compiler_params=_cparams("arbitrary"),
        name="compress_pool",
    )(cache_t, pe_t, w1t, w2e)
    return out.reshape(ns * KV_GROUPS * tp, halves * HEAD_DIM), tp


def _select_topn(score, index, axis, n_index):
    work = score
    for _ in range(TOP_N):
        m = jnp.max(work, axis=axis, keepdims=True)
        first = jnp.min(jnp.where(work == m, index, n_index), axis=axis, keepdims=True)
        work = jnp.where(index == first, -jnp.inf, work)
    return jnp.where(work == -jnp.inf, 1.0, 0.0)


def _attn_kernel(qpt_ref, qrt_ref, gt_ref, kcmp_ref, vcmp_ref, ks_ref, vst_ref,
                 kwa_ref, kwb_ref, kwc_ref, vwa_ref, vwb_ref, vwc_ref, o_ref,
                 rhs, bias16, m_sc, acc, outacc):
    i = pl.program_id(0)
    nb = kcmp_ref.shape[0]
    bpt = TK // BLOCK
    hw = HEADS_PER_GROUP * TQ
    t0 = i * TQ
    t_row = t0 + lax.broadcasted_iota(jnp.int32, (1, TQ), 1)
    zeros_half = jnp.zeros((HEAD_DIM, TQ), BF16)
    wide = lambda x: jnp.concatenate([x] * HEADS_PER_GROUP, axis=1)

    def group_queries(src_ref, g):
        cols = []
        for r in range(HEADS_PER_GROUP):
            h = g * HEADS_PER_GROUP + r
            qh = src_ref[h * HEAD_DIM:(h + 1) * HEAD_DIM, :]
            cols.append(jnp.concatenate([qh, zeros_half] if g == 0 else [zeros_half, qh], axis=0))
        return jnp.concatenate(cols, axis=1)

    def gate_row(branch, g):
        base = 32 + branch * N_HEADS + g * HEADS_PER_GROUP
        return jnp.concatenate([gt_ref[base + r:base + r + 1, :] for r in range(HEADS_PER_GROUP)], axis=1)

    @pl.when(i == 0)
    def _():
        rhs[...] = jnp.zeros(rhs.shape, BF16)

    def compress_and_select(rows):
        kcmp = kcmp_ref[0:rows, :].astype(BF16)
        vcmp_t = vcmp_ref[...].T.astype(BF16)[:, 0:rows]
        jrow = lax.broadcasted_iota(jnp.int32, (rows, TQ), 0)
        mask_c1 = (jrow + 1) * BLOCK - 1 <= t_row
        mask_c = wide(mask_c1.astype(jnp.int32)) > 0
        cur = lax.shift_right_logical(t_row, 6)
        valid = jrow <= cur
        forced = valid & ((jrow == 0) | (jrow == cur) | (jrow == cur - 1))
        unused = jnp.concatenate([jnp.full((bpt, TQ), NEG, F32), jnp.zeros((16 - bpt, TQ), F32)], axis=0).astype(BF16)
        for g in range(KV_GROUPS):
            s = jnp.where(mask_c, _dot(kcmp, group_queries(qpt_ref, g)), NEG)
            m = jnp.max(s, axis=0, keepdims=True)
            e = jnp.where(mask_c, jnp.exp2(s - m), 0.0)
            p = e * (1.0 / jnp.maximum(jnp.sum(e, axis=0, keepdims=True), 1e-30))
            imp = p[:, 0:TQ]
            for r in range(1, HEADS_PER_GROUP):
                imp = imp + p[:, r * TQ:(r + 1) * TQ]
            o_c = _dot(vcmp_t, p.astype(BF16))[g * HEAD_DIM:(g + 1) * HEAD_DIM, :]
            outacc[g] = gate_row(0, g) * o_c
            score = jnp.where(forced, BIG, jnp.where(valid, imp, -BIG))
            sel = _select_topn(score, jrow, 0, rows)
            bsel = jnp.where((sel > 0.5) & (score > -0.5 * BIG), 0.0, NEG)
            for c in range(rows // bpt):
                bias16[g, c] = jnp.concatenate([bsel[c * bpt:(c + 1) * bpt, :], jnp.zeros((16 - bpt, TQ), F32)],
                                               axis=0).astype(BF16)
            for c in range(rows // bpt, nb // bpt):
                bias16[g, c] = unused

    half = nb // 2
    if half % 128 == 0 and half >= TOP_N:
        few_blocks = (t0 + TQ) // BLOCK <= half

        @pl.when(few_blocks)
        def _():
            compress_and_select(half)

        @pl.when(jnp.logical_not(few_blocks))
        def _():
            compress_and_select(nb)
    else:
        compress_and_select(nb)
    for g in range(KV_GROUPS):
        rhs[g, 0:KV_W, :] = group_queries(qrt_ref, g)

    m_sc[...] = jnp.full(m_sc.shape, NEG, F32)
    acc[...] = jnp.zeros(acc.shape, F32)
    n_kt = (t0 + TQ + TK - 1) // TK

    def key_tile(kt, diagonal, lazy):
        k0 = pl.multiple_of(kt * TK, TK)
        k_aug = ks_ref[pl.ds(k0, TK), :]
        if diagonal:
            kpos = k0 + lax.broadcasted_iota(jnp.int32, (TK, TQ), 0)
            causal = wide(jnp.where(kpos > t_row, NEG, 0.0))

        def scores(g):
            s = _dot(k_aug, rhs[g])
            return s + causal if diagonal else s

        def values(g):
            return jnp.concatenate([vst_ref[kt * (TK // TQ) + c, g] for c in range(TK // TQ)], axis=1)

        def exact_update(g):
            s = scores(g)
            m_old8 = m_sc[g]
            m_new8 = jnp.maximum(m_old8, jnp.max(s, axis=0, keepdims=True))
            m_new = m_new8[0:1, :]
            alpha = jnp.exp2(m_old8[0:1, :] - m_new)
            p = jnp.exp2(s - m_new).astype(BF16)
            acc[g] = alpha * acc[g] + _dot(values(g), p)
            m_sc[g] = m_new8

        for g in range(KV_GROUPS):
            rhs[g, KV_W:KV_W + 16, :] = wide(bias16[g, kt])
        if not lazy:
            for g in range(KV_GROUPS):
                exact_update(g)
            return
        m_old8, m_tile, pv, rise = [], [], [], []
        for g in range(KV_GROUPS):
            s = scores(g)
            m_old8.append(m_sc[g])
            p = jnp.exp2(s - m_old8[g][0:1, :]).astype(BF16)
            m_tile.append(jnp.max(s, axis=0, keepdims=True))
            pv.append(_dot(values(g), p))
            rise.append(jnp.max(m_tile[g] - m_old8[g][0:1, :]))
        safe = functools.reduce(jnp.maximum, rise) <= LAZY_MAX_RISE

        @pl.when(safe)
        def _():
            for g in range(KV_GROUPS):
                m_new8 = jnp.maximum(m_old8[g], m_tile[g])
                alpha = jnp.exp2(m_old8[g][0:1, :] - m_new8[0:1, :])
                acc[g] = alpha * (acc[g] + pv[g])
                m_sc[g] = m_new8

        @pl.when(jnp.logical_not(safe))
        def _():
            for g in range(KV_GROUPS):
                exact_update(g)

    def lazy_tile(kt, carry):
        key_tile(kt, False, True)
        return carry

    key_tile(0, True, False)
    lax.fori_loop(1, n_kt - 1, lazy_tile, 0)

    @pl.when(n_kt > 1)
    def _():
        key_tile(n_kt - 1, True, True)

    kw = jnp.concatenate([kwa_ref[...], kwb_ref[...], kwc_ref[...]], axis=0)
    wpos = (i - 2) * TQ + lax.broadcasted_iota(jnp.int32, (3 * TQ, TQ), 0)
    bias_w = wide(jnp.where((wpos >= 0) & (wpos <= t_row) & (t_row - wpos < WINDOW), 0.0, NEG))
    for g in range(KV_GROUPS):
        s = _dot(kw, rhs[g, 0:KV_W, :]) + bias_w
        m = jnp.max(s, axis=0, keepdims=True)
        p = jnp.exp2(s - m).astype(BF16)
        pv = (_dot(vwa_ref[0, g], p[0:TQ, :]) + _dot(vwb_ref[0, g], p[TQ:2 * TQ, :])
              + _dot(vwc_ref[0, g], p[2 * TQ:, :]))
        o_w = pv[0:HEAD_DIM, :] * (1.0 / pv[HEAD_DIM:HEAD_DIM + 1, :])
        a = acc[g]
        o_s = a[0:HEAD_DIM, :] * (1.0 / a[HEAD_DIM:HEAD_DIM + 1, :])
        o_g = outacc[g] + gate_row(1, g) * o_s + gate_row(2, g) * o_w
        for r in range(0, HEADS_PER_GROUP, 2):
            h = g * HEADS_PER_GROUP + r
            pair = jnp.concatenate([o_g[:, r * TQ:(r + 1) * TQ], o_g[:, (r + 1) * TQ:(r + 2) * TQ]], axis=0)
            o_ref[:, h * HEAD_DIM:(h + 2) * HEAD_DIM] = pair.T


def _attention_prompt(qpt, qrt, gt, kcmp, vcmp, ksb, vst, kwb, vwt):
    l = qpt.shape[1]
    nb = kcmp.shape[0]
    nt = l // TQ
    const2 = lambda i: (0, 0)
    blk = lambda d: (lambda i: (jnp.maximum(i - d, 0), 0))
    blk4 = lambda d: (lambda i: (jnp.maximum(i - d, 0), 0, 0, 0))
    vblk = lambda d: pl.BlockSpec((1, KV_GROUPS, V_ROWS, TQ), blk4(d))
    return pl.pallas_call(
        _attn_kernel,
        grid=(nt,),
        in_specs=[pl.BlockSpec((D_MODEL, TQ), lambda i: (0, i)),
                  pl.BlockSpec((D_MODEL, TQ), lambda i: (0, i)),
                  pl.BlockSpec((128, TQ), lambda i: (0, i)),
                  pl.BlockSpec((nb, KV_W), const2),
                  pl.BlockSpec((nb, KV_W), const2),
                  pl.BlockSpec((l, 2 * KV_W), const2),
                  pl.BlockSpec((nt, KV_GROUPS, V_ROWS, TQ), lambda i: (0, 0, 0, 0)),
                  pl.BlockSpec((TQ, KV_W), blk(2)),
                  pl.BlockSpec((TQ, KV_W), blk(1)),
                  pl.BlockSpec((TQ, KV_W), blk(0)),
                  vblk(2), vblk(1), vblk(0)],
        out_specs=pl.BlockSpec((TQ, D_MODEL), lambda i: (i, 0)),
        out_shape=jax.ShapeDtypeStruct((l, D_MODEL), F32),
        scratch_shapes=[pltpu.VMEM((KV_GROUPS, 2 * KV_W, HEADS_PER_GROUP * TQ), BF16),
                        pltpu.VMEM((KV_GROUPS, nb // (TK // BLOCK), 16, TQ), BF16),
                        pltpu.VMEM((KV_GROUPS, 8, HEADS_PER_GROUP * TQ), F32),
                        pltpu.VMEM((KV_GROUPS, V_ROWS, HEADS_PER_GROUP * TQ), F32),
                        pltpu.VMEM((KV_GROUPS, HEAD_DIM, HEADS_PER_GROUP * TQ), F32)],
        compiler_params=_cparams("arbitrary"),
        name="attention_prompt",
    )(qpt, qrt, gt, kcmp, vcmp, ksb, vst, kwb, kwb, kwb, vwt, vwt, vwt)


def _merge_kernel(x_ref, cf_ref, ssm_ref, att_ref, gm_ref, g1_ref, wa_ref, wo_ref, o_ref):
    gm = _sigmoid(gm_ref[...])
    br_att = _dot(att_ref[...].astype(BF16), wa_ref[...])
    mixed = (gm[:, :D_MODEL] * cf_ref[...] + gm[:, D_MODEL:2 * D_MODEL] * ssm_ref[...]
             + gm[:, 2 * D_MODEL:] * br_att)
    o_ref[...] = x_ref[...] + g1_ref[...] * _dot(mixed.astype(BF16), wo_ref[...])


def _merge(x, br_cf, br_ssm, att, u, g1, w_attn, w_o, tm):
    r = x.shape[0]
    mrows = tm if g1.shape[0] == r else 1
    mmap = (lambda i: (i, 0)) if g1.shape[0] == r else (lambda i: (0, 0))
    row = lambda i: (i, 0)
    const = lambda i: (0, 0)
    return pl.pallas_call(
        _merge_kernel,
        grid=(r // tm,),
        in_specs=[pl.BlockSpec((tm, D_MODEL), row)] * 4 + [
            pl.BlockSpec((tm, 3 * D_MODEL), lambda i: (i, GMIX_OFF // (3 * D_MODEL))),
            pl.BlockSpec((mrows, D_MODEL), mmap),
            pl.BlockSpec((D_MODEL, D_MODEL), const),
            pl.BlockSpec((D_MODEL, D_MODEL), const)],
        out_specs=pl.BlockSpec((tm, D_MODEL), row),
        out_shape=jax.ShapeDtypeStruct((r, D_MODEL), F32),
        compiler_params=_cparams("arbitrary"),
        name="merge",
    )(x, br_cf, br_ssm, att, u, g1, w_attn, w_o)


FF_T = D_FF // 2


def _ffn_kernel(x_ref, g_ref, sc_ref, sh_ref, g2_ref, wg_ref, wu_ref, wo_ref, o_ref, h_ref, acc_ref):
    j = pl.program_id(1)

    @pl.when(j == 0)
    def _():
        x = x_ref[...]
        y = x * lax.rsqrt(jnp.mean(x * x, axis=-1, keepdims=True) + EPS) * g_ref[...]
        h_ref[...] = (y * (1.0 + sc_ref[...]) + sh_ref[...]).astype(BF16)
        acc_ref[...] = jnp.zeros(acc_ref.shape, F32)

    h = h_ref[...]
    act = _silu(_dot(h, wg_ref[...])) * _dot(h, wu_ref[...])
    acc_ref[...] += _dot(act.astype(BF16), wo_ref[...])

    @pl.when(j == pl.num_programs(1) - 1)
    def _():
        o_ref[...] = x_ref[...] + g2_ref[...] * acc_ref[...]


def _ffn(x, g, sc, sh, g2, w_in, w_out, tm):
    r = x.shape[0]
    nj = D_FF // FF_T
    mrows = tm if sc.shape[0] == r else 1
    mmap = (lambda i, j: (i, 0)) if sc.shape[0] == r else (lambda i, j: (0, 0))
    row = lambda i, j: (i, 0)
    return pl.pallas_call(
        _ffn_kernel,
        grid=(r // tm, nj),
        in_specs=[pl.BlockSpec((tm, D_MODEL), row),
                  pl.BlockSpec((1, D_MODEL), lambda i, j: (0, 0)),
                  pl.BlockSpec((mrows, D_MODEL), mmap),
                  pl.BlockSpec((mrows, D_MODEL), mmap),
                  pl.BlockSpec((mrows, D_MODEL), mmap),
                  pl.BlockSpec((D_MODEL, FF_T), lambda i, j: (0, j)),
                  pl.BlockSpec((D_MODEL, FF_T), lambda i, j: (0, nj + j)),
                  pl.BlockSpec((FF_T, D_MODEL), lambda i, j: (j, 0))],
        out_specs=pl.BlockSpec((tm, D_MODEL), row),
        out_shape=jax.ShapeDtypeStruct((r, D_MODEL), F32),
        scratch_shapes=[pltpu.VMEM((tm, D_MODEL), BF16), pltpu.VMEM((tm, D_MODEL), F32)],
        compiler_params=_cparams("arbitrary", "arbitrary"),
        name="ffn",
    )(x, g, sc, sh, g2, w_in, w_in, w_out)


def _final_norm_kernel(x_ref, g_ref, o_ref):
    x = x_ref[...]
    o_ref[...] = x * lax.rsqrt(jnp.mean(x * x, axis=-1, keepdims=True) + EPS) * g_ref[...]


def _final_norm(x, g, tm):
    r = x.shape[0]
    return pl.pallas_call(
        _final_norm_kernel,
        grid=(r // tm,),
        in_specs=[pl.BlockSpec((tm, D_MODEL), lambda i: (i, 0)), pl.BlockSpec((1, D_MODEL), lambda i: (0, 0))],
        out_specs=pl.BlockSpec((tm, D_MODEL), lambda i: (i, 0)),
        out_shape=jax.ShapeDtypeStruct((r, D_MODEL), F32),
        compiler_params=_cparams("arbitrary"),
        name="final_norm",
    )(x, g)


def _sample_mix_kernel(ucf_ref, z_ref, xbc_ref, sm_ref, cst_ref, sst_ref, h0_ref,
                       dww_ref, dwb_ref, lg_ref, lb_ref, cw_ref, cb_ref, dtb_ref, alog_ref,
                       dsk_ref, ng_ref, rexp_ref,
                       cv_ref, yn_ref, glu_ref, hnew_ref):
    u = ucf_ref[...]
    glu = u[:, :D_MODEL] * _sigmoid(u[:, D_MODEL:])
    glu_ref[...] = glu
    conv = (dwb_ref[...] + dww_ref[CONV_K - 1:CONV_K, :] * glu
            + jnp.sum(dww_ref[0:CONV_K - 1, :] * cst_ref[...], axis=0, keepdims=True))
    cv_ref[...] = _layernorm_silu(conv, lg_ref[...], lb_ref[...])

    xbc = xbc_ref[...]
    conv = (cb_ref[...] + cw_ref[SSM_CONV_K - 1:SSM_CONV_K, :] * xbc
            + jnp.sum(cw_ref[0:SSM_CONV_K - 1, :] * sst_ref[...], axis=0, keepdims=True))
    xa = _silu(conv)
    xs = xa[:, :D_INNER]
    bm = xa[:, D_INNER:D_INNER + SSM_GROUPS * D_STATE]
    cm = xa[:, D_INNER + SSM_GROUPS * D_STATE:]
    dt = _softplus(sm_ref[...] + dtb_ref[...])
    dec = jnp.exp(dt * (-jnp.exp(alog_ref[...])))
    dtx = _dot_exact_rhs01(jnp.broadcast_to(dt, (8, 128)), rexp_ref[...])[0:1, :]
    xin = xs * dtx
    dec_col = jnp.broadcast_to(dec, (128, 128)).T
    nblk = D_INNER // 128
    xrows = jnp.concatenate([xin[:, a * 128:(a + 1) * 128] for a in range(nblk)]
                            + [jnp.zeros((128 - nblk, 128), F32)], axis=0)
    x_col = xrows.T
    blocks_per_group = (D_INNER // SSM_GROUPS) // 128
    for a in range(nblk):
        g = a // blocks_per_group
        bg = bm[:, g * D_STATE:(g + 1) * D_STATE]
        dcol = jnp.concatenate([jnp.broadcast_to(dec_col[2 * a:2 * a + 1, :], (SSM_HEADDIM, 128)),
                                jnp.broadcast_to(dec_col[2 * a + 1:2 * a + 2, :], (SSM_HEADDIM, 128))], axis=0)
        hnew_ref[a * 128:(a + 1) * 128, :] = h0_ref[a * 128:(a + 1) * 128, :] * dcol + x_col[:, a:a + 1] * bg
    c8 = jnp.concatenate([cm[:, g * D_STATE:(g + 1) * D_STATE] for g in range(SSM_GROUPS)]
                         + [jnp.zeros((8 - SSM_GROUPS, D_STATE), F32)], axis=0)
    yall = _dot_nt(c8.astype(BF16), hnew_ref[...].astype(BF16))
    lane = lax.broadcasted_iota(jnp.int32, (1, D_INNER), 1)
    gw = D_INNER // SSM_GROUPS
    y = jnp.zeros((1, D_INNER), F32)
    for g in range(SSM_GROUPS):
        y = y + jnp.where((lane >= g * gw) & (lane < (g + 1) * gw), yall[g:g + 1, :], 0.0)
    y = y + dsk_ref[...] * xs
    yn_ref[...] = _gated_group_norm(y, z_ref[...], ng_ref[...])


def _sample_mix(u3, conv_st, sconv_st, h0, dw_w, dw_b, ln_g, ln_b, conv_w, conv_b, dt_bias, a_log,
                dskip_x, norm_g, rexp):
    b = u3.shape[0]
    const = lambda i: (0, 0)
    ublk = lambda w, off: pl.BlockSpec((None, 1, w), lambda i: (i, 0, off // w))
    out1 = lambda w: pl.BlockSpec((None, 1, w), lambda i: (i, 0, 0))
    return pl.pallas_call(
        _sample_mix_kernel,
        grid=(b,),
        in_specs=[ublk(2 * D_MODEL, UCF_OFF), ublk(D_INNER, Z_OFF), ublk(SSM_CONV_DIM, XBC_OFF),
                  ublk(128, SMALL_OFF),
                  pl.BlockSpec((None, CONV_K - 1, D_MODEL), lambda i: (i, 0, 0)),
                  pl.BlockSpec((None, SSM_CONV_K - 1, SSM_CONV_DIM), lambda i: (i, 0, 0)),
                  pl.BlockSpec((None, D_INNER, D_STATE), lambda i: (i, 0, 0)),
                  pl.BlockSpec((CONV_K, D_MODEL), const),
                  pl.BlockSpec((1, D_MODEL), const),
                  pl.BlockSpec((1, D_MODEL), const),
                  pl.BlockSpec((1, D_MODEL), const),
                  pl.BlockSpec((SSM_CONV_K, SSM_CONV_DIM), const),
                  pl.BlockSpec((1, SSM_CONV_DIM), const),
                  pl.BlockSpec((1, 128), const),
                  pl.BlockSpec((1, 128), const),
                  pl.BlockSpec((1, D_INNER), const),
                  pl.BlockSpec((1, D_INNER), const),
                  pl.BlockSpec((128, D_INNER), const)],
        out_specs=[out1(D_MODEL), out1(D_INNER), out1(D_MODEL),
                   pl.BlockSpec((None, D_INNER, D_STATE), lambda i: (i, 0, 0))],
        out_shape=[jax.ShapeDtypeStruct((b, 1, D_MODEL), F32),
                   jax.ShapeDtypeStruct((b, 1, D_INNER), F32),
                   jax.ShapeDtypeStruct((b, 1, D_MODEL), F32),
                   jax.ShapeDtypeStruct((b, D_INNER, D_STATE), F32)],
        compiler_params=_cparams("arbitrary"),
        name="sample_mix",
    )(u3, u3, u3, u3, conv_st, sconv_st, h0, dw_w, dw_b, ln_g, ln_b, conv_w, conv_b,
      _pad_lanes(dt_bias), _pad_lanes(a_log), dskip_x, norm_g, rexp)


SEL_LANES = 128


def _sample_cmp_kernel(pt_ref, qlo_ref, qhi_ref, kvpool_ref, oc_ref, idx_ref, kvbuf, *, n_pages, t_pos, tp):
    b = pl.program_id(0)
    nb = 2 * n_pages
    for pg in range(n_pages):
        page = pt_ref[b, pg]
        base = (page // tp) * (KV_GROUPS * tp) + page % tp
        for g in range(KV_GROUPS):
            kvbuf[g, pg:pg + 1, :] = kvpool_ref[pl.ds(base + g * tp, 1), :]
    lane_pg = lax.broadcasted_iota(jnp.int32, (1, n_pages), 1)
    jl = jnp.concatenate([2 * lane_pg, 2 * lane_pg + 1], axis=1)
    mask_c = (jl + 1) * BLOCK - 1 <= t_pos
    imp_rows = []
    for g in range(KV_GROUPS):
        hs = slice(g * HEADS_PER_GROUP, (g + 1) * HEADS_PER_GROUP)
        kg = kvbuf[g, :, 0:KV_W].astype(BF16)
        vg = kvbuf[g, :, KV_W:2 * KV_W].astype(BF16)
        s = jnp.concatenate([_dot_nt(qlo_ref[hs, :], kg), _dot_nt(qhi_ref[hs, :], kg)], axis=1)
        s = jnp.where(mask_c, s, NEG)
        m = jnp.max(s, axis=1, keepdims=True)
        e = jnp.where(mask_c, jnp.exp2(s - m), 0.0)
        p = e * (1.0 / jnp.maximum(jnp.sum(e, axis=1, keepdims=True), 1e-30))
        pb = p.astype(BF16)
        oc_ref[hs, :] = (_dot(pb[:, :n_pages], vg)[:, :HEAD_DIM] + _dot(pb[:, n_pages:], vg)[:, HEAD_DIM:])
        imp_rows.append(jnp.sum(p, axis=0, keepdims=True))

    nbs = t_pos // BLOCK + 1
    extra = ((nbs - nb + 127) // 128) * 128
    width = nb + extra
    imp = jnp.concatenate(imp_rows + [jnp.zeros((SEL_LANES - KV_GROUPS, nb), F32)], axis=0)
    imp = jnp.concatenate([imp, jnp.zeros((SEL_LANES, extra), F32)], axis=1).T
    r = lax.broadcasted_iota(jnp.int32, (width, SEL_LANES), 0)
    jw = jnp.where(r < n_pages, 2 * r, jnp.where(r < nb, 2 * (r - n_pages) + 1, r))
    cur = t_pos // BLOCK
    valid = jw <= cur
    forced = valid & ((jw == 0) | (jw == cur) | (jw == cur - 1))
    score = jnp.where(forced, BIG, jnp.where(valid, imp, -BIG))
    work = jnp.where(jw < nbs, score, -jnp.inf)
    pick_row = lax.broadcasted_iota(jnp.int32, (TOP_N, SEL_LANES), 0)
    out = jnp.full((TOP_N, SEL_LANES), -1, jnp.int32)
    for it in range(TOP_N):
        mx = jnp.max(work, axis=0, keepdims=True)
        first = jnp.min(jnp.where(work == mx, jw, width), axis=0, keepdims=True)
        out = jnp.where(pick_row == it, jnp.where(mx > -0.5 * BIG, first, -1), out)
        work = jnp.where(jw == first, -jnp.inf, work)
    idx_ref[...] = out


def _sample_cmp(page_table, q_lo, q_hi, kvpool, t_pos, tp):
    b, n_pages = page_table.shape
    npool = kvpool.shape[0]
    kern = functools.partial(_sample_cmp_kernel, n_pages=n_pages, t_pos=t_pos, tp=tp)
    return pl.pallas_call(
        kern,
        grid_spec=pltpu.PrefetchScalarGridSpec(
            num_scalar_prefetch=1,
            grid=(b,),
            in_specs=[pl.BlockSpec((None, N_HEADS, KV_W), lambda i, pt: (i, 0, 0)),
                      pl.BlockSpec((None, N_HEADS, KV_W), lambda i, pt: (i, 0, 0)),
                      pl.BlockSpec((npool, 2 * KV_W), lambda i, pt: (0, 0))],
            out_specs=[pl.BlockSpec((None, N_HEADS, HEAD_DIM), lambda i, pt: (i, 0, 0)),
                       pl.BlockSpec((None, TOP_N, SEL_LANES), lambda i, pt: (i, 0, 0))],
            scratch_shapes=[pltpu.VMEM((KV_GROUPS, n_pages, 2 * KV_W), F32)]),
        out_shape=[jax.ShapeDtypeStruct((b, N_HEADS, HEAD_DIM), F32),
                   jax.ShapeDtypeStruct((b, TOP_N, SEL_LANES), jnp.int32)],
        compiler_params=_cparams("arbitrary"),
        name="sample_cmp",
    )(page_table, q_lo, q_hi, kvpool)


def _sample_attn_kernel(idx_ref, pt_ref, q16_ref, qr_ref, new_ref, wk_ref, wv_ref, oc_ref, gate_ref,
                        kcache_ref, vcache_ref, o_ref, kbuf, vbuf, sem, *, n_pages, t_pos, layer):
    b = pl.program_id(0)
    nb_past = 2 * n_pages

    def page_copy(cache_ref, buf, g, k, which):
        j = idx_ref[b, g, k]
        jj = jnp.where((j >= 0) & (j < nb_past), j, 0)
        page = pt_ref[b, jj // 2]
        return pltpu.make_async_copy(cache_ref.at[layer, page, g],
                                     buf.at[g, :, pl.ds(k * PAGE_SIZE, PAGE_SIZE)], sem.at[which, g, k])

    for g in range(KV_GROUPS):
        for k in range(TOP_N):
            page_copy(kcache_ref, kbuf, g, k, 0).start()
            page_copy(vcache_ref, vbuf, g, k, 1).start()

    qr = qr_ref[...]
    qf = qr.astype(F32)
    new = new_ref[...]
    grp_lo = lax.broadcasted_iota(jnp.int32, (N_HEADS, KV_W), 0) < HEADS_PER_GROUP
    lane_lo = lax.broadcasted_iota(jnp.int32, (N_HEADS, KV_W), 1) < HEAD_DIM
    own = grp_lo == lane_lo

    def own_half(x):
        x = jnp.where(own, x, 0.0)
        return x[:, :HEAD_DIM] + x[:, HEAD_DIM:]

    win = wk_ref.shape[0]
    s = _dot_nt(qr, wk_ref[...].astype(BF16))
    wpos = t_pos - win + lax.broadcasted_iota(jnp.int32, (N_HEADS, win), 1)
    mask_w = t_pos - wpos < WINDOW
    s = jnp.where(mask_w, s, NEG)
    s_new = jnp.sum(qf * new[2:3, :], axis=1, keepdims=True)
    m = jnp.maximum(jnp.max(s, axis=1, keepdims=True), s_new)
    p = jnp.where(mask_w, jnp.exp2(s - m), 0.0)
    p_new = jnp.exp2(s_new - m)
    lsum = jnp.sum(p, axis=1, keepdims=True) + p_new
    o_w = own_half((_dot(p.astype(BF16), wv_ref[...].astype(BF16)) + p_new * new[3:4, :]) * (1.0 / lsum))

    for g in range(KV_GROUPS):
        for k in range(TOP_N):
            page_copy(kcache_ref, kbuf, g, k, 0).wait()
            page_copy(vcache_ref, vbuf, g, k, 1).wait()
    nsel = TOP_N * PAGE_SIZE
    lane = lax.broadcasted_iota(jnp.int32, (1, nsel), 1)
    lane_slot = lane // PAGE_SIZE
    lane_half = (lane // BLOCK) % 2
    o_s_parts = []
    for g in range(KV_GROUPS):
        ok = jnp.zeros((1, nsel), jnp.int32)
        has_new = jnp.zeros((1, 1), jnp.int32)
        for k in range(TOP_N):
            j = idx_ref[b, g, k]
            in_cache = (j >= 0) & (j < nb_past)
            hit = jnp.where(in_cache, (lane_half == j % 2).astype(jnp.int32), 0)
            ok = jnp.where(lane_slot == k, hit, ok)
            has_new = jnp.maximum(has_new, (j == nb_past).astype(jnp.int32))
        mask_s = ok > 0
        mask_n = has_new > 0
        qg = q16_ref[g * HEADS_PER_GROUP:(g + 1) * HEADS_PER_GROUP, :]
        k_new = new[0:1, g * HEAD_DIM:(g + 1) * HEAD_DIM]
        v_new = new[1:2, g * HEAD_DIM:(g + 1) * HEAD_DIM]
        s = jnp.where(mask_s, _dot(qg, kbuf[g].astype(BF16)), NEG)
        s_new = jnp.where(mask_n, jnp.sum(qg.astype(F32) * k_new, axis=1, keepdims=True), NEG)
        m = jnp.maximum(jnp.max(s, axis=1, keepdims=True), s_new)
        p = jnp.where(mask_s, jnp.exp2(s - m), 0.0)
        p_new = jnp.where(mask_n, jnp.exp2(s_new - m), 0.0)
        lsum = jnp.maximum(jnp.sum(p, axis=1, keepdims=True) + p_new, 1e-30)
        o_s_parts.append((_dot_nt(p.astype(BF16), vbuf[g].astype(BF16)) + p_new * v_new) * (1.0 / lsum))
    o_s = jnp.concatenate(o_s_parts, axis=0)

    gate = gate_ref[...]
    o_ref[...] = gate[:, 0:1] * oc_ref[...] + gate[:, 1:2] * o_s + gate[:, 2:3] * o_w


def _sample_attn(idx, page_table, q16, qr_pad, new_rows, win_k, win_v, o_c, gates, kcache, vcache, t_pos, layer):
    b, n_pages = page_table.shape
    win = win_k.shape[1]
    kern = functools.partial(_sample_attn_kernel, n_pages=n_pages, t_pos=t_pos, layer=layer)
    per_b = lambda r, c: pl.BlockSpec((None, r, c), lambda i, ix, pt: (i, 0, 0))
    return pl.pallas_call(
        kern,
        grid_spec=pltpu.PrefetchScalarGridSpec(
            num_scalar_prefetch=2,
            grid=(b,),
            in_specs=[per_b(N_HEADS, HEAD_DIM), per_b(N_HEADS, KV_W), per_b(8, KV_W), per_b(win, KV_W),
                      per_b(win, KV_W), per_b(N_HEADS, HEAD_DIM), per_b(N_HEADS, KV_W),
                      pl.BlockSpec(memory_space=pl.ANY), pl.BlockSpec(memory_space=pl.ANY)],
            out_specs=per_b(N_HEADS, HEAD_DIM),
            scratch_shapes=[pltpu.VMEM((KV_GROUPS, HEAD_DIM, TOP_N * PAGE_SIZE), F32),
                            pltpu.VMEM((KV_GROUPS, HEAD_DIM, TOP_N * PAGE_SIZE), F32),
                            pltpu.SemaphoreType.DMA((2, KV_GROUPS, TOP_N))]),
        out_shape=jax.ShapeDtypeStruct((b, N_HEADS, HEAD_DIM), F32),
        compiler_params=_cparams("arbitrary"),
        name="sample_attn",
    )(idx, page_table, q16, qr_pad, new_rows, win_k, win_v, o_c, gates, kcache, vcache)


def _pack_w_in(w):
    segs = [w[:, 4096:7168], w[:, 9040:12112], w[:, 0:2048], w[:, 2048:4096], w[:, 7200:8224],
            w[:, 8224:8992], w[:, 7168:7200], w[:, 8992:9040],
            jnp.zeros((D_MODEL, N_PACK - 12112), w.dtype)]
    return jnp.concatenate(segs, axis=1).astype(BF16)


def _expand_cmp_weights(pe, w1, w2):
    eye = jnp.eye(KV_GROUPS, dtype=F32)
    w1r = w1.reshape(BLOCK, HEAD_DIM, CMP_HIDDEN)
    w1e = jnp.einsum('ldh,gk->lgdkh', w1r, eye).reshape(BLOCK * KV_W, KV_GROUPS * CMP_HIDDEN).astype(BF16)
    w2e = jnp.einsum('hd,gk->ghkd', w2, eye).reshape(KV_GROUPS * CMP_HIDDEN, KV_W).astype(BF16)
    pe2 = jnp.broadcast_to(pe[:, None, :], (BLOCK, KV_GROUPS, HEAD_DIM)).reshape(1, BLOCK * KV_W)
    halves = PAGE_SIZE // BLOCK
    w1t = jnp.einsum('ldh,ab->dalbh', w1r, jnp.eye(halves, dtype=F32))
    w1t = w1t.reshape(HEAD_DIM * PAGE_SIZE, halves * CMP_HIDDEN).astype(BF16)
    pe_t = jnp.tile(pe.T, (1, halves))
    return (pe2, w1e, w2e), (pe_t, w1t, w2e)


def _rope_tables(pos):
    half = ROT_DIM // 2
    inv = ROPE_THETA ** (-(jnp.arange(half, dtype=F32) * 2.0 / ROT_DIM))
    e = jnp.arange(KV_W, dtype=jnp.int32) % HEAD_DIM
    ang = pos.astype(F32)[:, None] * inv[e % half][None, :]
    cos = jnp.cos(ang)
    sin = jnp.sin(ang)
    c = jnp.where(e < ROT_DIM, cos, 1.0)
    s1 = jnp.where((e >= half) & (e < ROT_DIM), sin, 0.0)
    s2 = jnp.where(e < half, -sin, 0.0)
    return jnp.stack([c, s1, s2])


def _head_expand_matrix():
    h = np.arange(128)[:, None]
    lane = np.arange(D_INNER)[None, :]
    return jnp.asarray((lane // SSM_HEADDIM == h).astype(np.float32), dtype=BF16)


def _group_padded(q):
    b = q.shape[0]
    qh = q.reshape(b, KV_GROUPS, HEADS_PER_GROUP, HEAD_DIM)
    z = jnp.zeros_like(qh[:, 0])
    lo = jnp.concatenate([qh[:, 0], z], axis=-1)
    hi = jnp.concatenate([z, qh[:, 1]], axis=-1)
    return jnp.concatenate([lo, hi], axis=1).astype(BF16)


def _layer_weights(l, p):
    w = {k: v[l] for k, v in p.items()}
    w['w_in_p'] = _pack_w_in(w['w_in'])
    for nm in ('w_cf_out', 'w_ssm_out', 'w_attn_out', 'w_o', 'w_ffn_in', 'w_ffn_out'):
        w[nm] = w[nm].astype(BF16)
    w['cmp_k'] = _expand_cmp_weights(w['cmp_pe_k'], w['cmp_w1_k'], w['cmp_w2_k'])
    w['cmp_v'] = _expand_cmp_weights(w['cmp_pe_v'], w['cmp_w1_v'], w['cmp_w2_v'])
    w['dskip_x'] = jnp.repeat(w['d_skip'], SSM_HEADDIM)[None, :]
    for nm in ('norm1_g', 'norm2_g', 'cf_dw_b', 'cf_ln_g', 'cf_ln_b', 'ssm_conv_b', 'dt_bias', 'a_log',
               'ssm_norm_g'):
        w[nm] = w[nm][None, :]
    return w


def _prompt_layer(x, mod, w, tab, rexp):
    l = x.shape[0]
    sh1, sc1, g1, sh2, sc2, g2 = [mod[:, k * D_MODEL:(k + 1) * D_MODEL] for k in range(6)]
    u = _in_proj(x, w['norm1_g'], sc1, sh1, w['w_in_p'], 2048)
    br_cf, conv_tail = _conformer_prompt(u, w['cf_dw_w'], w['cf_dw_b'], w['cf_ln_g'], w['cf_ln_b'], w['w_cf_out'])
    br_ssm, h_new, sconv_tail = _ssd_prompt(u, w['ssm_conv_w'], w['ssm_conv_b'], w['dt_bias'], w['a_log'],
                                            w['dskip_x'], w['ssm_norm_g'], rexp, w['w_ssm_out'])
    kc, vc, ks, vs, kw, vw, qpt, qrt, ksb, vst, kwb, vwt, gt = _prep_prompt(u, tab)
    kcmp = _compress(kc.reshape(l // BLOCK, BLOCK * KV_W), *w['cmp_k'][0])
    vcmp = _compress(vc.reshape(l // BLOCK, BLOCK * KV_W), *w['cmp_v'][0])
    att = _attention_prompt(qpt, qrt, gt, kcmp, vcmp, ksb, vst, kwb, vwt)
    x = _merge(x, br_cf, br_ssm, att, u, g1, w['w_attn_out'], w['w_o'], 512)
    x = _ffn(x, w['norm2_g'], sc2, sh2, g2, w['w_ffn_in'], w['w_ffn_out'], 1024)
    keep = min(WINDOW, l)
    kv4 = lambda a: a.reshape(1, -1, KV_GROUPS, HEAD_DIM)
    state = (kv4(kc), kv4(vc), kv4(ks), kv4(vs), kv4(kw[l - keep:]), kv4(vw[l - keep:]),
             conv_tail[CF_HALO - (CONV_K - 1):][None],
             sconv_tail[8 - (SSM_CONV_K - 1):][None],
             h_new.reshape(1, SSM_HEADS, SSM_HEADDIM, D_STATE))
    return x, state


def _sample_layer(x, mod, w, tab, rexp, caches, layer, states, page_table, t_pos):
    b = x.shape[0]
    cache_ck, cache_cv, cache_sk, cache_sv = caches
    win_k, win_v, conv_st, sconv_st, ssm_st = states
    depth, n_pool = cache_ck.shape[0], cache_ck.shape[1]
    rows2d = lambda c: c.reshape(depth, n_pool * KV_GROUPS * HEAD_DIM, PAGE_SIZE)
    sh1, sc1, g1, sh2, sc2, g2 = [mod[:, k * D_MODEL:(k + 1) * D_MODEL] for k in range(6)]
    u = _in_proj(x, w['norm1_g'], sc1, sh1, w['w_in_p'], b)
    cv, yn, glu, h_new = _sample_mix(u.reshape(b, 1, N_PACK), conv_st, sconv_st,
                                     ssm_st.reshape(b, D_INNER, D_STATE),
                                     w['cf_dw_w'], w['cf_dw_b'], w['cf_ln_g'], w['cf_ln_b'],
                                     w['ssm_conv_w'], w['ssm_conv_b'], w['dt_bias'], w['a_log'],
                                     w['dskip_x'], w['ssm_norm_g'], rexp)
    br_cf = _matmul(cv.reshape(b, D_MODEL), w['w_cf_out'])
    br_ssm = _matmul(yn.reshape(b, D_INNER), w['w_ssm_out'])
    qp, qr, kvo = _prep_sample(u, tab)
    seg = lambda k: kvo[:, k * KV_W:(k + 1) * KV_W]
    kc, vc, ks, vs, kw, vw = [seg(k) for k in range(6)]
    gates = seg(6)[:, 32:32 + 3 * N_HEADS].reshape(b, 3, N_HEADS).transpose(0, 2, 1)
    gates = jnp.concatenate([gates, jnp.zeros((b, N_HEADS, KV_W - 3), F32)], axis=-1)
    kpool, tp = _compress_pool(rows2d(cache_ck), layer, *w['cmp_k'][1])
    vpool, _ = _compress_pool(rows2d(cache_cv), layer, *w['cmp_v'][1])
    qp16 = qp.reshape(b, N_HEADS, HEAD_DIM).astype(BF16)
    zq = jnp.zeros_like(qp16)
    o_c, idx = _sample_cmp(page_table, jnp.concatenate([qp16, zq], axis=-1), jnp.concatenate([zq, qp16], axis=-1),
                           jnp.concatenate([kpool, vpool], axis=1), t_pos, tp)
    new_rows = jnp.stack([ks, vs, kw, vw] + [jnp.zeros_like(ks)] * 4, axis=1)
    att = _sample_attn(idx[:, :, :KV_GROUPS].transpose(0, 2, 1), page_table, qr.reshape(b, N_HEADS, HEAD_DIM).astype(BF16),
                       _group_padded(qr), new_rows,
                       win_k.reshape(b, -1, KV_W), win_v.reshape(b, -1, KV_W), o_c, gates,
                       cache_sk, cache_sv, t_pos, layer)
    x = _merge(x, br_cf, br_ssm, att.reshape(b, D_MODEL), u, g1, w['w_attn_out'], w['w_o'], b)
    x = _ffn(x, w['norm2_g'], sc2, sh2, g2, w['w_ffn_in'], w['w_ffn_out'], b)
    keep = min(WINDOW, t_pos + 1)
    kv4 = lambda a: a.reshape(b, 1, KV_GROUPS, HEAD_DIM)
    new_kw = jnp.concatenate([win_k, kv4(kw)], axis=1)
    new_vw = jnp.concatenate([win_v, kv4(vw)], axis=1)
    state = (kv4(kc), kv4(vc), kv4(ks), kv4(vs),
             new_kw[:, new_kw.shape[1] - keep:], new_vw[:, new_vw.shape[1] - keep:],
             jnp.concatenate([conv_st[:, 1:], glu], axis=1),
             jnp.concatenate([sconv_st[:, 1:], u[:, None, XBC_OFF:XBC_OFF + SSM_CONV_DIM]], axis=1),
             h_new.reshape(b, SSM_HEADS, SSM_HEADDIM, D_STATE))
    return x, state


def kernel(x_prompt, x_sample, c_prompt, c_sample, cache_cmp_k, cache_cmp_v, cache_slc_k, cache_slc_v, state_win_k, state_win_v, state_conv, state_ssm_conv, state_ssm, page_table, norm1_g, w_ada, b_ada, w_in, cf_dw_w, cf_dw_b, cf_ln_g, cf_ln_b, w_cf_out, ssm_conv_w, ssm_conv_b, dt_bias, a_log, d_skip, ssm_norm_g, w_ssm_out, cmp_pe_k, cmp_pe_v, cmp_w1_k, cmp_w2_k, cmp_w1_v, cmp_w2_v, w_attn_out, w_o, norm2_g, w_ffn_in, w_ffn_out, final_g):
    depth = w_in.shape[0]
    bp, lp = x_prompt.shape[0], x_prompt.shape[1]
    bs = x_sample.shape[0]
    assert bp == 1 and x_sample.shape[1] == 1
    assert lp % TK == 0 and lp // BLOCK >= TOP_N
    t_pos = page_table.shape[1] * PAGE_SIZE
    params = dict(norm1_g=norm1_g, w_in=w_in, cf_dw_w=cf_dw_w, cf_dw_b=cf_dw_b, cf_ln_g=cf_ln_g, cf_ln_b=cf_ln_b,
                  w_cf_out=w_cf_out, ssm_conv_w=ssm_conv_w, ssm_conv_b=ssm_conv_b, dt_bias=dt_bias, a_log=a_log,
                  d_skip=d_skip, ssm_norm_g=ssm_norm_g, w_ssm_out=w_ssm_out, cmp_pe_k=cmp_pe_k, cmp_pe_v=cmp_pe_v,
                  cmp_w1_k=cmp_w1_k, cmp_w2_k=cmp_w2_k, cmp_w1_v=cmp_w1_v, cmp_w2_v=cmp_w2_v,
                  w_attn_out=w_attn_out, w_o=w_o, norm2_g=norm2_g, w_ffn_in=w_ffn_in, w_ffn_out=w_ffn_out)
    n_c = bp + bs
    c_pad = (-n_c) % 8
    c_all = jnp.concatenate([c_prompt, c_sample, jnp.zeros((c_pad, D_MODEL), F32)], axis=0)
    mod = _modulation(c_all, w_ada, b_ada)
    rexp = _head_expand_matrix()
    tab_p = _rope_tables(jnp.arange(lp, dtype=jnp.int32))
    tab_s = _rope_tables(jnp.full((bs,), t_pos, jnp.int32))
    caches = tuple(jnp.transpose(c, (0, 1, 3, 4, 2)) for c in (cache_cmp_k, cache_cmp_v, cache_slc_k, cache_slc_v))

    xp = x_prompt.reshape(lp, D_MODEL)
    xs = x_sample.reshape(bs, D_MODEL)
    outs_p = [[] for _ in range(9)]
    outs_s = [[] for _ in range(9)]
    for l in range(depth):
        w = _layer_weights(l, params)
        xp, st_p = _prompt_layer(xp, mod[l, 0:bp], w, tab_p, rexp)
        states = (state_win_k[l], state_win_v[l], state_conv[l], state_ssm_conv[l], state_ssm[l])
        xs, st_s = _sample_layer(xs, mod[l, bp:bp + bs], w, tab_s, rexp, caches, l, states, page_table, t_pos)
        for k in range(9):
            outs_p[k].append(st_p[k])
            outs_s[k].append(st_s[k])
    y_prompt = _final_norm(xp, final_g[None, :], 512).reshape(bp, lp, D_MODEL)
    y_sample = _final_norm(xs, final_g[None, :], bs).reshape(bs, 1, D_MODEL)
    np_ = [jnp.stack(o) for o in outs_p]
    ns_ = [jnp.stack(o) for o in outs_s]
    return (y_prompt, y_sample, *np_, *ns_)
```
